```python
import math
import jax, jax.numpy as jnp
from jax import lax
import numpy as np

D_MODEL = 1024
BATCH = 16
SEQ = 2048
DEPTH = 4

N_A_LAYERS = DEPTH // 2
N_B_LAYERS = DEPTH - N_A_LAYERS

H_A = 16
QK_NOPE = 128
QK_ROPE = 64
QK_HEAD = QK_NOPE + QK_ROPE
V_HEAD = 128
Q_LORA = 256
KV_LORA = 128

DILATED_GROUPS = ((128, 1), (512, 4), (2048, 16))
N_GROUPS = len(DILATED_GROUPS)
H_B = 8
HEAD_DIM_B = 128

D_FF = 4 * D_MODEL

ROPE_THETA = 10000.0
Q_BLOCK = 128
NORM_EPS = 1e-6
NEG_INF = -1e30

kernel_name = "yoco_mla_dilated_window_hybrid"


def _rms_norm(x, g):
    xf = x.astype(jnp.float32)
    y = xf * lax.rsqrt(jnp.mean(xf * xf, axis=-1, keepdims=True) + NORM_EPS)
    return (y * g.astype(jnp.float32)).astype(x.dtype)


def _rope(x, positions):
    r = x.shape[-1]
    inv_freq = ROPE_THETA ** (-jnp.arange(0, r, 2, dtype=jnp.float32) / r)
    ang = positions.astype(jnp.float32)[..., None] * inv_freq
    cos = jnp.cos(ang)[:, :, None, :]
    sin = jnp.sin(ang)[:, :, None, :]
    x1, x2 = jnp.split(x.astype(jnp.float32), 2, axis=-1)
    out = jnp.concatenate([x1 * cos - x2 * sin, x2 * cos + x1 * sin], axis=-1)
    return out.astype(x.dtype)


def _causal_block_attention(q, k, v, scale):
    S = q.shape[1]
    outs = []
    for i in range(S // Q_BLOCK):
        lo, hi = i * Q_BLOCK, (i + 1) * Q_BLOCK
        s = jnp.einsum('bqhd,bkhd->bhqk', q[:, lo:hi], k[:, :hi]).astype(jnp.float32) * scale
        mask = np.arange(lo, hi)[:, None] >= np.arange(hi)[None, :]
        s = jnp.where(mask, s, NEG_INF)
        p = jax.nn.softmax(s, axis=-1).astype(v.dtype)
        outs.append(jnp.einsum('bhqk,bkhd->bqhd', p, v[:, :hi]))
    return jnp.concatenate(outs, axis=1)


def _mla(xn, positions, w_in, qa_norm, kva_norm, w_qb, w_kvb, q_norm, k_norm, w_o):
    B, S, _ = xn.shape
    lat = xn @ w_in
    c_q, c_kv, k_pe = jnp.split(lat, [Q_LORA, Q_LORA + KV_LORA], axis=-1)
    q = (_rms_norm(c_q, qa_norm) @ w_qb).reshape(B, S, H_A, QK_HEAD)
    kv = (_rms_norm(c_kv, kva_norm) @ w_kvb).reshape(B, S, H_A, QK_NOPE + V_HEAD)
    k_nope, v = jnp.split(kv, [QK_NOPE], axis=-1)
    k_pe = jnp.broadcast_to(k_pe[:, :, None, :], (B, S, H_A, QK_ROPE))
    k = jnp.concatenate([k_nope, k_pe], axis=-1)
    q = _rms_norm(q, q_norm)
    k = _rms_norm(k, k_norm)
    q = jnp.concatenate([q[..., :QK_NOPE], _rope(q[..., QK_NOPE:], positions)], axis=-1)
    k = jnp.concatenate([k[..., :QK_NOPE], _rope(k[..., QK_NOPE:], positions)], axis=-1)
    o = _causal_block_attention(q, k, v, QK_HEAD ** -0.5)
    return o.reshape(B, S, H_A * V_HEAD) @ w_o


def _sliding_window_attention(q, k, v, steps):
    N, L, H, D = q.shape
    bq = math.gcd(L, Q_BLOCK)
    nb = L // bq
    pad = ((0, 0), (steps, 0), (0, 0), (0, 0))
    kp = jnp.pad(k, pad)
    vp = jnp.pad(v, pad)
    idx = np.arange(nb)[:, None] * bq + np.arange(bq + steps)[None, :]
    kb = kp[:, idx]
    vb = vp[:, idx]
    qb = q.reshape(N, nb, bq, H, D)
    s = jnp.einsum('nbqhd,nbkhd->nbhqk', qb, kb).astype(jnp.float32) * (D ** -0.5)
    rel = steps + np.arange(bq)[:, None] - np.arange(bq + steps)[None, :]
    band = (rel >= 0) & (rel <= steps)
    valid = band[None, :, :] & (idx[:, None, :] >= steps)
    s = jnp.where(valid[None, :, None], s, NEG_INF)
    m = jnp.max(s, axis=-1, keepdims=True)
    p = jnp.exp(s - m)
    denom = jnp.sum(p, axis=-1)
    out = jnp.einsum('nbhqk,nbkhd->nbqhd', p, vb.astype(jnp.float32))
    out = out / jnp.swapaxes(denom, 2, 3)[..., None]
    lse = jnp.swapaxes(m[..., 0] + jnp.log(denom), 2, 3)
    return out.reshape(N, L, H, D), lse.reshape(N, L, H)


def _dilated_group(q, k, v, dilation, steps):
    B, S, H, D = q.shape
    L = S // dilation

    def to_res(t):
        return t.reshape(B, L, dilation, H, D).transpose(0, 2, 1, 3, 4).reshape(B * dilation, L, H, D)

    out, lse = _sliding_window_attention(to_res(q), to_res(k), to_res(v), steps)
    out = out.reshape(B, dilation, L, H, D).transpose(0, 2, 1, 3, 4).reshape(B, S, H, D)
    lse = lse.reshape(B, dilation, L, H).transpose(0, 2, 1, 3).reshape(B, S, H)
    return out, lse


def _shared_kv(x, positions, kv_norm, w_kv, k_norm_b):
    B, S, _ = x.shape
    kv = (_rms_norm(x, kv_norm) @ w_kv).reshape(B, S, 2, N_GROUPS, H_B, HEAD_DIM_B)
    k, v = kv[:, :, 0], kv[:, :, 1]
    k = _rms_norm(k, k_norm_b[:, None, :])
    k = _rope(k.reshape(B, S, N_GROUPS * H_B, HEAD_DIM_B), positions).reshape(B, S, N_GROUPS, H_B, HEAD_DIM_B)
    return k, v


def _dilated_mixture(xn, positions, k, v, w_q, q_norm, w_o):
    B, S, _ = xn.shape
    q = (xn @ w_q).reshape(B, S, N_GROUPS, H_B, HEAD_DIM_B)
    q = _rms_norm(q, q_norm[:, None, :])
    q = _rope(q.reshape(B, S, N_GROUPS * H_B, HEAD_DIM_B), positions).reshape(B, S, N_GROUPS, H_B, HEAD_DIM_B)
    outs, lses = [], []
    for g, (window, dilation) in enumerate(DILATED_GROUPS):
        o, lse = _dilated_group(q[:, :, g], k[:, :, g], v[:, :, g], dilation, window // dilation)
        outs.append(o)
        lses.append(lse)
    wts = jax.nn.softmax(jnp.stack(lses, axis=0), axis=0)
    o = jnp.sum(wts[..., None] * jnp.stack(outs, axis=0), axis=0).astype(xn.dtype)
    return o.reshape(B, S, H_B * HEAD_DIM_B) @ w_o


def _sq_relu_mlp(xn, w1, w2):
    h = jax.nn.relu(xn @ w1)
    return (h * h) @ w2


def setup_inputs(seed: int = 0) -> dict:
    key = jax.random.key(seed)
    ks = jax.random.split(key, 24)
    f32 = jnp.float32

    def w(k, shape, fan_in):
        return jax.random.normal(k, shape, f32) * fan_in ** -0.5

    def gain(k, shape):
        return 1.0 + 0.02 * jax.random.normal(k, shape, f32)

    x = jax.random.normal(ks[0], (BATCH, SEQ, D_MODEL), f32)
    start = jax.random.randint(ks[1], (BATCH, 1), 0, 1024, dtype=jnp.int32)
    positions = start + jnp.arange(SEQ, dtype=jnp.int32)[None, :]
    qkv_b = N_GROUPS * H_B * HEAD_DIM_B
    return {
        "x": x,
        "positions": positions,
        "attn_norm": gain(ks[2], (DEPTH, D_MODEL)),
        "mlp_norm": gain(ks[3], (DEPTH, D_MODEL)),
        "mla_w_in": w(ks[4], (N_A_LAYERS, D_MODEL, Q_LORA + KV_LORA + QK_ROPE), D_MODEL),
        "mla_qa_norm": gain(ks[5], (N_A_LAYERS, Q_LORA)),
        "mla_kva_norm": gain(ks[6], (N_A_LAYERS, KV_LORA)),
        "mla_w_qb": w(ks[7], (N_A_LAYERS, Q_LORA, H_A * QK_HEAD), Q_LORA),
        "mla_w_kvb": w(ks[8], (N_A_LAYERS, KV_LORA, H_A * (QK_NOPE + V_HEAD)), KV_LORA),
        "mla_q_norm": gain(ks[9], (N_A_LAYERS, QK_HEAD)),
        "mla_k_norm": gain(ks[10], (N_A_LAYERS, QK_HEAD)),
        "mla_w_o": w(ks[11], (N_A_LAYERS, H_A * V_HEAD, D_MODEL), H_A * V_HEAD),
        "kv_norm": gain(ks[12], (D_MODEL,)),
        "w_kv": w(ks[13], (D_MODEL, 2 * qkv_b), D_MODEL),
        "k_norm_b": gain(ks[14], (N_GROUPS, HEAD_DIM_B)),
        "w_q_b": w(ks[15], (N_B_LAYERS, D_MODEL, qkv_b), D_MODEL),
        "q_norm_b": gain(ks[16], (N_B_LAYERS, N_GROUPS, HEAD_DIM_B)),
        "w_o_b": w(ks[17], (N_B_LAYERS, H_B * HEAD_DIM_B, D_MODEL), H_B * HEAD_DIM_B),
        "mlp_w1": w(ks[18], (DEPTH, D_MODEL, D_FF), D_MODEL),
        "mlp_w2": w(ks[19], (DEPTH, D_FF, D_MODEL), D_FF),
    }


def reference(x, positions, attn_norm, mlp_norm, mla_w_in, mla_qa_norm, mla_kva_norm, mla_w_qb,
              mla_w_kvb, mla_q_norm, mla_k_norm, mla_w_o, kv_norm, w_kv, k_norm_b, w_q_b, q_norm_b,
              w_o_b, mlp_w1, mlp_w2):
    shared_k, shared_v = None, None
    for layer in range(DEPTH):
        if layer == N_A_LAYERS:
            shared_k, shared_v = _shared_kv(x, positions, kv_norm, w_kv, k_norm_b)
        xn = _rms_norm(x, attn_norm[layer])
        if layer < N_A_LAYERS:
            a = layer
            h = _mla(xn, positions, mla_w_in[a], mla_qa_norm[a], mla_kva_norm[a], mla_w_qb[a],
                     mla_w_kvb[a], mla_q_norm[a], mla_k_norm[a], mla_w_o[a])
        else:
            b = layer - N_A_LAYERS
            h = _dilated_mixture(xn, positions, shared_k, shared_v, w_q_b[b], q_norm_b[b], w_o_b[b])
        x = x + h
        x = x + _sq_relu_mlp(_rms_norm(x, mlp_norm[layer]), mlp_w1[layer], mlp_w2[layer])
    return x
```

```python
import functools

import jax
import jax.numpy as jnp
from jax import lax
from jax.experimental import pallas as pl
from jax.experimental.pallas import tpu as pltpu

D_MODEL = 1024
N_A_LAYERS = 2
N_B_LAYERS = 2
H_A = 16
QK_NOPE = 128
QK_ROPE = 64
QK_HEAD = QK_NOPE + QK_ROPE
V_HEAD = 128
Q_LORA = 256
KV_LORA = 128
DILATED_GROUPS = ((128, 1), (512, 4), (2048, 16))
N_GROUPS = 3
H_B = 8
HEAD_DIM_B = 128
D_FF = 4 * D_MODEL
ROPE_THETA = 10000.0
NORM_EPS = 1e-6
NEG_INF = -1e30

LANES = 128
QK_SLOT = 2 * LANES
VMEM_LIMIT = 56 * 1024 * 1024

BF16 = jnp.bfloat16
F32 = jnp.float32


def _params(semantics):
    return pltpu.CompilerParams(dimension_semantics=semantics, vmem_limit_bytes=VMEM_LIMIT)


def _rms(x, gain):
    ms = jnp.mean(x * x, axis=-1, keepdims=True)
    return x * lax.rsqrt(ms + NORM_EPS) * gain


def _rot_half(u):
    return pltpu.roll(u, LANES // 2, axis=1)


def _dot(a, b):
    return jnp.dot(a, b, preferred_element_type=F32)


def _dot_nt(a, b):
    return lax.dot_general(a, b, (((1,), (1,)), ((), ())), preferred_element_type=F32)


def _tables_kernel(pos_ref, fa_ref, ma_ref, sa_ref, fb_ref, sb_ref, cosa_ref, sina_ref, cosb_ref, sinb_ref):
    pos = pos_ref[...]
    ang_a = pos * fa_ref[...]
    cosa_ref[...] = jnp.cos(ang_a) * ma_ref[...]
    sina_ref[...] = jnp.sin(ang_a) * sa_ref[...]
    ang_b = pos * fb_ref[...]
    cosb_ref[...] = jnp.cos(ang_b)
    sinb_ref[...] = jnp.sin(ang_b) * sb_ref[...]


def _rope_tables(positions):
    T = positions.size
    tm = 1024
    pos = positions.astype(F32).reshape(T, 1)
    inv_a = ROPE_THETA ** (-jnp.arange(0, QK_ROPE, 2, dtype=F32) / QK_ROPE)
    inv_b = ROPE_THETA ** (-jnp.arange(0, HEAD_DIM_B, 2, dtype=F32) / HEAD_DIM_B)
    z32 = jnp.zeros((32,), F32)
    o32 = jnp.ones((32,), F32)
    fa = jnp.concatenate([inv_a, z32, inv_a, z32]).reshape(1, LANES)
    ma = jnp.concatenate([o32, z32, o32, z32]).reshape(1, LANES)
    sa = jnp.concatenate([-o32, z32, o32, z32]).reshape(1, LANES)
    fb = jnp.concatenate([inv_b, inv_b]).reshape(1, LANES)
    sb = jnp.concatenate([-jnp.ones((64,), F32), jnp.ones((64,), F32)]).reshape(1, LANES)
    row = pl.BlockSpec((1, LANES), lambda i: (0, 0))
    tab = pl.BlockSpec((tm, LANES), lambda i: (i, 0))
    shp = jax.ShapeDtypeStruct((T, LANES), F32)
    return pl.pallas_call(
        _tables_kernel,
        out_shape=(shp, shp, shp, shp),
        grid=(T // tm,),
        in_specs=[pl.BlockSpec((tm, 1), lambda i: (i, 0)), row, row, row, row, row],
        out_specs=(tab, tab, tab, tab),
        compiler_params=_params(("parallel",)),
        name="rope_tables",
    )(pos, fa, ma, sa, fb, sb)


def _mla_proj_kernel(x_ref, g_ref, win_ref, qa_ref, kva_ref, wqb_ref, wkb_ref, wvb_ref, qg_ref, kg_ref,
                     cos_ref, sin_ref, q_ref, k_ref, v_ref, *, scale):
    x = x_ref[...]
    xn = _rms(x, g_ref[...]).astype(BF16)
    lat = _dot(xn, win_ref[...])
    c_q = lat[:, :Q_LORA]
    c_kv = lat[:, Q_LORA:Q_LORA + KV_LORA]
    k_pe = lat[:, Q_LORA + KV_LORA:]
    cqn = _rms(c_q, qa_ref[...]).astype(BF16)
    ckvn = _rms(c_kv, kva_ref[...]).astype(BF16)
    cos = cos_ref[...]
    sin = sin_ref[...]
    qg = qg_ref[...]
    kg = kg_ref[...]
    qg_n, qg_pe = qg[:, :LANES], qg[:, LANES:]
    kg_n, kg_pe = kg[:, :LANES], kg[:, LANES:]

    v_ref[...] = _dot(ckvn, wvb_ref[...]).astype(BF16)

    kpe_ss = jnp.sum(k_pe * k_pe, axis=-1, keepdims=True)
    kpe_g = k_pe * kg_pe
    kpe_rot = kpe_g * cos + _rot_half(kpe_g) * sin

    for h in range(H_A):
        qh = _dot(cqn, wqb_ref[:, h * QK_SLOT:(h + 1) * QK_SLOT])
        qn, qp = qh[:, :LANES], qh[:, LANES:]
        ss = jnp.sum(qn * qn, axis=-1, keepdims=True) + jnp.sum(qp * qp, axis=-1, keepdims=True)
        rs = lax.rsqrt(ss * (1.0 / QK_HEAD) + NORM_EPS) * scale
        qpg = qp * qg_pe
        q_ref[:, h * QK_SLOT:h * QK_SLOT + LANES] = (qn * rs * qg_n).astype(BF16)
        q_ref[:, h * QK_SLOT + LANES:(h + 1) * QK_SLOT] = ((qpg * cos + _rot_half(qpg) * sin) * rs).astype(BF16)

        kn = _dot(ckvn, wkb_ref[:, h * LANES:(h + 1) * LANES])
        ssk = jnp.sum(kn * kn, axis=-1, keepdims=True) + kpe_ss
        rsk = lax.rsqrt(ssk * (1.0 / QK_HEAD) + NORM_EPS)
        k_ref[:, h * QK_SLOT:h * QK_SLOT + LANES] = (kn * rsk * kg_n).astype(BF16)
        k_ref[:, h * QK_SLOT + LANES:(h + 1) * QK_SLOT] = (kpe_rot * rsk).astype(BF16)


def _rope_tile_cols(a):
    z = jnp.zeros(a.shape[:-1] + (32,), a.dtype)
    return jnp.concatenate([a[..., :32], z, a[..., 32:], z], axis=-1)


def _mla_proj(x2d, gain, w_in, qa_norm, kva_norm, w_qb, w_kvb, q_norm, k_norm, cos_a, sin_a):
    T = x2d.shape[0]
    tm = 256
    w_in_p = jnp.concatenate(
        [w_in[:, :Q_LORA + KV_LORA], _rope_tile_cols(w_in[:, Q_LORA + KV_LORA:])], axis=-1).astype(BF16)
    wq = w_qb.reshape(Q_LORA, H_A, QK_HEAD)
    wq_p = jnp.concatenate([wq[..., :QK_NOPE], _rope_tile_cols(wq[..., QK_NOPE:])], axis=-1)
    wq_p = wq_p.reshape(Q_LORA, H_A * QK_SLOT).astype(BF16)
    wkv = w_kvb.reshape(KV_LORA, H_A, QK_NOPE + V_HEAD)
    wkb = wkv[..., :QK_NOPE].reshape(KV_LORA, H_A * QK_NOPE).astype(BF16)
    wvb = wkv[..., QK_NOPE:].reshape(KV_LORA, H_A * V_HEAD).astype(BF16)
    qg = jnp.concatenate([q_norm[:QK_NOPE], _rope_tile_cols(q_norm[QK_NOPE:])]).reshape(1, QK_SLOT)
    kg = jnp.concatenate([k_norm[:QK_NOPE], _rope_tile_cols(k_norm[QK_NOPE:])]).reshape(1, QK_SLOT)

    def const(shape):
        return pl.BlockSpec(shape, lambda i: (0, 0))

    def rows(width):
        return pl.BlockSpec((tm, width), lambda i: (i, 0))

    n_in = Q_LORA + KV_LORA + LANES
    return pl.pallas_call(
        functools.partial(_mla_proj_kernel, scale=QK_HEAD ** -0.5),
        out_shape=(jax.ShapeDtypeStruct((T, H_A * QK_SLOT), BF16),
                   jax.ShapeDtypeStruct((T, H_A * QK_SLOT), BF16),
                   jax.ShapeDtypeStruct((T, H_A * V_HEAD), BF16)),
        grid=(T // tm,),
        in_specs=[rows(D_MODEL), const((1, D_MODEL)), const((D_MODEL, n_in)), const((1, Q_LORA)),
                  const((1, KV_LORA)), const((Q_LORA, H_A * QK_SLOT)), const((KV_LORA, H_A * QK_NOPE)),
                  const((KV_LORA, H_A * V_HEAD)), const((1, QK_SLOT)), const((1, QK_SLOT)),
                  rows(LANES), rows(LANES)],
        out_specs=(rows(H_A * QK_SLOT), rows(H_A * QK_SLOT), rows(H_A * V_HEAD)),
        compiler_params=_params(("parallel",)),
        name="mla_proj",
    )(x2d, gain.reshape(1, D_MODEL), w_in_p, qa_norm.reshape(1, Q_LORA), kva_norm.reshape(1, KV_LORA),
      wq_p, wkb, wvb, qg, kg, cos_a, sin_a)


def _flash_kernel(q_ref, k_ref, v_ref, o_ref, *, tq, tk):
    i = pl.program_id(2)
    q = q_ref[...]

    def step(j, carry, masked):
        m, l, acc = carry
        off = pl.multiple_of(j * tk, tk)
        k = k_ref[pl.ds(off, tk), :]
        v = v_ref[pl.ds(off, tk), :]
        s = _dot_nt(q, k)
        if masked:
            row = lax.broadcasted_iota(jnp.int32, (tq, tk), 0)
            col = lax.broadcasted_iota(jnp.int32, (tq, tk), 1)
            s = jnp.where(row >= col, s, NEG_INF)
        m_new = jnp.maximum(m, jnp.max(s, axis=-1, keepdims=True))
        p = jnp.exp(s - m_new)
        alpha = jnp.exp(m - m_new)
        l = alpha * l + jnp.sum(p, axis=-1, keepdims=True)
        acc = alpha * acc + _dot(p.astype(BF16), v)
        return m_new, l, acc

    init = (jnp.full((tq, 1), NEG_INF, F32), jnp.zeros((tq, 1), F32), jnp.zeros((tq, V_HEAD), F32))
    carry = lax.fori_loop(0, i * (tq // tk), lambda j, c: step(j, c, False), init)
    m, l, acc = step(i * (tq // tk), carry, True)
    o_ref[...] = (acc / l).astype(o_ref.dtype)


def _mla_attention(q, k, v, batch, seq):
    tq = tk = 512
    nq = seq // tq
    T = q.shape[0]
    return pl.pallas_call(
        functools.partial(_flash_kernel, tq=tq, tk=tk),
        out_shape=jax.ShapeDtypeStruct((T, H_A * V_HEAD), BF16),
        grid=(batch, H_A, nq),
        in_specs=[pl.BlockSpec((tq, QK_SLOT), lambda b, h, i: (b * nq + i, h)),
                  pl.BlockSpec((seq, QK_SLOT), lambda b, h, i: (b, h)),
                  pl.BlockSpec((seq, V_HEAD), lambda b, h, i: (b, h))],
        out_specs=pl.BlockSpec((tq, V_HEAD), lambda b, h, i: (b * nq + i, h)),
        compiler_params=_params(("parallel", "parallel", "arbitrary")),
        name="mla_flash",
    )(q, k, v)


def _out_proj_kernel(o_ref, w_ref, x_ref, y_ref):
    y_ref[...] = x_ref[...] + _dot(o_ref[...], w_ref[...])


def _out_proj(o, w_o, x2d):
    T, K = o.shape
    tm = 512
    return pl.pallas_call(
        _out_proj_kernel,
        out_shape=jax.ShapeDtypeStruct((T, D_MODEL), F32),
        grid=(T // tm,),
        in_specs=[pl.BlockSpec((tm, K), lambda i: (i, 0)),
                  pl.BlockSpec((K, D_MODEL), lambda i: (0, 0)),
                  pl.BlockSpec((tm, D_MODEL), lambda i: (i, 0))],
        out_specs=pl.BlockSpec((tm, D_MODEL), lambda i: (i, 0)),
        compiler_params=_params(("parallel",)),
        name="out_proj",
    )(o, w_o.astype(BF16), x2d)


def _mlp_kernel(x_ref, g_ref, w1_ref, w2_ref, y_ref, xn_ref, acc_ref):
    f = pl.program_id(1)

    @pl.when(f == 0)
    def _():
        xn_ref[...] = _rms(x_ref[...], g_ref[...]).astype(BF16)
        acc_ref[...] = jnp.zeros_like(acc_ref)

    h = jnp.maximum(_dot(xn_ref[...], w1_ref[...]), 0.0)
    acc_ref[...] += _dot((h * h).astype(BF16), w2_ref[...])

    @pl.when(f == pl.num_programs(1) - 1)
    def _():
        y_ref[...] = x_ref[...] + acc_ref[...]


def _mlp(x2d, gain, w1, w2):
    T = x2d.shape[0]
    tm, tf = 512, 1024
    return pl.pallas_call(
        _mlp_kernel,
        out_shape=jax.ShapeDtypeStruct((T, D_MODEL), F32),
        grid=(T // tm, D_FF // tf),
        in_specs=[pl.BlockSpec((tm, D_MODEL), lambda i, f: (i, 0)),
                  pl.BlockSpec((1, D_MODEL), lambda i, f: (0, 0)),
                  pl.BlockSpec((D_MODEL, tf), lambda i, f: (0, f)),
                  pl.BlockSpec((tf, D_MODEL), lambda i, f: (f, 0))],
        out_specs=pl.BlockSpec((tm, D_MODEL), lambda i, f: (i, 0)),
        scratch_shapes=[pltpu.VMEM((tm, D_MODEL), BF16), pltpu.VMEM((tm, D_MODEL), F32)],
        compiler_params=_params(("parallel", "arbitrary")),
        name="mlp",
    )(x2d, gain.reshape(1, D_MODEL), w1.astype(BF16), w2.astype(BF16))


def _head_proj_kernel(x_ref, g_ref, w_ref, hg_ref, cos_ref, sin_ref, y_ref, xn_ref, *, n_rope_blocks, scale):
    j = pl.program_id(1)

    @pl.when(j == 0)
    def _():
        xn_ref[...] = _rms(x_ref[...], g_ref[...]).astype(BF16)

    y = _dot(xn_ref[...], w_ref[...])
    tn = y.shape[1]

    @pl.when(j < n_rope_blocks)
    def _():
        cos = cos_ref[...]
        sin = sin_ref[...]
        hg = hg_ref[...]
        for h in range(tn // LANES):
            yh = y[:, h * LANES:(h + 1) * LANES]
            rs = lax.rsqrt(jnp.mean(yh * yh, axis=-1, keepdims=True) + NORM_EPS)
            yg = yh * hg[:, h * LANES:(h + 1) * LANES]
            y_ref[:, h * LANES:(h + 1) * LANES] = ((yg * cos + _rot_half(yg) * sin) * (rs * scale)).astype(BF16)

    @pl.when(j >= n_rope_blocks)
    def _():
        y_ref[...] = y.astype(BF16)


def _head_proj(x2d, gain, w, head_gain, n_rope_cols, scale, cos_b, sin_b):
    T = x2d.shape[0]
    N = w.shape[1]
    tm, tn = 512, 1024
    hg = jnp.concatenate([head_gain.reshape(1, n_rope_cols), jnp.ones((1, N - n_rope_cols), F32)], axis=-1)
    return pl.pallas_call(
        functools.partial(_head_proj_kernel, n_rope_blocks=n_rope_cols // tn, scale=scale),
        out_shape=jax.ShapeDtypeStruct((T, N), BF16),
        grid=(T // tm, N // tn),
        in_specs=[pl.BlockSpec((tm, D_MODEL), lambda i, j: (i, 0)),
                  pl.BlockSpec((1, D_MODEL), lambda i, j: (0, 0)),
                  pl.BlockSpec((D_MODEL, tn), lambda i, j: (0, j)),
                  pl.BlockSpec((1, tn), lambda i, j: (0, j)),
                  pl.BlockSpec((tm, LANES), lambda i, j: (i, 0)),
                  pl.BlockSpec((tm, LANES), lambda i, j: (i, 0))],
        out_specs=pl.BlockSpec((tm, tn), lambda i, j: (i, j)),
        scratch_shapes=[pltpu.VMEM((tm, D_MODEL), BF16)],
        compiler_params=_params(("parallel", "arbitrary")),
        name="head_proj",
    )(x2d, gain.reshape(1, D_MODEL), w.astype(BF16), hg, cos_b, sin_b)


def _band_kernel(*refs, tq, has_prev):
    if has_prev:
        q_ref, kc_ref, vc_ref, kp_ref, vp_ref, o_ref, lse_ref = refs
    else:
        q_ref, kc_ref, vc_ref, o_ref, lse_ref = refs
    i = pl.program_id(2)
    row = lax.broadcasted_iota(jnp.int32, (tq, tq), 0)
    col = lax.broadcasted_iota(jnp.int32, (tq, tq), 1)
    cur_ok = row >= col
    lane_head = lax.broadcasted_iota(jnp.int32, (tq, LANES), 1) // (LANES // H_B)
    lse_tile = jnp.zeros((tq, LANES), F32)
    for h in range(H_B):
        hs = slice(h * HEAD_DIM_B, (h + 1) * HEAD_DIM_B)
        q = q_ref[0, :, hs]
        s_c = jnp.where(cur_ok, _dot_nt(q, kc_ref[0, :, hs]), NEG_INF)
        m = jnp.max(s_c, axis=-1, keepdims=True)
        if has_prev:
            prev_ok = jnp.logical_and(col >= row, i > 0)
            s_p = jnp.where(prev_ok, _dot_nt(q, kp_ref[0, :, hs]), NEG_INF)
            m = jnp.maximum(m, jnp.max(s_p, axis=-1, keepdims=True))
        p_c = jnp.exp(s_c - m)
        denom = jnp.sum(p_c, axis=-1, keepdims=True)
        out = _dot(p_c.astype(BF16), vc_ref[0, :, hs])
        if has_prev:
            p_p = jnp.exp(s_p - m)
            denom = denom + jnp.sum(p_p, axis=-1, keepdims=True)
            out = out + _dot(p_p.astype(BF16), vp_ref[0, :, hs])
        o_ref[0, :, hs] = (out / denom).astype(o_ref.dtype)
        lse_tile = jnp.where(lane_head == h, m + jnp.log(denom), lse_tile)
    lse_ref[0] = lse_tile


def _band_attention(q, kv, g, dilation, window, batch, seq):
    steps = window // dilation
    L = seq // dilation
    tq = steps
    assert tq == 128 and L % tq == 0
    nb = L // tq
    has_prev = nb > 1
    T = q.shape[0]
    C = H_B * HEAD_DIM_B
    qv = q.reshape(batch, L, dilation * N_GROUPS * C)
    kvv = kv.reshape(batch, L, dilation * 2 * N_GROUPS * C)
    blk = (1, tq, C)
    in_specs = [pl.BlockSpec(blk, lambda b, r, i: (b, i, r * N_GROUPS + g)),
                pl.BlockSpec(blk, lambda b, r, i: (b, i, r * 2 * N_GROUPS + g)),
                pl.BlockSpec(blk, lambda b, r, i: (b, i, r * 2 * N_GROUPS + N_GROUPS + g))]
    args = [qv, kvv, kvv]
    if has_prev:
        in_specs += [pl.BlockSpec(blk, lambda b, r, i: (b, jnp.maximum(i - 1, 0), r * 2 * N_GROUPS + g)),
                     pl.BlockSpec(blk, lambda b, r, i: (b, jnp.maximum(i - 1, 0), r * 2 * N_GROUPS + N_GROUPS + g))]
        args += [kvv, kvv]
    out, lse = pl.pallas_call(
        functools.partial(_band_kernel, tq=tq, has_prev=has_prev),
        out_shape=(jax.ShapeDtypeStruct((batch, L, dilation * C), BF16),
                   jax.ShapeDtypeStruct((batch, L, dilation * LANES), F32)),
        grid=(batch, dilation, nb),
        in_specs=in_specs,
        out_specs=(pl.BlockSpec(blk, lambda b, r, i: (b, i, r)),
                   pl.BlockSpec((1, tq, LANES), lambda b, r, i: (b, i, r))),
        compiler_params=_params(("parallel", "parallel", "parallel")),
        name=f"band_attn_d{dilation}",
    )(*args)
    return out.reshape(T, C), lse.reshape(T, LANES)


def _combine_proj_kernel(o0_ref, o1_ref, o2_ref, l0_ref, l1_ref, l2_ref, w_ref, x_ref, y_ref, o_scr):
    l0, l1, l2 = l0_ref[...], l1_ref[...], l2_ref[...]
    mx = jnp.maximum(jnp.maximum(l0, l1), l2)
    e0, e1, e2 = jnp.exp(l0 - mx), jnp.exp(l1 - mx), jnp.exp(l2 - mx)
    inv = 1.0 / (e0 + e1 + e2)
    w0, w1, w2 = e0 * inv, e1 * inv, e2 * inv
    lanes_per_head = LANES // H_B
    for h in range(H_B):
        hs = slice(h * HEAD_DIM_B, (h + 1) * HEAD_DIM_B)
        c = h * lanes_per_head
        o = (w0[:, c:c + 1] * o0_ref[:, hs].astype(F32)
             + w1[:, c:c + 1] * o1_ref[:, hs].astype(F32)
             + w2[:, c:c + 1] * o2_ref[:, hs].astype(F32))
        o_scr[:, hs] = o.astype(BF16)
    y_ref[...] = x_ref[...] + _dot(o_scr[...], w_ref[...])


def _combine_proj(outs, lses, w_o, x2d):
    T = x2d.shape[0]
    C = H_B * HEAD_DIM_B
    tm = 512
    o_spec = pl.BlockSpec((tm, C), lambda i: (i, 0))
    l_spec = pl.BlockSpec((tm, LANES), lambda i: (i, 0))
    return pl.pallas_call(
        _combine_proj_kernel,
        out_shape=jax.ShapeDtypeStruct((T, D_MODEL), F32),
        grid=(T // tm,),
        in_specs=[o_spec, o_spec, o_spec, l_spec, l_spec, l_spec,
                  pl.BlockSpec((C, D_MODEL), lambda i: (0, 0)),
                  pl.BlockSpec((tm, D_MODEL), lambda i: (i, 0))],
        out_specs=pl.BlockSpec((tm, D_MODEL), lambda i: (i, 0)),
        scratch_shapes=[pltpu.VMEM((tm, C), BF16)],
        compiler_params=_params(("parallel",)),
        name="combine_proj",
    )(*outs, *lses, w_o.astype(BF16), x2d)


def kernel(x, positions, attn_norm, mlp_norm, mla_w_in, mla_qa_norm, mla_kva_norm, mla_w_qb, mla_w_kvb,
           mla_q_norm, mla_k_norm, mla_w_o, kv_norm, w_kv, k_norm_b, w_q_b, q_norm_b, w_o_b, mlp_w1, mlp_w2):
    B, S, D = x.shape
    T = B * S
    cos_a, sin_a, cos_b, sin_b = _rope_tables(positions)
    h = x.reshape(T, D)
    n_rope = N_GROUPS * H_B * HEAD_DIM_B

    def head_gains(gn):
        return jnp.broadcast_to(gn[:, None, :], (N_GROUPS, H_B, HEAD_DIM_B)).reshape(1, n_rope)

    for a in range(N_A_LAYERS):
        q, k, v = _mla_proj(h, attn_norm[a], mla_w_in[a], mla_qa_norm[a], mla_kva_norm[a], mla_w_qb[a],
                            mla_w_kvb[a], mla_q_norm[a], mla_k_norm[a], cos_a, sin_a)
        o = _mla_attention(q, k, v, B, S)
        h = _out_proj(o, mla_w_o[a], h)
        h = _mlp(h, mlp_norm[a], mlp_w1[a], mlp_w2[a])

    kv = _head_proj(h, kv_norm, w_kv, head_gains(k_norm_b), n_rope, 1.0, cos_b, sin_b)

    for b in range(N_B_LAYERS):
        layer = N_A_LAYERS + b
        q = _head_proj(h, attn_norm[layer], w_q_b[b], head_gains(q_norm_b[b]), n_rope,
                       HEAD_DIM_B ** -0.5, cos_b, sin_b)
        outs, lses = [], []
        for g, (window, dilation) in enumerate(DILATED_GROUPS):
            o, lse = _band_attention(q, kv, g, dilation, window, B, S)
            outs.append(o)
            lses.append(lse)
        h = _combine_proj(outs, lses, w_o_b[b], h)
        h = _mlp(h, mlp_norm[layer], mlp_w1[layer], mlp_w2[layer])

    return h.reshape(B, S, D)
```

```python
import functools

import jax
import jax.numpy as jnp
from jax import lax
from jax.experimental import pallas as pl
from jax.experimental.pallas import tpu as pltpu

D_MODEL = 1024
N_A_LAYERS = 2
N_B_LAYERS = 2
H_A = 16
QK_NOPE = 128
QK_ROPE = 64
QK_HEAD = QK_NOPE + QK_ROPE
V_HEAD = 128
Q_LORA = 256
KV_LORA = 128
DILATED_GROUPS = ((128, 1), (512, 4), (2048, 16))
N_GROUPS = 3
H_B = 8
HEAD_DIM_B = 128
C_B = H_B * HEAD_DIM_B
D_FF = 4 * D_MODEL
ROPE_THETA = 10000.0
NORM_EPS = 1e-6
NEG_INF = -1e30
LOG2E = 1.4426950408889634

LANES = 128
QK_SLOT = 2 * LANES
VMEM_LIMIT = 56 * 1024 * 1024
BAND = 128

BF16 = jnp.bfloat16
F32 = jnp.float32


def _params(semantics):
    return pltpu.CompilerParams(dimension_semantics=semantics, vmem_limit_bytes=VMEM_LIMIT)


def _rms(x, gain):
    ms = jnp.mean(x * x, axis=-1, keepdims=True)
    return x * lax.rsqrt(ms + NORM_EPS) * gain


def _rot_half(u):
    return pltpu.roll(u, LANES // 2, axis=1)


def _dot(a, b):
    return jnp.dot(a, b, preferred_element_type=F32)


def _dot_nt(a, b):
    return lax.dot_general(a, b, (((1,), (1,)), ((), ())), preferred_element_type=F32)


def _tables_kernel(pos_ref, fa_ref, ma_ref, sa_ref, fb_ref, sb_ref, cosa_ref, sina_ref, cosb_ref, sinb_ref):
    pos = pos_ref[...]
    ang_a = pos * fa_ref[...]
    cosa_ref[...] = jnp.cos(ang_a) * ma_ref[...]
    sina_ref[...] = jnp.sin(ang_a) * sa_ref[...]
    ang_b = pos * fb_ref[...]
    cosb_ref[...] = jnp.cos(ang_b)
    sinb_ref[...] = jnp.sin(ang_b) * sb_ref[...]


def _rope_tables(positions):
    T = positions.size
    tm = 1024
    pos = positions.astype(F32).reshape(T, 1)
    inv_a = ROPE_THETA ** (-jnp.arange(0, QK_ROPE, 2, dtype=F32) / QK_ROPE)
    inv_b = ROPE_THETA ** (-jnp.arange(0, HEAD_DIM_B, 2, dtype=F32) / HEAD_DIM_B)
    z32 = jnp.zeros((32,), F32)
    o32 = jnp.ones((32,), F32)
    fa = jnp.concatenate([inv_a, z32, inv_a, z32]).reshape(1, LANES)
    ma = jnp.concatenate([o32, z32, o32, z32]).reshape(1, LANES)
    sa = jnp.concatenate([-o32, z32, o32, z32]).reshape(1, LANES)
    fb = jnp.concatenate([inv_b, inv_b]).reshape(1, LANES)
    sb = jnp.concatenate([-jnp.ones((64,), F32), jnp.ones((64,), F32)]).reshape(1, LANES)
    row = pl.BlockSpec((1, LANES), lambda i: (0, 0))
    tab = pl.BlockSpec((tm, LANES), lambda i: (i, 0))
    shp = jax.ShapeDtypeStruct((T, LANES), F32)
    return pl.pallas_call(
        _tables_kernel,
        out_shape=(shp, shp, shp, shp),
        grid=(T // tm,),
        in_specs=[pl.BlockSpec((tm, 1), lambda i: (i, 0)), row, row, row, row, row],
        out_specs=(tab, tab, tab, tab),
        compiler_params=_params(("parallel",)),
        name="rope_tables",
    )(pos, fa, ma, sa, fb, sb)


def _mla_proj_kernel(x_ref, g_ref, win_ref, qa_ref, kva_ref, wqb_ref, wkb_ref, wvb_ref, qg_ref, kg_ref,
                     cos_ref, sin_ref, q_ref, k_ref, v_ref, *, scale):
    x = x_ref[...]
    xn = _rms(x, g_ref[...]).astype(BF16)
    lat = _dot(xn, win_ref[...])
    c_q = lat[:, :Q_LORA]
    c_kv = lat[:, Q_LORA:Q_LORA + KV_LORA]
    k_pe = lat[:, Q_LORA + KV_LORA:]
    cqn = _rms(c_q, qa_ref[...]).astype(BF16)
    ckvn = _rms(c_kv, kva_ref[...]).astype(BF16)
    cos = cos_ref[...]
    sin = sin_ref[...]
    qg = qg_ref[...]
    kg = kg_ref[...]
    qg_n, qg_pe = qg[:, :LANES], qg[:, LANES:]
    kg_n, kg_pe = kg[:, :LANES], kg[:, LANES:]

    v_ref[...] = _dot(ckvn, wvb_ref[...]).astype(BF16)

    kpe_ss = jnp.sum(k_pe * k_pe, axis=-1, keepdims=True)
    kpe_g = k_pe * kg_pe
    kpe_rot = kpe_g * cos + _rot_half(kpe_g) * sin

    for h in range(H_A):
        qh = _dot(cqn, wqb_ref[:, h * QK_SLOT:(h + 1) * QK_SLOT])
        qn, qp = qh[:, :LANES], qh[:, LANES:]
        ss = jnp.sum(qn * qn, axis=-1, keepdims=True) + jnp.sum(qp * qp, axis=-1, keepdims=True)
        rs = lax.rsqrt(ss * (1.0 / QK_HEAD) + NORM_EPS) * scale
        qpg = qp * qg_pe
        q_ref[:, h * QK_SLOT:h * QK_SLOT + LANES] = (qn * rs * qg_n).astype(BF16)
        q_ref[:, h * QK_SLOT + LANES:(h + 1) * QK_SLOT] = ((qpg * cos + _rot_half(qpg) * sin) * rs).astype(BF16)

        kn = _dot(ckvn, wkb_ref[:, h * LANES:(h + 1) * LANES])
        ssk = jnp.sum(kn * kn, axis=-1, keepdims=True) + kpe_ss
        rsk = lax.rsqrt(ssk * (1.0 / QK_HEAD) + NORM_EPS)
        k_ref[:, h * QK_SLOT:h * QK_SLOT + LANES] = (kn * rsk * kg_n).astype(BF16)
        k_ref[:, h * QK_SLOT + LANES:(h + 1) * QK_SLOT] = (kpe_rot * rsk).astype(BF16)


def _rope_tile_cols(a):
    z = jnp.zeros(a.shape[:-1] + (32,), a.dtype)
    return jnp.concatenate([a[..., :32], z, a[..., 32:], z], axis=-1)


def _mla_proj(x2d, gain, w_in, qa_norm, kva_norm, w_qb, w_kvb, q_norm, k_norm, cos_a, sin_a):
    T = x2d.shape[0]
    tm = 256
    w_in_p = jnp.concatenate(
        [w_in[:, :Q_LORA + KV_LORA], _rope_tile_cols(w_in[:, Q_LORA + KV_LORA:])], axis=-1).astype(BF16)
    wq = w_qb.reshape(Q_LORA, H_A, QK_HEAD)
    wq_p = jnp.concatenate([wq[..., :QK_NOPE], _rope_tile_cols(wq[..., QK_NOPE:])], axis=-1)
    wq_p = wq_p.reshape(Q_LORA, H_A * QK_SLOT).astype(BF16)
    wkv = w_kvb.reshape(KV_LORA, H_A, QK_NOPE + V_HEAD)
    wkb = wkv[..., :QK_NOPE].reshape(KV_LORA, H_A * QK_NOPE).astype(BF16)
    wvb = wkv[..., QK_NOPE:].reshape(KV_LORA, H_A * V_HEAD).astype(BF16)
    qg = jnp.concatenate([q_norm[:QK_NOPE], _rope_tile_cols(q_norm[QK_NOPE:])]).reshape(1, QK_SLOT)
    kg = jnp.concatenate([k_norm[:QK_NOPE], _rope_tile_cols(k_norm[QK_NOPE:])]).reshape(1, QK_SLOT)

    def const(shape):
        return pl.BlockSpec(shape, lambda i: (0, 0))

    def rows(width):
        return pl.BlockSpec((tm, width), lambda i: (i, 0))

    n_in = Q_LORA + KV_LORA + LANES
    return pl.pallas_call(
        functools.partial(_mla_proj_kernel, scale=QK_HEAD ** -0.5 * LOG2E),
        out_shape=(jax.ShapeDtypeStruct((T, H_A * QK_SLOT), BF16),
                   jax.ShapeDtypeStruct((T, H_A * QK_SLOT), BF16),
                   jax.ShapeDtypeStruct((T, H_A * V_HEAD), BF16)),
        grid=(T // tm,),
        in_specs=[rows(D_MODEL), const((1, D_MODEL)), const((D_MODEL, n_in)), const((1, Q_LORA)),
                  const((1, KV_LORA)), const((Q_LORA, H_A * QK_SLOT)), const((KV_LORA, H_A * QK_NOPE)),
                  const((KV_LORA, H_A * V_HEAD)), const((1, QK_SLOT)), const((1, QK_SLOT)),
                  rows(LANES), rows(LANES)],
        out_specs=(rows(H_A * QK_SLOT), rows(H_A * QK_SLOT), rows(H_A * V_HEAD)),
        compiler_params=_params(("parallel",)),
        name="mla_proj",
    )(x2d, gain.reshape(1, D_MODEL), w_in_p, qa_norm.reshape(1, Q_LORA), kva_norm.reshape(1, KV_LORA),
      wq_p, wkb, wvb, qg, kg, cos_a, sin_a)


FLASH_HEADS = 2


def _flash_kernel(q_ref, k_ref, v_ref, o_ref, *, tq, tk):
    i = pl.program_id(2)
    qs = [q_ref[:, h * QK_SLOT:(h + 1) * QK_SLOT] for h in range(FLASH_HEADS)]

    def step(j, carry, masked):
        off = pl.multiple_of(j * tk, tk)
        ss = []
        for h in range(FLASH_HEADS):
            s = _dot_nt(qs[h], k_ref[pl.ds(off, tk), h * QK_SLOT:(h + 1) * QK_SLOT])
            if masked:
                row = lax.broadcasted_iota(jnp.int32, (tq, tk), 0)
                col = lax.broadcasted_iota(jnp.int32, (tq, tk), 1)
                s = jnp.where(row >= col, s, NEG_INF)
            ss.append(s)
        new = []
        for h in range(FLASH_HEADS):
            m, l, acc = carry[h]
            s = ss[h]
            m_new = jnp.maximum(m, jnp.max(s, axis=-1, keepdims=True))
            p = jnp.exp2(s - m_new)
            alpha = jnp.exp2(m - m_new)
            l = alpha * l + jnp.sum(p, axis=-1, keepdims=True)
            acc = alpha * acc + _dot(p.astype(BF16), v_ref[pl.ds(off, tk), h * V_HEAD:(h + 1) * V_HEAD])
            new.append((m_new, l, acc))
        return tuple(new)

    init = tuple((jnp.full((tq, 1), NEG_INF, F32), jnp.zeros((tq, 1), F32), jnp.zeros((tq, V_HEAD), F32))
                 for _ in range(FLASH_HEADS))
    carry = lax.fori_loop(0, i * (tq // tk), lambda j, c: step(j, c, False), init)
    carry = step(i * (tq // tk), carry, True)
    for h in range(FLASH_HEADS):
        m, l, acc = carry[h]
        o_ref[:, h * V_HEAD:(h + 1) * V_HEAD] = (acc / l).astype(o_ref.dtype)


def _mla_attention(q, k, v, batch, seq):
    tq = tk = 512
    nq = seq // tq
    T = q.shape[0]
    hh = FLASH_HEADS
    return pl.pallas_call(
        functools.partial(_flash_kernel, tq=tq, tk=tk),
        out_shape=jax.ShapeDtypeStruct((T, H_A * V_HEAD), BF16),
        grid=(batch, H_A // hh, nq),
        in_specs=[pl.BlockSpec((tq, hh * QK_SLOT), lambda b, h, i: (b * nq + i, h)),
                  pl.BlockSpec((seq, hh * QK_SLOT), lambda b, h, i: (b, h)),
                  pl.BlockSpec((seq, hh * V_HEAD), lambda b, h, i: (b, h))],
        out_specs=pl.BlockSpec((tq, hh * V_HEAD), lambda b, h, i: (b * nq + i, h)),
        compiler_params=_params(("parallel", "parallel", "arbitrary")),
        name="mla_flash",
    )(q, k, v)


def _out_proj_kernel(o_ref, w_ref, x_ref, y_ref):
    y_ref[...] = x_ref[...] + _dot(o_ref[...], w_ref[...])


def _out_proj(o, w_o, x2d):
    T, K = o.shape
    tm = 512
    return pl.pallas_call(
        _out_proj_kernel,
        out_shape=jax.ShapeDtypeStruct((T, D_MODEL), F32),
        grid=(T // tm,),
        in_specs=[pl.BlockSpec((tm, K), lambda i: (i, 0)),
                  pl.BlockSpec((K, D_MODEL), lambda i: (0, 0)),
                  pl.BlockSpec((tm, D_MODEL), lambda i: (i, 0))],
        out_specs=pl.BlockSpec((tm, D_MODEL), lambda i: (i, 0)),
        compiler_params=_params(("parallel",)),
        name="out_proj",
    )(o, w_o.astype(BF16), x2d)


def _mlp_kernel(x_ref, g_ref, w1_ref, w2_ref, y_ref, xn_ref, acc_ref):
    f = pl.program_id(1)

    @pl.when(f == 0)
    def _():
        xn_ref[...] = _rms(x_ref[...], g_ref[...]).astype(BF16)
        acc_ref[...] = jnp.zeros_like(acc_ref)

    h = jnp.maximum(_dot(xn_ref[...], w1_ref[...]), 0.0)
    acc_ref[...] += _dot((h * h).astype(BF16), w2_ref[...])

    @pl.when(f == pl.num_programs(1) - 1)
    def _():
        y_ref[...] = x_ref[...] + acc_ref[...]


def _mlp(x2d, gain, w1, w2):
    T = x2d.shape[0]
    tm, tf = 512, 1024
    return pl.pallas_call(
        _mlp_kernel,
        out_shape=jax.ShapeDtypeStruct((T, D_MODEL), F32),
        grid=(T // tm, D_FF // tf),
        in_specs=[pl.BlockSpec((tm, D_MODEL), lambda i, f: (i, 0)),
                  pl.BlockSpec((1, D_MODEL), lambda i, f: (0, 0)),
                  pl.BlockSpec((D_MODEL, tf), lambda i, f: (0, f)),
                  pl.BlockSpec((tf, D_MODEL), lambda i, f: (f, 0))],
        out_specs=pl.BlockSpec((tm, D_MODEL), lambda i, f: (i, 0)),
        scratch_shapes=[pltpu.VMEM((tm, D_MODEL), BF16), pltpu.VMEM((tm, D_MODEL), F32)],
        compiler_params=_params(("parallel", "arbitrary")),
        name="mlp",
    )(x2d, gain.reshape(1, D_MODEL), w1.astype(BF16), w2.astype(BF16))


def _scatter_rows(scr_ref, out_ref, dilation):
    rows = scr_ref.shape[1] // dilation
    for r in range(dilation):
        for h in range(H_B):
            hs = slice(h * HEAD_DIM_B, (h + 1) * HEAD_DIM_B)
            if dilation == 1:
                out_ref[0, r, :, hs] = scr_ref[h].astype(out_ref.dtype)
            else:
                out_ref[0, r, :, hs] = scr_ref[h, pl.ds(r, rows, stride=dilation), :].astype(out_ref.dtype)


def _group_proj_kernel(*refs, has_v, scale):
    if has_v:
        x_ref, g_ref, wk_ref, wv_ref, hg_ref, cos_ref, sin_ref = refs[:7]
        outs = refs[7:13]
        xn_ref, scr_ref = refs[13:]
    else:
        x_ref, g_ref, wk_ref, hg_ref, cos_ref, sin_ref = refs[:6]
        outs = refs[6:9]
        xn_ref, scr_ref = refs[9:]
    grp = pl.program_id(1)

    @pl.when(grp == 0)
    def _():
        xn_ref[...] = _rms(x_ref[...], g_ref[...]).astype(BF16)

    y = _dot(xn_ref[...], wk_ref[...])
    cos = cos_ref[...]
    sin = sin_ref[...]
    hg = hg_ref[...]
    for h in range(H_B):
        hs = slice(h * HEAD_DIM_B, (h + 1) * HEAD_DIM_B)
        yh = y[:, hs]
        rs = lax.rsqrt(jnp.mean(yh * yh, axis=-1, keepdims=True) + NORM_EPS)
        yg = yh * hg[:, hs]
        scr_ref[h] = (yg * cos + _rot_half(yg) * sin) * (rs * scale)
    for gi, (_, dilation) in enumerate(DILATED_GROUPS):
        @pl.when(grp == gi)
        def _(gi=gi, dilation=dilation):
            _scatter_rows(scr_ref, outs[gi], dilation)

    if has_v:
        yv = _dot(xn_ref[...], wv_ref[...])
        for h in range(H_B):
            scr_ref[h] = yv[:, h * HEAD_DIM_B:(h + 1) * HEAD_DIM_B]
        for gi, (_, dilation) in enumerate(DILATED_GROUPS):
            @pl.when(grp == gi)
            def _(gi=gi, dilation=dilation):
                _scatter_rows(scr_ref, outs[N_GROUPS + gi], dilation)


def _group_proj(x2d, gain, w, head_gain, has_v, scale, cos_b, sin_b, batch, seq):
    tm = 512
    nt = seq // tm
    n_rope = N_GROUPS * C_B
    w = w.astype(BF16)
    x_spec = pl.BlockSpec((tm, D_MODEL), lambda i, g: (i, 0))
    in_specs = [x_spec, pl.BlockSpec((1, D_MODEL), lambda i, g: (0, 0)),
                pl.BlockSpec((D_MODEL, C_B), lambda i, g: (0, g))]
    args = [x2d, gain.reshape(1, D_MODEL), w]
    if has_v:
        in_specs.append(pl.BlockSpec((D_MODEL, C_B), lambda i, g: (0, N_GROUPS + g)))
        args.append(w)
    in_specs += [pl.BlockSpec((1, C_B), lambda i, g: (0, g)),
                 pl.BlockSpec((tm, LANES), lambda i, g: (i, 0)),
                 pl.BlockSpec((tm, LANES), lambda i, g: (i, 0))]
    args += [head_gain.reshape(1, n_rope), cos_b, sin_b]
    shapes, specs = [], []
    for _ in range(2 if has_v else 1):
        for _, d in DILATED_GROUPS:
            shapes.append(jax.ShapeDtypeStruct((batch, d, seq // d, C_B), BF16))
            specs.append(pl.BlockSpec((1, d, tm // d, C_B), lambda i, g: (i // nt, 0, i % nt, 0)))
    return pl.pallas_call(
        functools.partial(_group_proj_kernel, has_v=has_v, scale=scale),
        out_shape=tuple(shapes),
        grid=(batch * nt, N_GROUPS),
        in_specs=in_specs,
        out_specs=tuple(specs),
        scratch_shapes=[pltpu.VMEM((tm, D_MODEL), BF16), pltpu.VMEM((H_B, tm, HEAD_DIM_B), F32)],
        compiler_params=_params(("parallel", "arbitrary")),
        name="group_proj_kv" if has_v else "group_proj_q",
    )(*args)


BAND_HEADS = 4


def _band_kernel(q_ref, k_ref, v_ref, o_ref, lse_ref, *, dilation, length):
    nk = min(2 * BAND, length)
    row = lax.broadcasted_iota(jnp.int32, (BAND, nk), 0)
    col = lax.broadcasted_iota(jnp.int32, (BAND, nk), 1)
    lane_head = lax.broadcasted_iota(jnp.int32, (BAND, LANES), 1) // (LANES // BAND_HEADS)

    def block(r, i):
        q0 = i * BAND
        start = jnp.maximum(q0 + BAND - nk, 0)
        dist = (q0 - start) + row - col
        valid = jnp.logical_and(dist >= 0, dist <= BAND)
        if not isinstance(q0, int):
            q0 = pl.multiple_of(q0, BAND)
            start = pl.multiple_of(start, BAND)
        scores = []
        for h in range(BAND_HEADS):
            hs = slice(h * HEAD_DIM_B, (h + 1) * HEAD_DIM_B)
            s = _dot_nt(q_ref[0, r, pl.ds(q0, BAND), hs], k_ref[0, r, pl.ds(start, nk), hs])
            scores.append(jnp.where(valid, s, NEG_INF))
        probs, denoms, lse_tile = [], [], jnp.zeros((BAND, LANES), F32)
        for h in range(BAND_HEADS):
            m = jnp.max(scores[h], axis=-1, keepdims=True)
            p = jnp.exp2(scores[h] - m)
            denom = jnp.sum(p, axis=-1, keepdims=True)
            probs.append(p.astype(BF16))
            denoms.append(denom)
            lse_tile = jnp.where(lane_head == h, m + jnp.log2(denom), lse_tile)
        tok = pl.ds(q0 * dilation + r, BAND, stride=dilation) if dilation > 1 else pl.ds(q0, BAND)
        for h in range(BAND_HEADS):
            hs = slice(h * HEAD_DIM_B, (h + 1) * HEAD_DIM_B)
            out = _dot(probs[h], v_ref[0, r, pl.ds(start, nk), hs])
            o_ref[0, h, tok, :] = out / denoms[h]
        lse_ref[0, 0, tok, :] = lse_tile

    nb = length // BAND
    for r in range(dilation):
        if nb == 1:
            block(r, 0)
        else:
            def body(i, carry, r=r):
                block(r, i)
                return carry
            lax.fori_loop(0, nb, body, 0)


def _band_attention(q, k, v, dilation, batch, seq):
    length = seq // dilation
    halves = H_B // BAND_HEADS
    cw = BAND_HEADS * HEAD_DIM_B
    blk = pl.BlockSpec((1, dilation, length, cw), lambda b, hh: (b, 0, 0, hh))
    return pl.pallas_call(
        functools.partial(_band_kernel, dilation=dilation, length=length),
        out_shape=(jax.ShapeDtypeStruct((batch, H_B, seq, HEAD_DIM_B), F32),
                   jax.ShapeDtypeStruct((batch, halves, seq, LANES), F32)),
        grid=(batch, halves),
        in_specs=[blk, blk, blk],
        out_specs=(pl.BlockSpec((1, BAND_HEADS, seq, HEAD_DIM_B), lambda b, hh: (b, hh, 0, 0)),
                   pl.BlockSpec((1, 1, seq, LANES), lambda b, hh: (b, hh, 0, 0))),
        compiler_params=_params(("parallel", "parallel")),
        name=f"band_attn_d{dilation}",
    )(q, k, v)


def _combine_proj_kernel(o0_ref, o1_ref, o2_ref, l0_ref, l1_ref, l2_ref, w_ref, x_ref, y_ref, o_scr):
    lanes_per_head = LANES // BAND_HEADS
    for half in range(H_B // BAND_HEADS):
        l0, l1, l2 = l0_ref[0, half], l1_ref[0, half], l2_ref[0, half]
        mx = jnp.maximum(jnp.maximum(l0, l1), l2)
        e0, e1, e2 = jnp.exp2(l0 - mx), jnp.exp2(l1 - mx), jnp.exp2(l2 - mx)
        inv = 1.0 / (e0 + e1 + e2)
        w0, w1, w2 = e0 * inv, e1 * inv, e2 * inv
        for hq in range(BAND_HEADS):
            h = half * BAND_HEADS + hq
            c = hq * lanes_per_head
            o = (w0[:, c:c + 1] * o0_ref[0, h] + w1[:, c:c + 1] * o1_ref[0, h] + w2[:, c:c + 1] * o2_ref[0, h])
            o_scr[:, h * HEAD_DIM_B:(h + 1) * HEAD_DIM_B] = o.astype(BF16)
    y_ref[...] = x_ref[...] + _dot(o_scr[...], w_ref[...])


def _combine_proj(outs, lses, w_o, x2d, seq):
    T = x2d.shape[0]
    tm = 512
    nt = seq // tm
    halves = H_B // BAND_HEADS
    o_spec = pl.BlockSpec((1, H_B, tm, HEAD_DIM_B), lambda i: (i // nt, 0, i % nt, 0))
    l_spec = pl.BlockSpec((1, halves, tm, LANES), lambda i: (i // nt, 0, i % nt, 0))
    return pl.pallas_call(
        _combine_proj_kernel,
        out_shape=jax.ShapeDtypeStruct((T, D_MODEL), F32),
        grid=(T // tm,),
        in_specs=[o_spec, o_spec, o_spec, l_spec, l_spec, l_spec,
                  pl.BlockSpec((C_B, D_MODEL), lambda i: (0, 0)),
                  pl.BlockSpec((tm, D_MODEL), lambda i: (i, 0))],
        out_specs=pl.BlockSpec((tm, D_MODEL), lambda i: (i, 0)),
        scratch_shapes=[pltpu.VMEM((tm, C_B), BF16)],
        compiler_params=_params(("parallel",)),
        name="combine_proj",
    )(*outs, *lses, w_o.astype(BF16), x2d)


def kernel(x, positions, attn_norm, mlp_norm, mla_w_in, mla_qa_norm, mla_kva_norm, mla_w_qb, mla_w_kvb,
           mla_q_norm, mla_k_norm, mla_w_o, kv_norm, w_kv, k_norm_b, w_q_b, q_norm_b, w_o_b, mlp_w1, mlp_w2):
    B, S, D = x.shape
    T = B * S
    cos_a, sin_a, cos_b, sin_b = _rope_tables(positions)
    h = x.reshape(T, D)

    def head_gains(gn):
        return jnp.broadcast_to(gn[:, None, :], (N_GROUPS, H_B, HEAD_DIM_B))

    for a in range(N_A_LAYERS):
        q, k, v = _mla_proj(h, attn_norm[a], mla_w_in[a], mla_qa_norm[a], mla_kva_norm[a], mla_w_qb[a],
                            mla_w_kvb[a], mla_q_norm[a], mla_k_norm[a], cos_a, sin_a)
        o = _mla_attention(q, k, v, B, S)
        h = _out_proj(o, mla_w_o[a], h)
        h = _mlp(h, mlp_norm[a], mlp_w1[a], mlp_w2[a])

    kv = _group_proj(h, kv_norm, w_kv, head_gains(k_norm_b), True, 1.0, cos_b, sin_b, B, S)
    ks, vs = kv[:N_GROUPS], kv[N_GROUPS:]

    for b in range(N_B_LAYERS):
        layer = N_A_LAYERS + b
        qs = _group_proj(h, attn_norm[layer], w_q_b[b], head_gains(q_norm_b[b]), False,
                         HEAD_DIM_B ** -0.5 * LOG2E, cos_b, sin_b, B, S)
        outs, lses = [], []
        for g, (window, dilation) in enumerate(DILATED_GROUPS):
            assert window // dilation == BAND
            o, lse = _band_attention(qs[g], ks[g], vs[g], dilation, B, S)
            outs.append(o)
            lses.append(lse)
        h = _combine_proj(outs, lses, w_o_b[b], h, S)
        h = _mlp(h, mlp_norm[layer], mlp_w1[layer], mlp_w2[layer])

    return h.reshape(B, S, D)
```

```python
import functools

import jax
import jax.numpy as jnp
from jax import lax
from jax.experimental import pallas as pl
from jax.experimental.pallas import tpu as pltpu

D_MODEL = 1024
N_A_LAYERS = 2
N_B_LAYERS = 2
H_A = 16
QK_NOPE = 128
QK_ROPE = 64
QK_HEAD = QK_NOPE + QK_ROPE
V_HEAD = 128
Q_LORA = 256
KV_LORA = 128
DILATED_GROUPS = ((128, 1), (512, 4), (2048, 16))
N_GROUPS = 3
H_B = 8
HEAD_DIM_B = 128
C_B = H_B * HEAD_DIM_B
D_FF = 4 * D_MODEL
ROPE_THETA = 10000.0
NORM_EPS = 1e-6
NEG_INF = -1e30
LOG2E = 1.4426950408889634

LANES = 128
QK_SLOT = 2 * LANES
VMEM_LIMIT = 56 * 1024 * 1024
BAND = 128
PROJ_TILE = 512
PROJ_CHUNK = 128

BF16 = jnp.bfloat16
F32 = jnp.float32


def _params(semantics):
    return pltpu.CompilerParams(dimension_semantics=semantics, vmem_limit_bytes=VMEM_LIMIT)


def _rms(x, gain):
    ms = jnp.mean(x * x, axis=-1, keepdims=True)
    return x * lax.rsqrt(ms + NORM_EPS) * gain


def _rot_half(u):
    return pltpu.roll(u, LANES // 2, axis=1)


def _dot(a, b):
    return jnp.dot(a, b, preferred_element_type=F32)


def _dot_nt(a, b):
    return lax.dot_general(a, b, (((1,), (1,)), ((), ())), preferred_element_type=F32)


def _tables_kernel(pos_ref, f_ref, cm_ref, sm_ref, cos_ref, sin_ref):
    ang = pos_ref[...] * f_ref[...]
    cos_ref[...] = jnp.cos(ang) * cm_ref[...]
    sin_ref[...] = jnp.sin(ang) * sm_ref[...]


def _rope_table(pos, freq, cos_mask, sin_sign):
    n = pos.shape[0]
    tm = 1024
    row = pl.BlockSpec((1, LANES), lambda i: (0, 0))
    tab = pl.BlockSpec((tm, LANES), lambda i: (i, 0))
    shp = jax.ShapeDtypeStruct((n, LANES), F32)
    return pl.pallas_call(
        _tables_kernel,
        out_shape=(shp, shp),
        grid=(n // tm,),
        in_specs=[pl.BlockSpec((tm, 1), lambda i: (i, 0)), row, row, row],
        out_specs=(tab, tab),
        compiler_params=_params(("parallel",)),
        name="rope_table",
    )(pos.reshape(n, 1), freq.reshape(1, LANES), cos_mask.reshape(1, LANES), sin_sign.reshape(1, LANES))


def _rope_tables(positions):
    B, S = positions.shape
    pos = positions.astype(F32)
    inv_a = ROPE_THETA ** (-jnp.arange(0, QK_ROPE, 2, dtype=F32) / QK_ROPE)
    inv_b = ROPE_THETA ** (-jnp.arange(0, HEAD_DIM_B, 2, dtype=F32) / HEAD_DIM_B)
    z32 = jnp.zeros((32,), F32)
    o32 = jnp.ones((32,), F32)
    o64 = jnp.ones((64,), F32)
    cos_a, sin_a = _rope_table(pos.reshape(B * S), jnp.concatenate([inv_a, z32, inv_a, z32]),
                               jnp.concatenate([o32, z32, o32, z32]), jnp.concatenate([-o32, z32, o32, z32]))
    nt = S // PROJ_TILE
    orders = [pos.reshape(B, nt, PROJ_TILE // d, d).transpose(0, 1, 3, 2).reshape(B * S) for _, d in DILATED_GROUPS]
    cos_b, sin_b = _rope_table(jnp.concatenate(orders), jnp.concatenate([inv_b, inv_b]),
                               jnp.concatenate([o64, o64]), jnp.concatenate([-o64, o64]))
    return cos_a, sin_a, cos_b.reshape(N_GROUPS, B * S, LANES), sin_b.reshape(N_GROUPS, B * S, LANES)


def _mla_proj_kernel(x_ref, g_ref, win_ref, qa_ref, kva_ref, wqb_ref, wkb_ref, wvb_ref, qg_ref, kg_ref,
                     cos_ref, sin_ref, q_ref, k_ref, v_ref, q_scr, kn_scr, kpe_scr, *, scale):
    tm = x_ref.shape[0]
    xn = _rms(x_ref[...], g_ref[...]).astype(BF16)
    lat = _dot(xn, win_ref[...])
    cqn = _rms(lat[:, :Q_LORA], qa_ref[...]).astype(BF16)
    ckvn = _rms(lat[:, Q_LORA:Q_LORA + KV_LORA], kva_ref[...]).astype(BF16)
    kpe_scr[...] = lat[:, Q_LORA + KV_LORA:]
    v_ref[...] = _dot(ckvn, wvb_ref[...]).astype(BF16)
    q_scr[...] = _dot(cqn, wqb_ref[...])
    kn_scr[...] = _dot(ckvn, wkb_ref[...])

    def chunk(ci, carry):
        rows = pl.ds(pl.multiple_of(ci * PROJ_CHUNK, PROJ_CHUNK), PROJ_CHUNK)
        cos = cos_ref[rows, :]
        sin = sin_ref[rows, :]
        qg_n, qg_pe = qg_ref[:, :LANES], qg_ref[:, LANES:]
        kg_n, kg_pe = kg_ref[:, :LANES], kg_ref[:, LANES:]
        k_pe = kpe_scr[rows, :]
        kpe_ss = jnp.sum(k_pe * k_pe, axis=-1, keepdims=True)
        kpe_g = k_pe * kg_pe
        kpe_rot = kpe_g * cos + _rot_half(kpe_g) * sin
        for h in range(H_A):
            qn = q_scr[rows, h * QK_SLOT:h * QK_SLOT + LANES]
            qp = q_scr[rows, h * QK_SLOT + LANES:(h + 1) * QK_SLOT]
            ss = jnp.sum(qn * qn + qp * qp, axis=-1, keepdims=True)
            rs = lax.rsqrt(ss * (1.0 / QK_HEAD) + NORM_EPS) * scale
            qpg = qp * qg_pe
            q_ref[rows, h * QK_SLOT:h * QK_SLOT + LANES] = (qn * rs * qg_n).astype(BF16)
            q_ref[rows, h * QK_SLOT + LANES:(h + 1) * QK_SLOT] = (
                (qpg * cos + _rot_half(qpg) * sin) * rs).astype(BF16)
            kn = kn_scr[rows, h * LANES:(h + 1) * LANES]
            ssk = jnp.sum(kn * kn, axis=-1, keepdims=True) + kpe_ss
            rsk = lax.rsqrt(ssk * (1.0 / QK_HEAD) + NORM_EPS)
            k_ref[rows, h * QK_SLOT:h * QK_SLOT + LANES] = (kn * rsk * kg_n).astype(BF16)
            k_ref[rows, h * QK_SLOT + LANES:(h + 1) * QK_SLOT] = (kpe_rot * rsk).astype(BF16)
        return carry

    lax.fori_loop(0, tm // PROJ_CHUNK, chunk, 0)


def _rope_tile_cols(a):
    z = jnp.zeros(a.shape[:-1] + (32,), a.dtype)
    return jnp.concatenate([a[..., :32], z, a[..., 32:], z], axis=-1)


def _mla_proj(x2d, gain, w_in, qa_norm, kva_norm, w_qb, w_kvb, q_norm, k_norm, cos_a, sin_a):
    T = x2d.shape[0]
    tm = 256
    w_in_p = jnp.concatenate(
        [w_in[:, :Q_LORA + KV_LORA], _rope_tile_cols(w_in[:, Q_LORA + KV_LORA:])], axis=-1).astype(BF16)
    wq = w_qb.reshape(Q_LORA, H_A, QK_HEAD)
    wq_p = jnp.concatenate([wq[..., :QK_NOPE], _rope_tile_cols(wq[..., QK_NOPE:])], axis=-1)
    wq_p = wq_p.reshape(Q_LORA, H_A * QK_SLOT).astype(BF16)
    wkv = w_kvb.reshape(KV_LORA, H_A, QK_NOPE + V_HEAD)
    wkb = wkv[..., :QK_NOPE].reshape(KV_LORA, H_A * QK_NOPE).astype(BF16)
    wvb = wkv[..., QK_NOPE:].reshape(KV_LORA, H_A * V_HEAD).astype(BF16)
    qg = jnp.concatenate([q_norm[:QK_NOPE], _rope_tile_cols(q_norm[QK_NOPE:])]).reshape(1, QK_SLOT)
    kg = jnp.concatenate([k_norm[:QK_NOPE], _rope_tile_cols(k_norm[QK_NOPE:])]).reshape(1, QK_SLOT)

    def const(shape):
        return pl.BlockSpec(shape, lambda i: (0, 0))

    def rows(width):
        return pl.BlockSpec((tm, width), lambda i: (i, 0))

    n_in = Q_LORA + KV_LORA + LANES
    return pl.pallas_call(
        functools.partial(_mla_proj_kernel, scale=QK_HEAD ** -0.5 * LOG2E),
        out_shape=(jax.ShapeDtypeStruct((T, H_A * QK_SLOT), BF16),
                   jax.ShapeDtypeStruct((T, H_A * QK_SLOT), BF16),
                   jax.ShapeDtypeStruct((T, H_A * V_HEAD), BF16)),
        grid=(T // tm,),
        in_specs=[rows(D_MODEL), const((1, D_MODEL)), const((D_MODEL, n_in)), const((1, Q_LORA)),
                  const((1, KV_LORA)), const((Q_LORA, H_A * QK_SLOT)), const((KV_LORA, H_A * QK_NOPE)),
                  const((KV_LORA, H_A * V_HEAD)), const((1, QK_SLOT)), const((1, QK_SLOT)),
                  rows(LANES), rows(LANES)],
        out_specs=(rows(H_A * QK_SLOT), rows(H_A * QK_SLOT), rows(H_A * V_HEAD)),
        scratch_shapes=[pltpu.VMEM((tm, H_A * QK_SLOT), F32), pltpu.VMEM((tm, H_A * QK_NOPE), F32),
                        pltpu.VMEM((tm, LANES), F32)],
        compiler_params=_params(("parallel",)),
        name="mla_proj",
    )(x2d, gain.reshape(1, D_MODEL), w_in_p, qa_norm.reshape(1, Q_LORA), kva_norm.reshape(1, KV_LORA),
      wq_p, wkb, wvb, qg, kg, cos_a, sin_a)


FLASH_HEADS = 2


def _flash_kernel(q_ref, k_ref, v_ref, o_ref, *, tq, tk):
    i = pl.program_id(2)
    qs = [q_ref[:, h * QK_SLOT:(h + 1) * QK_SLOT] for h in range(FLASH_HEADS)]

    def step(j, carry, masked):
        off = pl.multiple_of(j * tk, tk)
        ss = []
        for h in range(FLASH_HEADS):
            s = _dot_nt(qs[h], k_ref[pl.ds(off, tk), h * QK_SLOT:(h + 1) * QK_SLOT])
            if masked:
                row = lax.broadcasted_iota(jnp.int32, (tq, tk), 0)
                col = lax.broadcasted_iota(jnp.int32, (tq, tk), 1)
                s = jnp.where(row >= col, s, NEG_INF)
            ss.append(s)
        new = []
        for h in range(FLASH_HEADS):
            m, l, acc = carry[h]
            s = ss[h]
            m_new = jnp.maximum(m, jnp.max(s, axis=-1, keepdims=True))
            p = jnp.exp2(s - m_new)
            alpha = jnp.exp2(m - m_new)
            l = alpha * l + jnp.sum(p, axis=-1, keepdims=True)
            acc = alpha * acc + _dot(p.astype(BF16), v_ref[pl.ds(off, tk), h * V_HEAD:(h + 1) * V_HEAD])
            new.append((m_new, l, acc))
        return tuple(new)

    init = tuple((jnp.full((tq, 1), NEG_INF, F32), jnp.zeros((tq, 1), F32), jnp.zeros((tq, V_HEAD), F32))
                 for _ in range(FLASH_HEADS))
    carry = lax.fori_loop(0, i * (tq // tk), lambda j, c: step(j, c, False), init)
    carry = step(i * (tq // tk), carry, True)
    for h in range(FLASH_HEADS):
        m, l, acc = carry[h]
        o_ref[:, h * V_HEAD:(h + 1) * V_HEAD] = (acc / l).astype(o_ref.dtype)


def _mla_attention(q, k, v, batch, seq):
    tq = tk = 512
    nq = seq // tq
    T = q.shape[0]
    hh = FLASH_HEADS
    return pl.pallas_call(
        functools.partial(_flash_kernel, tq=tq, tk=tk),
        out_shape=jax.ShapeDtypeStruct((T, H_A * V_HEAD), BF16),
        grid=(batch, H_A // hh, nq),
        in_specs=[pl.BlockSpec((tq, hh * QK_SLOT), lambda b, h, i: (b * nq + i, h)),
                  pl.BlockSpec((seq, hh * QK_SLOT), lambda b, h, i: (b, h)),
                  pl.BlockSpec((seq, hh * V_HEAD), lambda b, h, i: (b, h))],
        out_specs=pl.BlockSpec((tq, hh * V_HEAD), lambda b, h, i: (b * nq + i, h)),
        compiler_params=_params(("parallel", "parallel", "arbitrary")),
        name="mla_flash",
    )(q, k, v)


def _out_proj_kernel(o_ref, w_ref, x_ref, y_ref):
    y_ref[...] = x_ref[...] + _dot(o_ref[...], w_ref[...])


def _out_proj(o, w_o, x2d):
    T, K = o.shape
    tm = 512
    return pl.pallas_call(
        _out_proj_kernel,
        out_shape=jax.ShapeDtypeStruct((T, D_MODEL), F32),
        grid=(T // tm,),
        in_specs=[pl.BlockSpec((tm, K), lambda i: (i, 0)),
                  pl.BlockSpec((K, D_MODEL), lambda i: (0, 0)),
                  pl.BlockSpec((tm, D_MODEL), lambda i: (i, 0))],
        out_specs=pl.BlockSpec((tm, D_MODEL), lambda i: (i, 0)),
        compiler_params=_params(("parallel",)),
        name="out_proj",
    )(o, w_o.astype(BF16), x2d)


def _mlp_kernel(x_ref, g_ref, w1_ref, w2_ref, y_ref, xn_ref, acc_ref):
    f = pl.program_id(1)

    @pl.when(f == 0)
    def _():
        xn_ref[...] = _rms(x_ref[...], g_ref[...]).astype(BF16)
        acc_ref[...] = jnp.zeros_like(acc_ref)

    h = jnp.maximum(_dot(xn_ref[...], w1_ref[...]), 0.0)
    acc_ref[...] += _dot((h * h).astype(BF16), w2_ref[...])

    @pl.when(f == pl.num_programs(1) - 1)
    def _():
        y_ref[...] = x_ref[...] + acc_ref[...]


def _mlp(x2d, gain, w1, w2):
    T = x2d.shape[0]
    tm, tf = 512, 1024
    return pl.pallas_call(
        _mlp_kernel,
        out_shape=jax.ShapeDtypeStruct((T, D_MODEL), F32),
        grid=(T // tm, D_FF // tf),
        in_specs=[pl.BlockSpec((tm, D_MODEL), lambda i, f: (i, 0)),
                  pl.BlockSpec((1, D_MODEL), lambda i, f: (0, 0)),
                  pl.BlockSpec((D_MODEL, tf), lambda i, f: (0, f)),
                  pl.BlockSpec((tf, D_MODEL), lambda i, f: (f, 0))],
        out_specs=pl.BlockSpec((tm, D_MODEL), lambda i, f: (i, 0)),
        scratch_shapes=[pltpu.VMEM((tm, D_MODEL), BF16), pltpu.VMEM((tm, D_MODEL), F32)],
        compiler_params=_params(("parallel", "arbitrary")),
        name="mlp",
    )(x2d, gain.reshape(1, D_MODEL), w1.astype(BF16), w2.astype(BF16))


def _group_proj_kernel(*refs, has_v, scale):
    if has_v:
        x_ref, g_ref, w_ref, hg_ref, cos_ref, sin_ref = refs[:6]
        outs = refs[6:12]
        xs_ref, xn_ref, y_ref = refs[12:]
    else:
        x_ref, g_ref, w_ref, hg_ref, cos_ref, sin_ref = refs[:6]
        outs = refs[6:9]
        xs_ref, xn_ref, y_ref = refs[9:]
    tm = x_ref.shape[0]
    n_col = D_MODEL // LANES
    xn = _rms(x_ref[...], g_ref[...])
    for c in range(n_col):
        xs_ref[c] = xn[:, c * LANES:(c + 1) * LANES]

    for gi, (_, d) in enumerate(DILATED_GROUPS):
        rows = tm // d
        if d == 1:
            xn_ref[...] = xn.astype(BF16)
        else:
            for r in range(d):
                for c in range(n_col):
                    xn_ref[r * rows:(r + 1) * rows, c * LANES:(c + 1) * LANES] = (
                        xs_ref[c, pl.ds(r, rows, stride=d), :].astype(BF16))
        y_ref[...] = _dot(xn_ref[...], w_ref[:, gi * C_B:(gi + 1) * C_B])
        out_ref = outs[gi]

        def chunk(ci, carry, gi=gi, d=d, rows=rows, out_ref=out_ref):
            r0 = pl.multiple_of(ci * PROJ_CHUNK, PROJ_CHUNK)
            cos = cos_ref[gi, pl.ds(r0, PROJ_CHUNK), :]
            sin = sin_ref[gi, pl.ds(r0, PROJ_CHUNK), :]
            for h in range(H_B):
                hs = slice(h * HEAD_DIM_B, (h + 1) * HEAD_DIM_B)
                yh = y_ref[pl.ds(r0, PROJ_CHUNK), hs]
                rs = lax.rsqrt(jnp.mean(yh * yh, axis=-1, keepdims=True) + NORM_EPS) * scale
                yg = yh * hg_ref[:, gi * C_B + h * HEAD_DIM_B:gi * C_B + (h + 1) * HEAD_DIM_B]
                res = ((yg * cos + _rot_half(yg) * sin) * rs).astype(BF16)
                if rows >= PROJ_CHUNK:
                    per = rows // PROJ_CHUNK
                    out_ref[0, ci // per, pl.ds(pl.multiple_of((ci % per) * PROJ_CHUNK, PROJ_CHUNK), PROJ_CHUNK),
                            hs] = res
                else:
                    per = PROJ_CHUNK // rows
                    for s in range(per):
                        out_ref[0, ci * per + s, :, hs] = res[s * rows:(s + 1) * rows]
            return carry

        lax.fori_loop(0, tm // PROJ_CHUNK, chunk, 0)

        if has_v:
            yv = _dot(xn_ref[...], w_ref[:, (N_GROUPS + gi) * C_B:(N_GROUPS + gi + 1) * C_B])
            for r in range(d):
                outs[N_GROUPS + gi][0, r] = yv[r * rows:(r + 1) * rows].astype(BF16)


def _group_proj(x2d, gain, w, head_gain, has_v, scale, cos_b, sin_b, batch, seq):
    tm = PROJ_TILE
    nt = seq // tm
    n_rope = N_GROUPS * C_B
    n_out = w.shape[1]
    in_specs = [pl.BlockSpec((tm, D_MODEL), lambda i: (i, 0)),
                pl.BlockSpec((1, D_MODEL), lambda i: (0, 0)),
                pl.BlockSpec((D_MODEL, n_out), lambda i: (0, 0), pipeline_mode=pl.Buffered(1)),
                pl.BlockSpec((1, n_rope), lambda i: (0, 0)),
                pl.BlockSpec((N_GROUPS, tm, LANES), lambda i: (0, i, 0)),
                pl.BlockSpec((N_GROUPS, tm, LANES), lambda i: (0, i, 0))]
    shapes, specs = [], []
    for _ in range(2 if has_v else 1):
        for _, d in DILATED_GROUPS:
            shapes.append(jax.ShapeDtypeStruct((batch, d, seq // d, C_B), BF16))
            specs.append(pl.BlockSpec((1, d, tm // d, C_B), lambda i: (i // nt, 0, i % nt, 0)))
    return pl.pallas_call(
        functools.partial(_group_proj_kernel, has_v=has_v, scale=scale),
        out_shape=tuple(shapes),
        grid=(batch * nt,),
        in_specs=in_specs,
        out_specs=tuple(specs),
        scratch_shapes=[pltpu.VMEM((D_MODEL // LANES, tm, LANES), F32), pltpu.VMEM((tm, D_MODEL), BF16),
                        pltpu.VMEM((tm, C_B), F32)],
        compiler_params=_params(("parallel",)),
        name="group_proj_kv" if has_v else "group_proj_q",
    )(x2d, gain.reshape(1, D_MODEL), w.astype(BF16), head_gain.reshape(1, n_rope), cos_b, sin_b)


BAND_HEADS = 4


def _band_kernel(q_ref, k_ref, v_ref, o_ref, lse_ref, *, dilation, length):
    nk = min(2 * BAND, length)
    row = lax.broadcasted_iota(jnp.int32, (BAND, nk), 0)
    col = lax.broadcasted_iota(jnp.int32, (BAND, nk), 1)
    lane_head = lax.broadcasted_iota(jnp.int32, (BAND, LANES), 1) // (LANES // BAND_HEADS)

    def block(r, i):
        q0 = i * BAND
        start = jnp.maximum(q0 + BAND - nk, 0)
        dist = (q0 - start) + row - col
        valid = jnp.logical_and(dist >= 0, dist <= BAND)
        if not isinstance(q0, int):
            q0 = pl.multiple_of(q0, BAND)
            start = pl.multiple_of(start, BAND)
        scores = []
        for h in range(BAND_HEADS):
            hs = slice(h * HEAD_DIM_B, (h + 1) * HEAD_DIM_B)
            s = _dot_nt(q_ref[0, r, pl.ds(q0, BAND), hs], k_ref[0, r, pl.ds(start, nk), hs])
            scores.append(jnp.where(valid, s, NEG_INF))
        probs, denoms, lse_tile = [], [], jnp.zeros((BAND, LANES), F32)
        for h in range(BAND_HEADS):
            m = jnp.max(scores[h], axis=-1, keepdims=True)
            p = jnp.exp2(scores[h] - m)
            denom = jnp.sum(p, axis=-1, keepdims=True)
            probs.append(p.astype(BF16))
            denoms.append(denom)
            lse_tile = jnp.where(lane_head == h, m + jnp.log2(denom), lse_tile)
        tok = pl.ds(q0 * dilation + r, BAND, stride=dilation) if dilation > 1 else pl.ds(q0, BAND)
        for h in range(BAND_HEADS):
            hs = slice(h * HEAD_DIM_B, (h + 1) * HEAD_DIM_B)
            out = _dot(probs[h], v_ref[0, r, pl.ds(start, nk), hs])
            o_ref[0, h, tok, :] = out / denoms[h]
        lse_ref[0, 0, tok, :] = lse_tile

    nb = length // BAND
    for r in range(dilation):
        if nb == 1:
            block(r, 0)
        else:
            def body(i, carry, r=r):
                block(r, i)
                return carry
            lax.fori_loop(0, nb, body, 0)


def _band_attention(q, k, v, dilation, batch, seq):
    length = seq // dilation
    halves = H_B // BAND_HEADS
    cw = BAND_HEADS * HEAD_DIM_B
    blk = pl.BlockSpec((1, dilation, length, cw), lambda b, hh: (b, 0, 0, hh))
    return pl.pallas_call(
        functools.partial(_band_kernel, dilation=dilation, length=length),
        out_shape=(jax.ShapeDtypeStruct((batch, H_B, seq, HEAD_DIM_B), F32),
                   jax.ShapeDtypeStruct((batch, halves, seq, LANES), F32)),
        grid=(batch, halves),
        in_specs=[blk, blk, blk],
        out_specs=(pl.BlockSpec((1, BAND_HEADS, seq, HEAD_DIM_B), lambda b, hh: (b, hh, 0, 0)),
                   pl.BlockSpec((1, 1, seq, LANES), lambda b, hh: (b, hh, 0, 0))),
        compiler_params=_params(("parallel", "parallel")),
        name=f"band_attn_d{dilation}",
    )(q, k, v)


def _combine_proj_kernel(o0_ref, o1_ref, o2_ref, l0_ref, l1_ref, l2_ref, w_ref, x_ref, y_ref, o_scr):
    lanes_per_head = LANES // BAND_HEADS
    for half in range(H_B // BAND_HEADS):
        l0, l1, l2 = l0_ref[0, half], l1_ref[0, half], l2_ref[0, half]
        mx = jnp.maximum(jnp.maximum(l0, l1), l2)
        e0, e1, e2 = jnp.exp2(l0 - mx), jnp.exp2(l1 - mx), jnp.exp2(l2 - mx)
        inv = 1.0 / (e0 + e1 + e2)
        w0, w1, w2 = e0 * inv, e1 * inv, e2 * inv
        for hq in range(BAND_HEADS):
            h = half * BAND_HEADS + hq
            c = hq * lanes_per_head
            o = (w0[:, c:c + 1] * o0_ref[0, h] + w1[:, c:c + 1] * o1_ref[0, h] + w2[:, c:c + 1] * o2_ref[0, h])
            o_scr[:, h * HEAD_DIM_B:(h + 1) * HEAD_DIM_B] = o.astype(BF16)
    y_ref[...] = x_ref[...] + _dot(o_scr[...], w_ref[...])


def _combine_proj(outs, lses, w_o, x2d, seq):
    T = x2d.shape[0]
    tm = 512
    nt = seq // tm
    halves = H_B // BAND_HEADS
    o_spec = pl.BlockSpec((1, H_B, tm, HEAD_DIM_B), lambda i: (i // nt, 0, i % nt, 0))
    l_spec = pl.BlockSpec((1, halves, tm, LANES), lambda i: (i // nt, 0, i % nt, 0))
    return pl.pallas_call(
        _combine_proj_kernel,
        out_shape=jax.ShapeDtypeStruct((T, D_MODEL), F32),
        grid=(T // tm,),
        in_specs=[o_spec, o_spec, o_spec, l_spec, l_spec, l_spec,
                  pl.BlockSpec((C_B, D_MODEL), lambda i: (0, 0)),
                  pl.BlockSpec((tm, D_MODEL), lambda i: (i, 0))],
        out_specs=pl.BlockSpec((tm, D_MODEL), lambda i: (i, 0)),
        scratch_shapes=[pltpu.VMEM((tm, C_B), BF16)],
        compiler_params=_params(("parallel",)),
        name="combine_proj",
    )(*outs, *lses, w_o.astype(BF16), x2d)


def kernel(x, positions, attn_norm, mlp_norm, mla_w_in, mla_qa_norm, mla_kva_norm, mla_w_qb, mla_w_kvb,
           mla_q_norm, mla_k_norm, mla_w_o, kv_norm, w_kv, k_norm_b, w_q_b, q_norm_b, w_o_b, mlp_w1, mlp_w2):
    B, S, D = x.shape
    T = B * S
    cos_a, sin_a, cos_b, sin_b = _rope_tables(positions)
    h = x.reshape(T, D)

    def head_gains(gn):
        return jnp.broadcast_to(gn[:, None, :], (N_GROUPS, H_B, HEAD_DIM_B))

    for a in range(N_A_LAYERS):
        q, k, v = _mla_proj(h, attn_norm[a], mla_w_in[a], mla_qa_norm[a], mla_kva_norm[a], mla_w_qb[a],
                            mla_w_kvb[a], mla_q_norm[a], mla_k_norm[a], cos_a, sin_a)
        o = _mla_attention(q, k, v, B, S)
        h = _out_proj(o, mla_w_o[a], h)
        h = _mlp(h, mlp_norm[a], mlp_w1[a], mlp_w2[a])

    kv = _group_proj(h, kv_norm, w_kv, head_gains(k_norm_b), True, 1.0, cos_b, sin_b, B, S)
    ks, vs = kv[:N_GROUPS], kv[N_GROUPS:]

    for b in range(N_B_LAYERS):
        layer = N_A_LAYERS + b
        qs = _group_proj(h, attn_norm[layer], w_q_b[b], head_gains(q_norm_b[b]), False,
                         HEAD_DIM_B ** -0.5 * LOG2E, cos_b, sin_b, B, S)
        outs, lses = [], []
        for g, (window, dilation) in enumerate(DILATED_GROUPS):
            assert window // dilation == BAND
            o, lse = _band_attention(qs[g], ks[g], vs[g], dilation, B, S)
            outs.append(o)
            lses.append(lse)
        h = _combine_proj(outs, lses, w_o_b[b], h, S)
        h = _mlp(h, mlp_norm[layer], mlp_w1[layer], mlp_w2[layer])

    return h.reshape(B, S, D)
```

```python
import functools

import jax
import jax.numpy as jnp
from jax import lax
from jax.experimental import pallas as pl
from jax.experimental.pallas import tpu as pltpu

D_MODEL = 1024
N_A_LAYERS = 2
N_B_LAYERS = 2
H_A = 16
QK_NOPE = 128
QK_ROPE = 64
QK_HEAD = QK_NOPE + QK_ROPE
V_HEAD = 128
Q_LORA = 256
KV_LORA = 128
DILATED_GROUPS = ((128, 1), (512, 4), (2048, 16))
N_GROUPS = 3
H_B = 8
HEAD_DIM_B = 128
C_B = H_B * HEAD_DIM_B
D_FF = 4 * D_MODEL
ROPE_THETA = 10000.0
NORM_EPS = 1e-6
NEG_INF = -1e30
LOG2E = 1.4426950408889634

LANES = 128
QK_SLOT = 2 * LANES
VMEM_LIMIT = 56 * 1024 * 1024
BAND = 128
PROJ_TILE = 512
PROJ_CHUNK = 128

BF16 = jnp.bfloat16
F32 = jnp.float32


def _params(semantics):
    return pltpu.CompilerParams(dimension_semantics=semantics, vmem_limit_bytes=VMEM_LIMIT)


def _rms(x, gain):
    ms = jnp.mean(x * x, axis=-1, keepdims=True)
    return x * lax.rsqrt(ms + NORM_EPS) * gain


def _rot_half(u):
    return pltpu.roll(u, LANES // 2, axis=1)


def _dot(a, b):
    return jnp.dot(a, b, preferred_element_type=F32)


def _dot_nt(a, b):
    return lax.dot_general(a, b, (((1,), (1,)), ((), ())), preferred_element_type=F32)


def _tables_kernel(pos_ref, f_ref, cm_ref, sm_ref, cos_ref, sin_ref):
    ang = pos_ref[...] * f_ref[...]
    cos_ref[...] = jnp.cos(ang) * cm_ref[...]
    sin_ref[...] = jnp.sin(ang) * sm_ref[...]


def _rope_table(pos, freq, cos_mask, sin_sign):
    n = pos.shape[0]
    tm = 1024
    row = pl.BlockSpec((1, LANES), lambda i: (0, 0))
    tab = pl.BlockSpec((tm, LANES), lambda i: (i, 0))
    shp = jax.ShapeDtypeStruct((n, LANES), F32)
    return pl.pallas_call(
        _tables_kernel,
        out_shape=(shp, shp),
        grid=(n // tm,),
        in_specs=[pl.BlockSpec((tm, 1), lambda i: (i, 0)), row, row, row],
        out_specs=(tab, tab),
        compiler_params=_params(("parallel",)),
        name="rope_table",
    )(pos.reshape(n, 1), freq.reshape(1, LANES), cos_mask.reshape(1, LANES), sin_sign.reshape(1, LANES))


def _rope_tables(positions):
    B, S = positions.shape
    pos = positions.astype(F32)
    inv_a = ROPE_THETA ** (-jnp.arange(0, QK_ROPE, 2, dtype=F32) / QK_ROPE)
    inv_b = ROPE_THETA ** (-jnp.arange(0, HEAD_DIM_B, 2, dtype=F32) / HEAD_DIM_B)
    z32 = jnp.zeros((32,), F32)
    o32 = jnp.ones((32,), F32)
    o64 = jnp.ones((64,), F32)
    cos_a, sin_a = _rope_table(pos.reshape(B * S), jnp.concatenate([inv_a, z32, inv_a, z32]),
                               jnp.concatenate([o32, z32, o32, z32]), jnp.concatenate([-o32, z32, o32, z32]))
    nt = S // PROJ_TILE
    orders = [pos.reshape(B, nt, PROJ_TILE // d, d).transpose(0, 1, 3, 2).reshape(B * S) for _, d in DILATED_GROUPS]
    cos_b, sin_b = _rope_table(jnp.concatenate(orders), jnp.concatenate([inv_b, inv_b]),
                               jnp.concatenate([o64, o64]), jnp.concatenate([-o64, o64]))
    return cos_a, sin_a, cos_b.reshape(N_GROUPS, B * S, LANES), sin_b.reshape(N_GROUPS, B * S, LANES)


def _mla_proj_kernel(x_ref, g_ref, win_ref, qa_ref, kva_ref, wqb_ref, wkb_ref, wvb_ref, qg_ref, kg_ref,
                     cos_ref, sin_ref, q_ref, k_ref, v_ref, q_scr, kn_scr, kpe_scr, *, scale):
    tm = x_ref.shape[0]
    xn = _rms(x_ref[...], g_ref[...]).astype(BF16)
    lat = _dot(xn, win_ref[...])
    cqn = _rms(lat[:, :Q_LORA], qa_ref[...]).astype(BF16)
    ckvn = _rms(lat[:, Q_LORA:Q_LORA + KV_LORA], kva_ref[...]).astype(BF16)
    kpe_scr[...] = lat[:, Q_LORA + KV_LORA:]
    v_ref[...] = _dot(ckvn, wvb_ref[...]).astype(BF16)
    q_scr[...] = _dot(cqn, wqb_ref[...])
    kn_scr[...] = _dot(ckvn, wkb_ref[...])

    def chunk(ci, carry):
        rows = pl.ds(pl.multiple_of(ci * PROJ_CHUNK, PROJ_CHUNK), PROJ_CHUNK)
        cos = cos_ref[rows, :]
        sin = sin_ref[rows, :]
        qg_n, qg_pe = qg_ref[:, :LANES], qg_ref[:, LANES:]
        kg_n, kg_pe = kg_ref[:, :LANES], kg_ref[:, LANES:]
        k_pe = kpe_scr[rows, :]
        kpe_ss = jnp.sum(k_pe * k_pe, axis=-1, keepdims=True)
        kpe_g = k_pe * kg_pe
        kpe_rot = kpe_g * cos + _rot_half(kpe_g) * sin
        for h in range(H_A):
            qn = q_scr[rows, h * QK_SLOT:h * QK_SLOT + LANES]
            qp = q_scr[rows, h * QK_SLOT + LANES:(h + 1) * QK_SLOT]
            ss = jnp.sum(qn * qn + qp * qp, axis=-1, keepdims=True)
            rs = lax.rsqrt(ss * (1.0 / QK_HEAD) + NORM_EPS) * scale
            qpg = qp * qg_pe
            q_ref[rows, h * QK_SLOT:h * QK_SLOT + LANES] = (qn * rs * qg_n).astype(BF16)
            q_ref[rows, h * QK_SLOT + LANES:(h + 1) * QK_SLOT] = (
                (qpg * cos + _rot_half(qpg) * sin) * rs).astype(BF16)
            kn = kn_scr[rows, h * LANES:(h + 1) * LANES]
            ssk = jnp.sum(kn * kn, axis=-1, keepdims=True) + kpe_ss
            rsk = lax.rsqrt(ssk * (1.0 / QK_HEAD) + NORM_EPS)
            k_ref[rows, h * QK_SLOT:h * QK_SLOT + LANES] = (kn * rsk * kg_n).astype(BF16)
            k_ref[rows, h * QK_SLOT + LANES:(h + 1) * QK_SLOT] = (kpe_rot * rsk).astype(BF16)
        return carry

    lax.fori_loop(0, tm // PROJ_CHUNK, chunk, 0)


def _rope_tile_cols(a):
    z = jnp.zeros(a.shape[:-1] + (32,), a.dtype)
    return jnp.concatenate([a[..., :32], z, a[..., 32:], z], axis=-1)


def _mla_proj(x2d, gain, w_in, qa_norm, kva_norm, w_qb, w_kvb, q_norm, k_norm, cos_a, sin_a):
    T = x2d.shape[0]
    tm = 256
    w_in_p = jnp.concatenate(
        [w_in[:, :Q_LORA + KV_LORA], _rope_tile_cols(w_in[:, Q_LORA + KV_LORA:])], axis=-1).astype(BF16)
    wq = w_qb.reshape(Q_LORA, H_A, QK_HEAD)
    wq_p = jnp.concatenate([wq[..., :QK_NOPE], _rope_tile_cols(wq[..., QK_NOPE:])], axis=-1)
    wq_p = wq_p.reshape(Q_LORA, H_A * QK_SLOT).astype(BF16)
    wkv = w_kvb.reshape(KV_LORA, H_A, QK_NOPE + V_HEAD)
    wkb = wkv[..., :QK_NOPE].reshape(KV_LORA, H_A * QK_NOPE).astype(BF16)
    wvb = wkv[..., QK_NOPE:].reshape(KV_LORA, H_A * V_HEAD).astype(BF16)
    qg = jnp.concatenate([q_norm[:QK_NOPE], _rope_tile_cols(q_norm[QK_NOPE:])]).reshape(1, QK_SLOT)
    kg = jnp.concatenate([k_norm[:QK_NOPE], _rope_tile_cols(k_norm[QK_NOPE:])]).reshape(1, QK_SLOT)

    def const(shape):
        return pl.BlockSpec(shape, lambda i: (0, 0))

    def rows(width):
        return pl.BlockSpec((tm, width), lambda i: (i, 0))

    n_in = Q_LORA + KV_LORA + LANES
    return pl.pallas_call(
        functools.partial(_mla_proj_kernel, scale=QK_HEAD ** -0.5 * LOG2E),
        out_shape=(jax.ShapeDtypeStruct((T, H_A * QK_SLOT), BF16),
                   jax.ShapeDtypeStruct((T, H_A * QK_SLOT), BF16),
                   jax.ShapeDtypeStruct((T, H_A * V_HEAD), BF16)),
        grid=(T // tm,),
        in_specs=[rows(D_MODEL), const((1, D_MODEL)), const((D_MODEL, n_in)), const((1, Q_LORA)),
                  const((1, KV_LORA)), const((Q_LORA, H_A * QK_SLOT)), const((KV_LORA, H_A * QK_NOPE)),
                  const((KV_LORA, H_A * V_HEAD)), const((1, QK_SLOT)), const((1, QK_SLOT)),
                  rows(LANES), rows(LANES)],
        out_specs=(rows(H_A * QK_SLOT), rows(H_A * QK_SLOT), rows(H_A * V_HEAD)),
        scratch_shapes=[pltpu.VMEM((tm, H_A * QK_SLOT), F32), pltpu.VMEM((tm, H_A * QK_NOPE), F32),
                        pltpu.VMEM((tm, LANES), F32)],
        compiler_params=_params(("parallel",)),
        name="mla_proj",
    )(x2d, gain.reshape(1, D_MODEL), w_in_p, qa_norm.reshape(1, Q_LORA), kva_norm.reshape(1, KV_LORA),
      wq_p, wkb, wvb, qg, kg, cos_a, sin_a)


FLASH_HEADS = 2


FLASH_ROWS = 64


def _flash_kernel(q_ref, k_ref, v_ref, o_ref, s_scr, p_scr, m_scr, l_scr, a_scr, acc_scr, *, tq, tk):
    i = pl.program_id(2)
    m_scr[...] = jnp.full(m_scr.shape, NEG_INF, F32)
    l_scr[...] = jnp.zeros(l_scr.shape, F32)
    acc_scr[...] = jnp.zeros(acc_scr.shape, F32)

    def step(j, masked):
        off = pl.multiple_of(j * tk, tk)
        for h in range(FLASH_HEADS):
            s_scr[h] = _dot_nt(q_ref[:, h * QK_SLOT:(h + 1) * QK_SLOT],
                               k_ref[pl.ds(off, tk), h * QK_SLOT:(h + 1) * QK_SLOT])
        for h in range(FLASH_HEADS):
            for rb in range(tq // FLASH_ROWS):
                rows = slice(rb * FLASH_ROWS, (rb + 1) * FLASH_ROWS)
                cw = min(tk, -(-((rb + 1) * FLASH_ROWS) // LANES) * LANES) if masked else tk
                s = s_scr[h, rows, :cw]
                if masked:
                    row = lax.broadcasted_iota(jnp.int32, (FLASH_ROWS, cw), 0) + rb * FLASH_ROWS
                    col = lax.broadcasted_iota(jnp.int32, (FLASH_ROWS, cw), 1)
                    s = jnp.where(row >= col, s, NEG_INF)
                m_old = m_scr[h, rows, :]
                m_new = jnp.maximum(m_old, jnp.max(s, axis=-1, keepdims=True))
                p = jnp.exp2(s - jnp.tile(m_new, (1, cw // LANES)))
                alpha = jnp.exp2(m_old - m_new)
                l_scr[h, rows, :] = alpha * l_scr[h, rows, :] + jnp.sum(p, axis=-1, keepdims=True)
                m_scr[h, rows, :] = m_new
                a_scr[h, rows, :] = alpha
                p_scr[h, rows, :cw] = p.astype(BF16)
                if cw < tk:
                    p_scr[h, rows, cw:] = jnp.zeros((FLASH_ROWS, tk - cw), BF16)
            pv = _dot(p_scr[h], v_ref[pl.ds(off, tk), h * V_HEAD:(h + 1) * V_HEAD])
            acc_scr[h] = a_scr[h] * acc_scr[h] + pv

    def body(j, carry):
        step(j, False)
        return carry

    lax.fori_loop(0, i * (tq // tk), body, 0)
    step(i * (tq // tk), True)
    for h in range(FLASH_HEADS):
        o_ref[:, h * V_HEAD:(h + 1) * V_HEAD] = (acc_scr[h] / l_scr[h]).astype(o_ref.dtype)


def _mla_attention(q, k, v, batch, seq):
    tq = tk = 512
    nq = seq // tq
    T = q.shape[0]
    hh = FLASH_HEADS
    stat = pltpu.VMEM((hh, tq, LANES), F32)
    return pl.pallas_call(
        functools.partial(_flash_kernel, tq=tq, tk=tk),
        out_shape=jax.ShapeDtypeStruct((T, H_A * V_HEAD), BF16),
        grid=(batch, H_A // hh, nq),
        in_specs=[pl.BlockSpec((tq, hh * QK_SLOT), lambda b, h, i: (b * nq + i, h)),
                  pl.BlockSpec((seq, hh * QK_SLOT), lambda b, h, i: (b, h)),
                  pl.BlockSpec((seq, hh * V_HEAD), lambda b, h, i: (b, h))],
        out_specs=pl.BlockSpec((tq, hh * V_HEAD), lambda b, h, i: (b * nq + i, h)),
        scratch_shapes=[pltpu.VMEM((hh, tq, tk), F32), pltpu.VMEM((hh, tq, tk), BF16), stat, stat, stat,
                        pltpu.VMEM((hh, tq, V_HEAD), F32)],
        compiler_params=_params(("parallel", "parallel", "arbitrary")),
        name="mla_flash",
    )(q, k, v)


def _out_proj_kernel(o_ref, w_ref, x_ref, y_ref):
    y_ref[...] = x_ref[...] + _dot(o_ref[...], w_ref[...])


def _out_proj(o, w_o, x2d):
    T, K = o.shape
    tm = 512
    return pl.pallas_call(
        _out_proj_kernel,
        out_shape=jax.ShapeDtypeStruct((T, D_MODEL), F32),
        grid=(T // tm,),
        in_specs=[pl.BlockSpec((tm, K), lambda i: (i, 0)),
                  pl.BlockSpec((K, D_MODEL), lambda i: (0, 0)),
                  pl.BlockSpec((tm, D_MODEL), lambda i: (i, 0))],
        out_specs=pl.BlockSpec((tm, D_MODEL), lambda i: (i, 0)),
        compiler_params=_params(("parallel",)),
        name="out_proj",
    )(o, w_o.astype(BF16), x2d)


def _mlp_kernel(x_ref, g_ref, w1_ref, w2_ref, y_ref, xn_ref, acc_ref):
    f = pl.program_id(1)

    @pl.when(f == 0)
    def _():
        xn_ref[...] = _rms(x_ref[...], g_ref[...]).astype(BF16)
        acc_ref[...] = jnp.zeros_like(acc_ref)

    h = jnp.maximum(_dot(xn_ref[...], w1_ref[...]), 0.0)
    acc_ref[...] += _dot((h * h).astype(BF16), w2_ref[...])

    @pl.when(f == pl.num_programs(1) - 1)
    def _():
        y_ref[...] = x_ref[...] + acc_ref[...]


def _mlp(x2d, gain, w1, w2):
    T = x2d.shape[0]
    tm, tf = 512, 1024
    return pl.pallas_call(
        _mlp_kernel,
        out_shape=jax.ShapeDtypeStruct((T, D_MODEL), F32),
        grid=(T // tm, D_FF // tf),
        in_specs=[pl.BlockSpec((tm, D_MODEL), lambda i, f: (i, 0)),
                  pl.BlockSpec((1, D_MODEL), lambda i, f: (0, 0)),
                  pl.BlockSpec((D_MODEL, tf), lambda i, f: (0, f)),
                  pl.BlockSpec((tf, D_MODEL), lambda i, f: (f, 0))],
        out_specs=pl.BlockSpec((tm, D_MODEL), lambda i, f: (i, 0)),
        scratch_shapes=[pltpu.VMEM((tm, D_MODEL), BF16), pltpu.VMEM((tm, D_MODEL), F32)],
        compiler_params=_params(("parallel", "arbitrary")),
        name="mlp",
    )(x2d, gain.reshape(1, D_MODEL), w1.astype(BF16), w2.astype(BF16))


def _group_proj_kernel(*refs, has_v, scale):
    if has_v:
        x_ref, g_ref, w_ref, hg_ref, cos_ref, sin_ref = refs[:6]
        outs = refs[6:12]
        xs_ref, xn_ref, y_ref = refs[12:]
    else:
        x_ref, g_ref, w_ref, hg_ref, cos_ref, sin_ref = refs[:6]
        outs = refs[6:9]
        xs_ref, xn_ref, y_ref = refs[9:]
    tm = x_ref.shape[0]
    n_col = D_MODEL // LANES
    xn = _rms(x_ref[...], g_ref[...])
    for c in range(n_col):
        xs_ref[c] = xn[:, c * LANES:(c + 1) * LANES]

    for gi, (_, d) in enumerate(DILATED_GROUPS):
        rows = tm // d
        if d == 1:
            xn_ref[...] = xn.astype(BF16)
        else:
            for r in range(d):
                for c in range(n_col):
                    xn_ref[r * rows:(r + 1) * rows, c * LANES:(c + 1) * LANES] = (
                        xs_ref[c, pl.ds(r, rows, stride=d), :].astype(BF16))
        y_ref[...] = _dot(xn_ref[...], w_ref[:, gi * C_B:(gi + 1) * C_B])
        out_ref = outs[gi]

        def chunk(ci, carry, gi=gi, d=d, rows=rows, out_ref=out_ref):
            r0 = pl.multiple_of(ci * PROJ_CHUNK, PROJ_CHUNK)
            cos = cos_ref[gi, pl.ds(r0, PROJ_CHUNK), :]
            sin = sin_ref[gi, pl.ds(r0, PROJ_CHUNK), :]
            for h in range(H_B):
                hs = slice(h * HEAD_DIM_B, (h + 1) * HEAD_DIM_B)
                yh = y_ref[pl.ds(r0, PROJ_CHUNK), hs]
                rs = lax.rsqrt(jnp.mean(yh * yh, axis=-1, keepdims=True) + NORM_EPS) * scale
                yg = yh * hg_ref[:, gi * C_B + h * HEAD_DIM_B:gi * C_B + (h + 1) * HEAD_DIM_B]
                res = ((yg * cos + _rot_half(yg) * sin) * rs).astype(BF16)
                if rows >= PROJ_CHUNK:
                    per = rows // PROJ_CHUNK
                    out_ref[0, ci // per, pl.ds(pl.multiple_of((ci % per) * PROJ_CHUNK, PROJ_CHUNK), PROJ_CHUNK),
                            hs] = res
                else:
                    per = PROJ_CHUNK // rows
                    for s in range(per):
                        out_ref[0, ci * per + s, :, hs] = res[s * rows:(s + 1) * rows]
            return carry

        lax.fori_loop(0, tm // PROJ_CHUNK, chunk, 0)

        if has_v:
            yv = _dot(xn_ref[...], w_ref[:, (N_GROUPS + gi) * C_B:(N_GROUPS + gi + 1) * C_B])
            for r in range(d):
                outs[N_GROUPS + gi][0, r] = yv[r * rows:(r + 1) * rows].astype(BF16)


def _group_proj(x2d, gain, w, head_gain, has_v, scale, cos_b, sin_b, batch, seq):
    tm = PROJ_TILE
    nt = seq // tm
    n_rope = N_GROUPS * C_B
    n_out = w.shape[1]
    in_specs = [pl.BlockSpec((tm, D_MODEL), lambda i: (i, 0)),
                pl.BlockSpec((1, D_MODEL), lambda i: (0, 0)),
                pl.BlockSpec((D_MODEL, n_out), lambda i: (0, 0), pipeline_mode=pl.Buffered(1)),
                pl.BlockSpec((1, n_rope), lambda i: (0, 0)),
                pl.BlockSpec((N_GROUPS, tm, LANES), lambda i: (0, i, 0)),
                pl.BlockSpec((N_GROUPS, tm, LANES), lambda i: (0, i, 0))]
    shapes, specs = [], []
    for _ in range(2 if has_v else 1):
        for _, d in DILATED_GROUPS:
            shapes.append(jax.ShapeDtypeStruct((batch, d, seq // d, C_B), BF16))
            specs.append(pl.BlockSpec((1, d, tm // d, C_B), lambda i: (i // nt, 0, i % nt, 0)))
    return pl.pallas_call(
        functools.partial(_group_proj_kernel, has_v=has_v, scale=scale),
        out_shape=tuple(shapes),
        grid=(batch * nt,),
        in_specs=in_specs,
        out_specs=tuple(specs),
        scratch_shapes=[pltpu.VMEM((D_MODEL // LANES, tm, LANES), F32), pltpu.VMEM((tm, D_MODEL), BF16),
                        pltpu.VMEM((tm, C_B), F32)],
        compiler_params=_params(("parallel",)),
        name="group_proj_kv" if has_v else "group_proj_q",
    )(x2d, gain.reshape(1, D_MODEL), w.astype(BF16), head_gain.reshape(1, n_rope), cos_b, sin_b)


BAND_HEADS = 4


def _band_kernel(q_ref, k_ref, v_ref, o_ref, lse_ref, *, dilation, length):
    nk = min(2 * BAND, length)
    row = lax.broadcasted_iota(jnp.int32, (BAND, nk), 0)
    col = lax.broadcasted_iota(jnp.int32, (BAND, nk), 1)
    lane_head = lax.broadcasted_iota(jnp.int32, (BAND, LANES), 1) // (LANES // BAND_HEADS)

    def block(r, i):
        q0 = i * BAND
        start = jnp.maximum(q0 + BAND - nk, 0)
        dist = (q0 - start) + row - col
        valid = jnp.logical_and(dist >= 0, dist <= BAND)
        if not isinstance(q0, int):
            q0 = pl.multiple_of(q0, BAND)
            start = pl.multiple_of(start, BAND)
        scores = []
        for h in range(BAND_HEADS):
            hs = slice(h * HEAD_DIM_B, (h + 1) * HEAD_DIM_B)
            s = _dot_nt(q_ref[0, r, pl.ds(q0, BAND), hs], k_ref[0, r, pl.ds(start, nk), hs])
            scores.append(jnp.where(valid, s, NEG_INF))
        probs, denoms, lse_tile = [], [], jnp.zeros((BAND, LANES), F32)
        for h in range(BAND_HEADS):
            m = jnp.max(scores[h], axis=-1, keepdims=True)
            p = jnp.exp2(scores[h] - m)
            denom = jnp.sum(p, axis=-1, keepdims=True)
            probs.append(p.astype(BF16))
            denoms.append(denom)
            lse_tile = jnp.where(lane_head == h, m + jnp.log2(denom), lse_tile)
        tok = pl.ds(q0 * dilation + r, BAND, stride=dilation) if dilation > 1 else pl.ds(q0, BAND)
        for h in range(BAND_HEADS):
            hs = slice(h * HEAD_DIM_B, (h + 1) * HEAD_DIM_B)
            out = _dot(probs[h], v_ref[0, r, pl.ds(start, nk), hs])
            o_ref[0, h, tok, :] = out / denoms[h]
        lse_ref[0, 0, tok, :] = lse_tile

    nb = length // BAND
    for r in range(dilation):
        if nb == 1:
            block(r, 0)
        else:
            def body(i, carry, r=r):
                block(r, i)
                return carry
            lax.fori_loop(0, nb, body, 0)


def _band_attention(q, k, v, dilation, batch, seq):
    length = seq // dilation
    halves = H_B // BAND_HEADS
    cw = BAND_HEADS * HEAD_DIM_B
    blk = pl.BlockSpec((1, dilation, length, cw), lambda b, hh: (b, 0, 0, hh))
    return pl.pallas_call(
        functools.partial(_band_kernel, dilation=dilation, length=length),
        out_shape=(jax.ShapeDtypeStruct((batch, H_B, seq, HEAD_DIM_B), F32),
                   jax.ShapeDtypeStruct((batch, halves, seq, LANES), F32)),
        grid=(batch, halves),
        in_specs=[blk, blk, blk],
        out_specs=(pl.BlockSpec((1, BAND_HEADS, seq, HEAD_DIM_B), lambda b, hh: (b, hh, 0, 0)),
                   pl.BlockSpec((1, 1, seq, LANES), lambda b, hh: (b, hh, 0, 0))),
        compiler_params=_params(("parallel", "parallel")),
        name=f"band_attn_d{dilation}",
    )(q, k, v)


def _combine_proj_kernel(o0_ref, o1_ref, o2_ref, l0_ref, l1_ref, l2_ref, w_ref, x_ref, y_ref, o_scr):
    lanes_per_head = LANES // BAND_HEADS
    for half in range(H_B // BAND_HEADS):
        l0, l1, l2 = l0_ref[0, half], l1_ref[0, half], l2_ref[0, half]
        mx = jnp.maximum(jnp.maximum(l0, l1), l2)
        e0, e1, e2 = jnp.exp2(l0 - mx), jnp.exp2(l1 - mx), jnp.exp2(l2 - mx)
        inv = 1.0 / (e0 + e1 + e2)
        w0, w1, w2 = e0 * inv, e1 * inv, e2 * inv
        for hq in range(BAND_HEADS):
            h = half * BAND_HEADS + hq
            c = hq * lanes_per_head
            o = (w0[:, c:c + 1] * o0_ref[0, h] + w1[:, c:c + 1] * o1_ref[0, h] + w2[:, c:c + 1] * o2_ref[0, h])
            o_scr[:, h * HEAD_DIM_B:(h + 1) * HEAD_DIM_B] = o.astype(BF16)
    y_ref[...] = x_ref[...] + _dot(o_scr[...], w_ref[...])


def _combine_proj(outs, lses, w_o, x2d, seq):
    T = x2d.shape[0]
    tm = 512
    nt = seq // tm
    halves = H_B // BAND_HEADS
    o_spec = pl.BlockSpec((1, H_B, tm, HEAD_DIM_B), lambda i: (i // nt, 0, i % nt, 0))
    l_spec = pl.BlockSpec((1, halves, tm, LANES), lambda i: (i // nt, 0, i % nt, 0))
    return pl.pallas_call(
        _combine_proj_kernel,
        out_shape=jax.ShapeDtypeStruct((T, D_MODEL), F32),
        grid=(T // tm,),
        in_specs=[o_spec, o_spec, o_spec, l_spec, l_spec, l_spec,
                  pl.BlockSpec((C_B, D_MODEL), lambda i: (0, 0)),
                  pl.BlockSpec((tm, D_MODEL), lambda i: (i, 0))],
        out_specs=pl.BlockSpec((tm, D_MODEL), lambda i: (i, 0)),
        scratch_shapes=[pltpu.VMEM((tm, C_B), BF16)],
        compiler_params=_params(("parallel",)),
        name="combine_proj",
    )(*outs, *lses, w_o.astype(BF16), x2d)


def kernel(x, positions, attn_norm, mlp_norm, mla_w_in, mla_qa_norm, mla_kva_norm, mla_w_qb, mla_w_kvb,
           mla_q_norm, mla_k_norm, mla_w_o, kv_norm, w_kv, k_norm_b, w_q_b, q_norm_b, w_o_b, mlp_w1, mlp_w2):
    B, S, D = x.shape
    T = B * S
    cos_a, sin_a, cos_b, sin_b = _rope_tables(positions)
    h = x.reshape(T, D)

    def head_gains(gn):
        return jnp.broadcast_to(gn[:, None, :], (N_GROUPS, H_B, HEAD_DIM_B))

    for a in range(N_A_LAYERS):
        q, k, v = _mla_proj(h, attn_norm[a], mla_w_in[a], mla_qa_norm[a], mla_kva_norm[a], mla_w_qb[a],
                            mla_w_kvb[a], mla_q_norm[a], mla_k_norm[a], cos_a, sin_a)
        o = _mla_attention(q, k, v, B, S)
        h = _out_proj(o, mla_w_o[a], h)
        h = _mlp(h, mlp_norm[a], mlp_w1[a], mlp_w2[a])

    kv = _group_proj(h, kv_norm, w_kv, head_gains(k_norm_b), True, 1.0, cos_b, sin_b, B, S)
    ks, vs = kv[:N_GROUPS], kv[N_GROUPS:]

    for b in range(N_B_LAYERS):
        layer = N_A_LAYERS + b
        qs = _group_proj(h, attn_norm[layer], w_q_b[b], head_gains(q_norm_b[b]), False,
                         HEAD_DIM_B ** -0.5 * LOG2E, cos_b, sin_b, B, S)
        outs, lses = [], []
        for g, (window, dilation) in enumerate(DILATED_GROUPS):
            assert window // dilation == BAND
            o, lse = _band_attention(qs[g], ks[g], vs[g], dilation, B, S)
            outs.append(o)
            lses.append(lse)
        h = _combine_proj(outs, lses, w_o_b[b], h, S)
        h = _mlp(h, mlp_norm[layer], mlp_w1[layer], mlp_w2[layer])

    return h.reshape(B, S, D)
```

```python
import functools

import jax
import jax.numpy as jnp
from jax import lax
from jax.experimental import pallas as pl
from jax.experimental.pallas import tpu as pltpu

D_MODEL = 1024
N_A_LAYERS = 2
N_B_LAYERS = 2
H_A = 16
QK_NOPE = 128
QK_ROPE = 64
QK_HEAD = QK_NOPE + QK_ROPE
V_HEAD = 128
Q_LORA = 256
KV_LORA = 128
DILATED_GROUPS = ((128, 1), (512, 4), (2048, 16))
N_GROUPS = 3
H_B = 8
HEAD_DIM_B = 128
C_B = H_B * HEAD_DIM_B
D_FF = 4 * D_MODEL
ROPE_THETA = 10000.0
NORM_EPS = 1e-6
NEG_INF = -1e30
LOG2E = 1.4426950408889634

LANES = 128
QK_SLOT = 2 * LANES
VMEM_LIMIT = 56 * 1024 * 1024
BAND = 128
PROJ_TILE = 512
PROJ_CHUNK = 128
SLAB_W = 2 * LANES
SLABS = C_B // SLAB_W

BF16 = jnp.bfloat16
F32 = jnp.float32


def _params(semantics):
    return pltpu.CompilerParams(dimension_semantics=semantics, vmem_limit_bytes=VMEM_LIMIT)


def _rms(x, gain):
    ms = jnp.mean(x * x, axis=-1, keepdims=True)
    return x * lax.rsqrt(ms + NORM_EPS) * gain


def _rot_half(u):
    return pltpu.roll(u, LANES // 2, axis=1)


def _dot(a, b):
    return jnp.dot(a, b, preferred_element_type=F32)


def _dot_nt(a, b):
    return lax.dot_general(a, b, (((1,), (1,)), ((), ())), preferred_element_type=F32)


def _tables_kernel(pos_ref, f_ref, cm_ref, sm_ref, cos_ref, sin_ref):
    ang = pos_ref[...] * f_ref[...]
    cos_ref[...] = jnp.cos(ang) * cm_ref[...]
    sin_ref[...] = jnp.sin(ang) * sm_ref[...]


def _rope_table(pos, freq, cos_mask, sin_sign):
    n = pos.shape[0]
    tm = 1024
    row = pl.BlockSpec((1, LANES), lambda i: (0, 0))
    tab = pl.BlockSpec((tm, LANES), lambda i: (i, 0))
    shp = jax.ShapeDtypeStruct((n, LANES), F32)
    return pl.pallas_call(
        _tables_kernel,
        out_shape=(shp, shp),
        grid=(n // tm,),
        in_specs=[pl.BlockSpec((tm, 1), lambda i: (i, 0)), row, row, row],
        out_specs=(tab, tab),
        compiler_params=_params(("parallel",)),
        name="rope_table",
    )(pos.reshape(n, 1), freq.reshape(1, LANES), cos_mask.reshape(1, LANES), sin_sign.reshape(1, LANES))


def _rope_tables(positions):
    B, S = positions.shape
    pos = positions.astype(F32)
    inv_a = ROPE_THETA ** (-jnp.arange(0, QK_ROPE, 2, dtype=F32) / QK_ROPE)
    inv_b = ROPE_THETA ** (-jnp.arange(0, HEAD_DIM_B, 2, dtype=F32) / HEAD_DIM_B)
    z32 = jnp.zeros((32,), F32)
    o32 = jnp.ones((32,), F32)
    o64 = jnp.ones((64,), F32)
    cos_a, sin_a = _rope_table(pos.reshape(B * S), jnp.concatenate([inv_a, z32, inv_a, z32]),
                               jnp.concatenate([o32, z32, o32, z32]), jnp.concatenate([-o32, z32, o32, z32]))
    nt = S // PROJ_TILE
    orders = [pos.reshape(B, nt, PROJ_TILE // d, d).transpose(0, 1, 3, 2).reshape(B * S) for _, d in DILATED_GROUPS]
    cos_b, sin_b = _rope_table(jnp.concatenate(orders), jnp.concatenate([inv_b, inv_b]),
                               jnp.concatenate([o64, o64]), jnp.concatenate([-o64, o64]))
    return cos_a, sin_a, cos_b.reshape(N_GROUPS, B * S, LANES), sin_b.reshape(N_GROUPS, B * S, LANES)


def _mla_proj_kernel(x_ref, g_ref, win_ref, qa_ref, kva_ref, wqb_ref, wkb_ref, wvb_ref, qg_ref, kg_ref,
                     cos_ref, sin_ref, q_ref, k_ref, v_ref, q_scr, kn_scr, kpe_scr, *, scale):
    tm = x_ref.shape[0]
    xn = _rms(x_ref[...], g_ref[...]).astype(BF16)
    lat = _dot(xn, win_ref[...])
    cqn = _rms(lat[:, :Q_LORA], qa_ref[...]).astype(BF16)
    ckvn = _rms(lat[:, Q_LORA:Q_LORA + KV_LORA], kva_ref[...]).astype(BF16)
    kpe_scr[...] = lat[:, Q_LORA + KV_LORA:]
    v_ref[...] = _dot(ckvn, wvb_ref[...]).astype(BF16)
    q_scr[...] = _dot(cqn, wqb_ref[...])
    kn_scr[...] = _dot(ckvn, wkb_ref[...])

    def chunk(ci, carry):
        rows = pl.ds(pl.multiple_of(ci * PROJ_CHUNK, PROJ_CHUNK), PROJ_CHUNK)
        cos = cos_ref[rows, :]
        sin = sin_ref[rows, :]
        qg_n, qg_pe = qg_ref[:, :LANES], qg_ref[:, LANES:]
        kg_n, kg_pe = kg_ref[:, :LANES], kg_ref[:, LANES:]
        k_pe = kpe_scr[rows, :]
        kpe_ss = jnp.sum(k_pe * k_pe, axis=-1, keepdims=True)
        kpe_g = k_pe * kg_pe
        kpe_rot = kpe_g * cos + _rot_half(kpe_g) * sin
        for h in range(H_A):
            qn = q_scr[rows, h * QK_SLOT:h * QK_SLOT + LANES]
            qp = q_scr[rows, h * QK_SLOT + LANES:(h + 1) * QK_SLOT]
            ss = jnp.sum(qn * qn + qp * qp, axis=-1, keepdims=True)
            rs = lax.rsqrt(ss * (1.0 / QK_HEAD) + NORM_EPS) * scale
            qpg = qp * qg_pe
            q_ref[rows, h * QK_SLOT:h * QK_SLOT + LANES] = (qn * rs * qg_n).astype(BF16)
            q_ref[rows, h * QK_SLOT + LANES:(h + 1) * QK_SLOT] = (
                (qpg * cos + _rot_half(qpg) * sin) * rs).astype(BF16)
            kn = kn_scr[rows, h * LANES:(h + 1) * LANES]
            ssk = jnp.sum(kn * kn, axis=-1, keepdims=True) + kpe_ss
            rsk = lax.rsqrt(ssk * (1.0 / QK_HEAD) + NORM_EPS)
            k_ref[rows, h * QK_SLOT:h * QK_SLOT + LANES] = (kn * rsk * kg_n).astype(BF16)
            k_ref[rows, h * QK_SLOT + LANES:(h + 1) * QK_SLOT] = (kpe_rot * rsk).astype(BF16)
        return carry

    lax.fori_loop(0, tm // PROJ_CHUNK, chunk, 0)


def _rope_tile_cols(a):
    z = jnp.zeros(a.shape[:-1] + (32,), a.dtype)
    return jnp.concatenate([a[..., :32], z, a[..., 32:], z], axis=-1)


def _mla_proj(x2d, gain, w_in, qa_norm, kva_norm, w_qb, w_kvb, q_norm, k_norm, cos_a, sin_a):
    T = x2d.shape[0]
    tm = 256
    w_in_p = jnp.concatenate(
        [w_in[:, :Q_LORA + KV_LORA], _rope_tile_cols(w_in[:, Q_LORA + KV_LORA:])], axis=-1).astype(BF16)
    wq = w_qb.reshape(Q_LORA, H_A, QK_HEAD)
    wq_p = jnp.concatenate([wq[..., :QK_NOPE], _rope_tile_cols(wq[..., QK_NOPE:])], axis=-1)
    wq_p = wq_p.reshape(Q_LORA, H_A * QK_SLOT).astype(BF16)
    wkv = w_kvb.reshape(KV_LORA, H_A, QK_NOPE + V_HEAD)
    wkb = wkv[..., :QK_NOPE].reshape(KV_LORA, H_A * QK_NOPE).astype(BF16)
    wvb = wkv[..., QK_NOPE:].reshape(KV_LORA, H_A * V_HEAD).astype(BF16)
    qg = jnp.concatenate([q_norm[:QK_NOPE], _rope_tile_cols(q_norm[QK_NOPE:])]).reshape(1, QK_SLOT)
    kg = jnp.concatenate([k_norm[:QK_NOPE], _rope_tile_cols(k_norm[QK_NOPE:])]).reshape(1, QK_SLOT)

    def const(shape):
        return pl.BlockSpec(shape, lambda i: (0, 0))

    def rows(width):
        return pl.BlockSpec((tm, width), lambda i: (i, 0))

    n_in = Q_LORA + KV_LORA + LANES
    return pl.pallas_call(
        functools.partial(_mla_proj_kernel, scale=QK_HEAD ** -0.5 * LOG2E),
        out_shape=(jax.ShapeDtypeStruct((T, H_A * QK_SLOT), BF16),
                   jax.ShapeDtypeStruct((T, H_A * QK_SLOT), BF16),
                   jax.ShapeDtypeStruct((T, H_A * V_HEAD), BF16)),
        grid=(T // tm,),
        in_specs=[rows(D_MODEL), const((1, D_MODEL)), const((D_MODEL, n_in)), const((1, Q_LORA)),
                  const((1, KV_LORA)), const((Q_LORA, H_A * QK_SLOT)), const((KV_LORA, H_A * QK_NOPE)),
                  const((KV_LORA, H_A * V_HEAD)), const((1, QK_SLOT)), const((1, QK_SLOT)),
                  rows(LANES), rows(LANES)],
        out_specs=(rows(H_A * QK_SLOT), rows(H_A * QK_SLOT), rows(H_A * V_HEAD)),
        scratch_shapes=[pltpu.VMEM((tm, H_A * QK_SLOT), F32), pltpu.VMEM((tm, H_A * QK_NOPE), F32),
                        pltpu.VMEM((tm, LANES), F32)],
        compiler_params=_params(("parallel",)),
        name="mla_proj",
    )(x2d, gain.reshape(1, D_MODEL), w_in_p, qa_norm.reshape(1, Q_LORA), kva_norm.reshape(1, KV_LORA),
      wq_p, wkb, wvb, qg, kg, cos_a, sin_a)


FLASH_HEADS = 2


FLASH_ROWS = 64


def _flash_kernel(q_ref, k_ref, v_ref, o_ref, s_scr, p_scr, m_scr, l_scr, a_scr, acc_scr, *, tq, tk):
    i = pl.program_id(2)
    m_scr[...] = jnp.full(m_scr.shape, NEG_INF, F32)
    l_scr[...] = jnp.zeros(l_scr.shape, F32)
    acc_scr[...] = jnp.zeros(acc_scr.shape, F32)

    def step(j, masked):
        off = pl.multiple_of(j * tk, tk)
        for h in range(FLASH_HEADS):
            s_scr[h] = _dot_nt(q_ref[:, h * QK_SLOT:(h + 1) * QK_SLOT],
                               k_ref[pl.ds(off, tk), h * QK_SLOT:(h + 1) * QK_SLOT])
        for h in range(FLASH_HEADS):
            for rb in range(tq // FLASH_ROWS):
                rows = slice(rb * FLASH_ROWS, (rb + 1) * FLASH_ROWS)
                cw = min(tk, -(-((rb + 1) * FLASH_ROWS) // LANES) * LANES) if masked else tk
                s = s_scr[h, rows, :cw]
                if masked:
                    row = lax.broadcasted_iota(jnp.int32, (FLASH_ROWS, cw), 0) + rb * FLASH_ROWS
                    col = lax.broadcasted_iota(jnp.int32, (FLASH_ROWS, cw), 1)
                    s = jnp.where(row >= col, s, NEG_INF)
                m_old = m_scr[h, rows, :]
                m_new = jnp.maximum(m_old, jnp.max(s, axis=-1, keepdims=True))
                p = jnp.exp2(s - jnp.tile(m_new, (1, cw // LANES)))
                alpha = jnp.exp2(m_old - m_new)
                l_scr[h, rows, :] = alpha * l_scr[h, rows, :] + jnp.sum(p, axis=-1, keepdims=True)
                m_scr[h, rows, :] = m_new
                a_scr[h, rows, :] = alpha
                p_scr[h, rows, :cw] = p.astype(BF16)
                if cw < tk:
                    p_scr[h, rows, cw:] = jnp.zeros((FLASH_ROWS, tk - cw), BF16)
            pv = _dot(p_scr[h], v_ref[pl.ds(off, tk), h * V_HEAD:(h + 1) * V_HEAD])
            acc_scr[h] = a_scr[h] * acc_scr[h] + pv

    def body(j, carry):
        step(j, False)
        return carry

    lax.fori_loop(0, i * (tq // tk), body, 0)
    step(i * (tq // tk), True)
    for h in range(FLASH_HEADS):
        o_ref[:, h * V_HEAD:(h + 1) * V_HEAD] = (acc_scr[h] / l_scr[h]).astype(o_ref.dtype)


def _mla_attention(q, k, v, batch, seq):
    tq = tk = 512
    nq = seq // tq
    T = q.shape[0]
    hh = FLASH_HEADS
    stat = pltpu.VMEM((hh, tq, LANES), F32)
    return pl.pallas_call(
        functools.partial(_flash_kernel, tq=tq, tk=tk),
        out_shape=jax.ShapeDtypeStruct((T, H_A * V_HEAD), BF16),
        grid=(batch, H_A // hh, nq),
        in_specs=[pl.BlockSpec((tq, hh * QK_SLOT), lambda b, h, i: (b * nq + i, h)),
                  pl.BlockSpec((seq, hh * QK_SLOT), lambda b, h, i: (b, h)),
                  pl.BlockSpec((seq, hh * V_HEAD), lambda b, h, i: (b, h))],
        out_specs=pl.BlockSpec((tq, hh * V_HEAD), lambda b, h, i: (b * nq + i, h)),
        scratch_shapes=[pltpu.VMEM((hh, tq, tk), F32), pltpu.VMEM((hh, tq, tk), BF16), stat, stat, stat,
                        pltpu.VMEM((hh, tq, V_HEAD), F32)],
        compiler_params=_params(("parallel", "parallel", "arbitrary")),
        name="mla_flash",
    )(q, k, v)


def _out_proj_kernel(o_ref, w_ref, x_ref, y_ref):
    y_ref[...] = x_ref[...] + _dot(o_ref[...], w_ref[...])


def _out_proj(o, w_o, x2d):
    T, K = o.shape
    tm = 512
    return pl.pallas_call(
        _out_proj_kernel,
        out_shape=jax.ShapeDtypeStruct((T, D_MODEL), F32),
        grid=(T // tm,),
        in_specs=[pl.BlockSpec((tm, K), lambda i: (i, 0)),
                  pl.BlockSpec((K, D_MODEL), lambda i: (0, 0)),
                  pl.BlockSpec((tm, D_MODEL), lambda i: (i, 0))],
        out_specs=pl.BlockSpec((tm, D_MODEL), lambda i: (i, 0)),
        compiler_params=_params(("parallel",)),
        name="out_proj",
    )(o, w_o.astype(BF16), x2d)


def _mlp_kernel(x_ref, g_ref, w1_ref, w2_ref, y_ref, xn_ref, acc_ref):
    f = pl.program_id(1)

    @pl.when(f == 0)
    def _():
        xn_ref[...] = _rms(x_ref[...], g_ref[...]).astype(BF16)
        acc_ref[...] = jnp.zeros_like(acc_ref)

    h = jnp.maximum(_dot(xn_ref[...], w1_ref[...]), 0.0)
    acc_ref[...] += _dot((h * h).astype(BF16), w2_ref[...])

    @pl.when(f == pl.num_programs(1) - 1)
    def _():
        y_ref[...] = x_ref[...] + acc_ref[...]


def _mlp(x2d, gain, w1, w2):
    T = x2d.shape[0]
    tm, tf = 512, 1024
    return pl.pallas_call(
        _mlp_kernel,
        out_shape=jax.ShapeDtypeStruct((T, D_MODEL), F32),
        grid=(T // tm, D_FF // tf),
        in_specs=[pl.BlockSpec((tm, D_MODEL), lambda i, f: (i, 0)),
                  pl.BlockSpec((1, D_MODEL), lambda i, f: (0, 0)),
                  pl.BlockSpec((D_MODEL, tf), lambda i, f: (0, f)),
                  pl.BlockSpec((tf, D_MODEL), lambda i, f: (f, 0))],
        out_specs=pl.BlockSpec((tm, D_MODEL), lambda i, f: (i, 0)),
        scratch_shapes=[pltpu.VMEM((tm, D_MODEL), BF16), pltpu.VMEM((tm, D_MODEL), F32)],
        compiler_params=_params(("parallel", "arbitrary")),
        name="mlp",
    )(x2d, gain.reshape(1, D_MODEL), w1.astype(BF16), w2.astype(BF16))


def _group_proj_kernel(*refs, has_v, scale):
    if has_v:
        x_ref, g_ref, w_ref, hg_ref, cos_ref, sin_ref = refs[:6]
        outs = refs[6:12]
        xs_ref, xn_ref, y_ref = refs[12:]
    else:
        x_ref, g_ref, w_ref, hg_ref, cos_ref, sin_ref = refs[:6]
        outs = refs[6:9]
        xs_ref, xn_ref, y_ref = refs[9:]
    tm = x_ref.shape[0]
    n_col = D_MODEL // LANES
    n_chunks = tm // PROJ_CHUNK
    assert n_chunks == SLABS
    xn = _rms(x_ref[...], g_ref[...])
    for c in range(n_col):
        xs_ref[c] = xn[:, c * LANES:(c + 1) * LANES]

    def permute(gi, buf):
        d = DILATED_GROUPS[gi][1]
        rows = tm // d
        if d == 1:
            xn_ref[buf] = xn.astype(BF16)
        else:
            for r in range(d):
                for c in range(n_col):
                    xn_ref[buf, r * rows:(r + 1) * rows, c * LANES:(c + 1) * LANES] = (
                        xs_ref[c, pl.ds(r, rows, stride=d), :].astype(BF16))

    def norm_chunk(gi, buf, ci):
        d = DILATED_GROUPS[gi][1]
        rows = tm // d
        out_ref = outs[gi]
        r0 = pl.multiple_of(ci * PROJ_CHUNK, PROJ_CHUNK)
        cos = cos_ref[gi, pl.ds(r0, PROJ_CHUNK), :]
        sin = sin_ref[gi, pl.ds(r0, PROJ_CHUNK), :]
        for h in range(H_B):
            hs = slice(h * HEAD_DIM_B, (h + 1) * HEAD_DIM_B)
            sl = h * HEAD_DIM_B // SLAB_W
            lo = h * HEAD_DIM_B % SLAB_W
            yh = y_ref[buf, sl, pl.ds(r0, PROJ_CHUNK), lo:lo + HEAD_DIM_B]
            rs = lax.rsqrt(jnp.mean(yh * yh, axis=-1, keepdims=True) + NORM_EPS) * scale
            yg = yh * hg_ref[:, gi * C_B + h * HEAD_DIM_B:gi * C_B + (h + 1) * HEAD_DIM_B]
            res = ((yg * cos + _rot_half(yg) * sin) * rs).astype(BF16)
            if rows >= PROJ_CHUNK:
                per = rows // PROJ_CHUNK
                out_ref[0, ci // per, pl.ds(pl.multiple_of((ci % per) * PROJ_CHUNK, PROJ_CHUNK), PROJ_CHUNK),
                        hs] = res
            else:
                per = PROJ_CHUNK // rows
                for s in range(per):
                    out_ref[0, ci * per + s, :, hs] = res[s * rows:(s + 1) * rows]

    def v_slab(gi, buf, ci):
        d = DILATED_GROUPS[gi][1]
        rows = tm // d
        yv = _dot(xn_ref[buf], w_ref[(N_GROUPS + gi) * SLABS + ci])
        for r in range(d):
            outs[N_GROUPS + gi][0, r, ci] = yv[r * rows:(r + 1) * rows].astype(BF16)

    permute(0, 0)
    for s in range(SLABS):
        y_ref[0, s] = _dot(xn_ref[0], w_ref[s])
    for gi in range(N_GROUPS):
        buf = gi % 2
        if gi + 1 < N_GROUPS:
            permute(gi + 1, 1 - buf)

        def body(ci, carry, gi=gi, buf=buf):
            norm_chunk(gi, buf, ci)
            if gi + 1 < N_GROUPS:
                y_ref[1 - buf, ci] = _dot(xn_ref[1 - buf], w_ref[(gi + 1) * SLABS + ci])
            if has_v:
                v_slab(gi, buf, ci)
            return carry

        lax.fori_loop(0, n_chunks, body, 0)


def _group_proj(x2d, gain, w, head_gain, has_v, scale, cos_b, sin_b, batch, seq):
    tm = PROJ_TILE
    nt = seq // tm
    n_rope = N_GROUPS * C_B
    n_slabs = w.shape[1] // SLAB_W
    w_slabs = w.astype(BF16).reshape(D_MODEL, n_slabs, SLAB_W).transpose(1, 0, 2)
    in_specs = [pl.BlockSpec((tm, D_MODEL), lambda i: (i, 0)),
                pl.BlockSpec((1, D_MODEL), lambda i: (0, 0)),
                pl.BlockSpec((n_slabs, D_MODEL, SLAB_W), lambda i: (0, 0, 0), pipeline_mode=pl.Buffered(1)),
                pl.BlockSpec((1, n_rope), lambda i: (0, 0)),
                pl.BlockSpec((N_GROUPS, tm, LANES), lambda i: (0, i, 0)),
                pl.BlockSpec((N_GROUPS, tm, LANES), lambda i: (0, i, 0))]
    shapes, specs = [], []
    for _, d in DILATED_GROUPS:
        shapes.append(jax.ShapeDtypeStruct((batch, d, seq // d, C_B), BF16))
        specs.append(pl.BlockSpec((1, d, tm // d, C_B), lambda i: (i // nt, 0, i % nt, 0)))
    if has_v:
        for _, d in DILATED_GROUPS:
            shapes.append(jax.ShapeDtypeStruct((batch, d, SLABS, seq // d, SLAB_W), BF16))
            specs.append(pl.BlockSpec((1, d, SLABS, tm // d, SLAB_W), lambda i: (i // nt, 0, 0, i % nt, 0)))
    return pl.pallas_call(
        functools.partial(_group_proj_kernel, has_v=has_v, scale=scale),
        out_shape=tuple(shapes),
        grid=(batch * nt,),
        in_specs=in_specs,
        out_specs=tuple(specs),
        scratch_shapes=[pltpu.VMEM((D_MODEL // LANES, tm, LANES), F32), pltpu.VMEM((2, tm, D_MODEL), BF16),
                        pltpu.VMEM((2, SLABS, tm, SLAB_W), F32)],
        compiler_params=_params(("parallel",)),
        name="group_proj_kv" if has_v else "group_proj_q",
    )(x2d, gain.reshape(1, D_MODEL), w_slabs, head_gain.reshape(1, n_rope), cos_b, sin_b)


BAND_HEADS = 4


def _band_kernel(q_ref, k_ref, v_ref, o_ref, lse_ref, *, dilation, length):
    nk = min(2 * BAND, length)
    row = lax.broadcasted_iota(jnp.int32, (BAND, nk), 0)
    col = lax.broadcasted_iota(jnp.int32, (BAND, nk), 1)
    lane_head = lax.broadcasted_iota(jnp.int32, (BAND, LANES), 1) // (LANES // BAND_HEADS)

    def block(r, i):
        q0 = i * BAND
        start = jnp.maximum(q0 + BAND - nk, 0)
        dist = (q0 - start) + row - col
        valid = jnp.logical_and(dist >= 0, dist <= BAND)
        if not isinstance(q0, int):
            q0 = pl.multiple_of(q0, BAND)
            start = pl.multiple_of(start, BAND)
        scores = []
        for h in range(BAND_HEADS):
            hs = slice(h * HEAD_DIM_B, (h + 1) * HEAD_DIM_B)
            s = _dot_nt(q_ref[0, r, pl.ds(q0, BAND), hs], k_ref[0, r, pl.ds(start, nk), hs])
            scores.append(jnp.where(valid, s, NEG_INF))
        probs, denoms, lse_tile = [], [], jnp.zeros((BAND, LANES), F32)
        for h in range(BAND_HEADS):
            m = jnp.max(scores[h], axis=-1, keepdims=True)
            p = jnp.exp2(scores[h] - m)
            denom = jnp.sum(p, axis=-1, keepdims=True)
            probs.append(p.astype(BF16))
            denoms.append(denom)
            lse_tile = jnp.where(lane_head == h, m + jnp.log2(denom), lse_tile)
        tok = pl.ds(q0 * dilation + r, BAND, stride=dilation) if dilation > 1 else pl.ds(q0, BAND)
        for h in range(BAND_HEADS):
            hs = slice(h * HEAD_DIM_B, (h + 1) * HEAD_DIM_B)
            lo = h * HEAD_DIM_B % SLAB_W
            out = _dot(probs[h], v_ref[0, r, h * HEAD_DIM_B // SLAB_W, pl.ds(start, nk), lo:lo + HEAD_DIM_B])
            o_ref[0, h, tok, :] = out / denoms[h]
        lse_ref[0, 0, tok, :] = lse_tile

    nb = length // BAND
    for r in range(dilation):
        if nb == 1:
            block(r, 0)
        else:
            def body(i, carry, r=r):
                block(r, i)
                return carry
            lax.fori_loop(0, nb, body, 0, unroll=4)


def _band_attention(q, k, v, dilation, batch, seq):
    length = seq // dilation
    halves = H_B // BAND_HEADS
    cw = BAND_HEADS * HEAD_DIM_B
    blk = pl.BlockSpec((1, dilation, length, cw), lambda b, hh: (b, 0, 0, hh))
    v_blk = pl.BlockSpec((1, dilation, cw // SLAB_W, length, SLAB_W), lambda b, hh: (b, 0, hh, 0, 0))
    return pl.pallas_call(
        functools.partial(_band_kernel, dilation=dilation, length=length),
        out_shape=(jax.ShapeDtypeStruct((batch, H_B, seq, HEAD_DIM_B), F32),
                   jax.ShapeDtypeStruct((batch, halves, seq, LANES), F32)),
        grid=(batch, halves),
        in_specs=[blk, blk, v_blk],
        out_specs=(pl.BlockSpec((1, BAND_HEADS, seq, HEAD_DIM_B), lambda b, hh: (b, hh, 0, 0)),
                   pl.BlockSpec((1, 1, seq, LANES), lambda b, hh: (b, hh, 0, 0))),
        compiler_params=_params(("parallel", "parallel")),
        name=f"band_attn_d{dilation}",
    )(q, k, v)


def _combine_proj_kernel(o0_ref, o1_ref, o2_ref, l0_ref, l1_ref, l2_ref, w_ref, x_ref, y_ref, o_scr):
    lanes_per_head = LANES // BAND_HEADS
    for half in range(H_B // BAND_HEADS):
        l0, l1, l2 = l0_ref[0, half], l1_ref[0, half], l2_ref[0, half]
        mx = jnp.maximum(jnp.maximum(l0, l1), l2)
        e0, e1, e2 = jnp.exp2(l0 - mx), jnp.exp2(l1 - mx), jnp.exp2(l2 - mx)
        inv = 1.0 / (e0 + e1 + e2)
        w0, w1, w2 = e0 * inv, e1 * inv, e2 * inv
        for hq in range(BAND_HEADS):
            h = half * BAND_HEADS + hq
            c = hq * lanes_per_head
            o = (w0[:, c:c + 1] * o0_ref[0, h] + w1[:, c:c + 1] * o1_ref[0, h] + w2[:, c:c + 1] * o2_ref[0, h])
            o_scr[:, h * HEAD_DIM_B:(h + 1) * HEAD_DIM_B] = o.astype(BF16)
    y_ref[...] = x_ref[...] + _dot(o_scr[...], w_ref[...])


def _combine_proj(outs, lses, w_o, x2d, seq):
    T = x2d.shape[0]
    tm = 512
    nt = seq // tm
    halves = H_B // BAND_HEADS
    o_spec = pl.BlockSpec((1, H_B, tm, HEAD_DIM_B), lambda i: (i // nt, 0, i % nt, 0))
    l_spec = pl.BlockSpec((1, halves, tm, LANES), lambda i: (i // nt, 0, i % nt, 0))
    return pl.pallas_call(
        _combine_proj_kernel,
        out_shape=jax.ShapeDtypeStruct((T, D_MODEL), F32),
        grid=(T // tm,),
        in_specs=[o_spec, o_spec, o_spec, l_spec, l_spec, l_spec,
                  pl.BlockSpec((C_B, D_MODEL), lambda i: (0, 0)),
                  pl.BlockSpec((tm, D_MODEL), lambda i: (i, 0))],
        out_specs=pl.BlockSpec((tm, D_MODEL), lambda i: (i, 0)),
        scratch_shapes=[pltpu.VMEM((tm, C_B), BF16)],
        compiler_params=_params(("parallel",)),
        name="combine_proj",
    )(*outs, *lses, w_o.astype(BF16), x2d)


def kernel(x, positions, attn_norm, mlp_norm, mla_w_in, mla_qa_norm, mla_kva_norm, mla_w_qb, mla_w_kvb,
           mla_q_norm, mla_k_norm, mla_w_o, kv_norm, w_kv, k_norm_b, w_q_b, q_norm_b, w_o_b, mlp_w1, mlp_w2):
    B, S, D = x.shape
    T = B * S
    cos_a, sin_a, cos_b, sin_b = _rope_tables(positions)
    h = x.reshape(T, D)

    def head_gains(gn):
        return jnp.broadcast_to(gn[:, None, :], (N_GROUPS, H_B, HEAD_DIM_B))

    for a in range(N_A_LAYERS):
        q, k, v = _mla_proj(h, attn_norm[a], mla_w_in[a], mla_qa_norm[a], mla_kva_norm[a], mla_w_qb[a],
                            mla_w_kvb[a], mla_q_norm[a], mla_k_norm[a], cos_a, sin_a)
        o = _mla_attention(q, k, v, B, S)
        h = _out_proj(o, mla_w_o[a], h)
        h = _mlp(h, mlp_norm[a], mlp_w1[a], mlp_w2[a])

    kv = _group_proj(h, kv_norm, w_kv, head_gains(k_norm_b), True, 1.0, cos_b, sin_b, B, S)
    ks, vs = kv[:N_GROUPS], kv[N_GROUPS:]

    for b in range(N_B_LAYERS):
        layer = N_A_LAYERS + b
        qs = _group_proj(h, attn_norm[layer], w_q_b[b], head_gains(q_norm_b[b]), False,
                         HEAD_DIM_B ** -0.5 * LOG2E, cos_b, sin_b, B, S)
        outs, lses = [], []
        for g, (window, dilation) in enumerate(DILATED_GROUPS):
            assert window // dilation == BAND
            o, lse = _band_attention(qs[g], ks[g], vs[g], dilation, B, S)
            outs.append(o)
            lses.append(lse)
        h = _combine_proj(outs, lses, w_o_b[b], h, S)
        h = _mlp(h, mlp_norm[layer], mlp_w1[layer], mlp_w2[layer])

    return h.reshape(B, S, D)
```

```python
import functools

import jax
import jax.numpy as jnp
from jax import lax
from jax.experimental import pallas as pl
from jax.experimental.pallas import tpu as pltpu

D_MODEL = 1024
N_A_LAYERS = 2
N_B_LAYERS = 2
H_A = 16
QK_NOPE = 128
QK_ROPE = 64
QK_HEAD = QK_NOPE + QK_ROPE
V_HEAD = 128
Q_LORA = 256
KV_LORA = 128
DILATED_GROUPS = ((128, 1), (512, 4), (2048, 16))
N_GROUPS = 3
H_B = 8
HEAD_DIM_B = 128
C_B = H_B * HEAD_DIM_B
D_FF = 4 * D_MODEL
ROPE_THETA = 10000.0
NORM_EPS = 1e-6
NEG_INF = -1e30
LOG2E = 1.4426950408889634

LANES = 128
QK_SLOT = 2 * LANES
VMEM_LIMIT = 56 * 1024 * 1024
BAND = 128
PROJ_TILE = 512
PROJ_CHUNK = 128
SLAB_W = 2 * LANES
SLABS = C_B // SLAB_W

BF16 = jnp.bfloat16
F32 = jnp.float32


def _params(semantics):
    return pltpu.CompilerParams(dimension_semantics=semantics, vmem_limit_bytes=VMEM_LIMIT)


def _rms(x, gain):
    ms = jnp.mean(x * x, axis=-1, keepdims=True)
    return x * lax.rsqrt(ms + NORM_EPS) * gain


def _rot_half(u):
    return pltpu.roll(u, LANES // 2, axis=1)


def _dot(a, b):
    return jnp.dot(a, b, preferred_element_type=F32)


def _dot_nt(a, b):
    return lax.dot_general(a, b, (((1,), (1,)), ((), ())), preferred_element_type=F32)


def _tables_kernel(pos_ref, f_ref, cm_ref, sm_ref, cos_ref, sin_ref):
    ang = pos_ref[...] * f_ref[...]
    cos_ref[...] = jnp.cos(ang) * cm_ref[...]
    sin_ref[...] = jnp.sin(ang) * sm_ref[...]


def _rope_table(pos, freq, cos_mask, sin_sign):
    n = pos.shape[0]
    tm = 1024
    row = pl.BlockSpec((1, LANES), lambda i: (0, 0))
    tab = pl.BlockSpec((tm, LANES), lambda i: (i, 0))
    shp = jax.ShapeDtypeStruct((n, LANES), F32)
    return pl.pallas_call(
        _tables_kernel,
        out_shape=(shp, shp),
        grid=(n // tm,),
        in_specs=[pl.BlockSpec((tm, 1), lambda i: (i, 0)), row, row, row],
        out_specs=(tab, tab),
        compiler_params=_params(("parallel",)),
        name="rope_table",
    )(pos.reshape(n, 1), freq.reshape(1, LANES), cos_mask.reshape(1, LANES), sin_sign.reshape(1, LANES))


def _rope_tables(positions):
    B, S = positions.shape
    pos = positions.astype(F32)
    inv_a = ROPE_THETA ** (-jnp.arange(0, QK_ROPE, 2, dtype=F32) / QK_ROPE)
    inv_b = ROPE_THETA ** (-jnp.arange(0, HEAD_DIM_B, 2, dtype=F32) / HEAD_DIM_B)
    z32 = jnp.zeros((32,), F32)
    o32 = jnp.ones((32,), F32)
    o64 = jnp.ones((64,), F32)
    cos_a, sin_a = _rope_table(pos.reshape(B * S), jnp.concatenate([inv_a, z32, inv_a, z32]),
                               jnp.concatenate([o32, z32, o32, z32]), jnp.concatenate([-o32, z32, o32, z32]))
    cos_b, sin_b = _rope_table(pos.reshape(B * S), jnp.concatenate([inv_b, inv_b]),
                               jnp.concatenate([o64, o64]), jnp.concatenate([-o64, o64]))
    return cos_a, sin_a, cos_b, sin_b


def _mla_proj_kernel(x_ref, g_ref, win_ref, qa_ref, kva_ref, wqb_ref, wkb_ref, wvb_ref, qg_ref, kg_ref,
                     cos_ref, sin_ref, q_ref, k_ref, v_ref, q_scr, kn_scr, kpe_scr, *, scale):
    tm = x_ref.shape[0]
    xn = _rms(x_ref[...], g_ref[...]).astype(BF16)
    lat = _dot(xn, win_ref[...])
    cqn = _rms(lat[:, :Q_LORA], qa_ref[...]).astype(BF16)
    ckvn = _rms(lat[:, Q_LORA:Q_LORA + KV_LORA], kva_ref[...]).astype(BF16)
    kpe_scr[...] = lat[:, Q_LORA + KV_LORA:]
    v_ref[...] = _dot(ckvn, wvb_ref[...]).astype(BF16)
    q_scr[...] = _dot(cqn, wqb_ref[...])
    kn_scr[...] = _dot(ckvn, wkb_ref[...])

    def chunk(ci, carry):
        rows = pl.ds(pl.multiple_of(ci * PROJ_CHUNK, PROJ_CHUNK), PROJ_CHUNK)
        cos = cos_ref[rows, :]
        sin = sin_ref[rows, :]
        qg_n, qg_pe = qg_ref[:, :LANES], qg_ref[:, LANES:]
        kg_n, kg_pe = kg_ref[:, :LANES], kg_ref[:, LANES:]
        k_pe = kpe_scr[rows, :]
        kpe_ss = jnp.sum(k_pe * k_pe, axis=-1, keepdims=True)
        kpe_g = k_pe * kg_pe
        kpe_rot = kpe_g * cos + _rot_half(kpe_g) * sin
        for h in range(H_A):
            qn = q_scr[rows, h * QK_SLOT:h * QK_SLOT + LANES]
            qp = q_scr[rows, h * QK_SLOT + LANES:(h + 1) * QK_SLOT]
            ss = jnp.sum(qn * qn + qp * qp, axis=-1, keepdims=True)
            rs = lax.rsqrt(ss * (1.0 / QK_HEAD) + NORM_EPS) * scale
            qpg = qp * qg_pe
            q_ref[rows, h * QK_SLOT:h * QK_SLOT + LANES] = (qn * rs * qg_n).astype(BF16)
            q_ref[rows, h * QK_SLOT + LANES:(h + 1) * QK_SLOT] = (
                (qpg * cos + _rot_half(qpg) * sin) * rs).astype(BF16)
            kn = kn_scr[rows, h * LANES:(h + 1) * LANES]
            ssk = jnp.sum(kn * kn, axis=-1, keepdims=True) + kpe_ss
            rsk = lax.rsqrt(ssk * (1.0 / QK_HEAD) + NORM_EPS)
            k_ref[rows, h * QK_SLOT:h * QK_SLOT + LANES] = (kn * rsk * kg_n).astype(BF16)
            k_ref[rows, h * QK_SLOT + LANES:(h + 1) * QK_SLOT] = (kpe_rot * rsk).astype(BF16)
        return carry

    lax.fori_loop(0, tm // PROJ_CHUNK, chunk, 0)


def _rope_tile_cols(a):
    z = jnp.zeros(a.shape[:-1] + (32,), a.dtype)
    return jnp.concatenate([a[..., :32], z, a[..., 32:], z], axis=-1)


def _mla_proj(x2d, gain, w_in, qa_norm, kva_norm, w_qb, w_kvb, q_norm, k_norm, cos_a, sin_a):
    T = x2d.shape[0]
    tm = 256
    w_in_p = jnp.concatenate(
        [w_in[:, :Q_LORA + KV_LORA], _rope_tile_cols(w_in[:, Q_LORA + KV_LORA:])], axis=-1).astype(BF16)
    wq = w_qb.reshape(Q_LORA, H_A, QK_HEAD)
    wq_p = jnp.concatenate([wq[..., :QK_NOPE], _rope_tile_cols(wq[..., QK_NOPE:])], axis=-1)
    wq_p = wq_p.reshape(Q_LORA, H_A * QK_SLOT).astype(BF16)
    wkv = w_kvb.reshape(KV_LORA, H_A, QK_NOPE + V_HEAD)
    wkb = wkv[..., :QK_NOPE].reshape(KV_LORA, H_A * QK_NOPE).astype(BF16)
    wvb = wkv[..., QK_NOPE:].reshape(KV_LORA, H_A * V_HEAD).astype(BF16)
    qg = jnp.concatenate([q_norm[:QK_NOPE], _rope_tile_cols(q_norm[QK_NOPE:])]).reshape(1, QK_SLOT)
    kg = jnp.concatenate([k_norm[:QK_NOPE], _rope_tile_cols(k_norm[QK_NOPE:])]).reshape(1, QK_SLOT)

    def const(shape):
        return pl.BlockSpec(shape, lambda i: (0, 0))

    def rows(width):
        return pl.BlockSpec((tm, width), lambda i: (i, 0))

    n_in = Q_LORA + KV_LORA + LANES
    return pl.pallas_call(
        functools.partial(_mla_proj_kernel, scale=QK_HEAD ** -0.5 * LOG2E),
        out_shape=(jax.ShapeDtypeStruct((T, H_A * QK_SLOT), BF16),
                   jax.ShapeDtypeStruct((T, H_A * QK_SLOT), BF16),
                   jax.ShapeDtypeStruct((T, H_A * V_HEAD), BF16)),
        grid=(T // tm,),
        in_specs=[rows(D_MODEL), const((1, D_MODEL)), const((D_MODEL, n_in)), const((1, Q_LORA)),
                  const((1, KV_LORA)), const((Q_LORA, H_A * QK_SLOT)), const((KV_LORA, H_A * QK_NOPE)),
                  const((KV_LORA, H_A * V_HEAD)), const((1, QK_SLOT)), const((1, QK_SLOT)),
                  rows(LANES), rows(LANES)],
        out_specs=(rows(H_A * QK_SLOT), rows(H_A * QK_SLOT), rows(H_A * V_HEAD)),
        scratch_shapes=[pltpu.VMEM((tm, H_A * QK_SLOT), F32), pltpu.VMEM((tm, H_A * QK_NOPE), F32),
                        pltpu.VMEM((tm, LANES), F32)],
        compiler_params=_params(("parallel",)),
        name="mla_proj",
    )(x2d, gain.reshape(1, D_MODEL), w_in_p, qa_norm.reshape(1, Q_LORA), kva_norm.reshape(1, KV_LORA),
      wq_p, wkb, wvb, qg, kg, cos_a, sin_a)


FLASH_HEADS = 2


FLASH_ROWS = 64


def _flash_kernel(q_ref, k_ref, v_ref, o_ref, s_scr, p_scr, m_scr, l_scr, a_scr, acc_scr, *, tq, tk):
    i = pl.program_id(2)
    m_scr[...] = jnp.full(m_scr.shape, NEG_INF, F32)
    l_scr[...] = jnp.zeros(l_scr.shape, F32)
    acc_scr[...] = jnp.zeros(acc_scr.shape, F32)

    def scores(j, buf):
        off = pl.multiple_of(j * tk, tk)
        for h in range(FLASH_HEADS):
            s_scr[buf, h] = _dot_nt(q_ref[:, h * QK_SLOT:(h + 1) * QK_SLOT],
                                    k_ref[pl.ds(off, tk), h * QK_SLOT:(h + 1) * QK_SLOT])

    def softmax_pv(j, buf, masked):
        off = pl.multiple_of(j * tk, tk)
        for h in range(FLASH_HEADS):
            for rb in range(tq // FLASH_ROWS):
                rows = slice(rb * FLASH_ROWS, (rb + 1) * FLASH_ROWS)
                cw = min(tk, -(-((rb + 1) * FLASH_ROWS) // LANES) * LANES) if masked else tk
                s = s_scr[buf, h, rows, :cw]
                if masked:
                    row = lax.broadcasted_iota(jnp.int32, (FLASH_ROWS, cw), 0) + rb * FLASH_ROWS
                    col = lax.broadcasted_iota(jnp.int32, (FLASH_ROWS, cw), 1)
                    s = jnp.where(row >= col, s, NEG_INF)
                m_old = m_scr[h, rows, :]
                m_new = jnp.maximum(m_old, jnp.max(s, axis=-1, keepdims=True))
                p = jnp.exp2(s - jnp.tile(m_new, (1, cw // LANES)))
                alpha = jnp.exp2(m_old - m_new)
                l_scr[h, rows, :] = alpha * l_scr[h, rows, :] + jnp.sum(p, axis=-1, keepdims=True)
                m_scr[h, rows, :] = m_new
                a_scr[h, rows, :] = alpha
                p_scr[h, rows, :cw] = p.astype(BF16)
                if cw < tk:
                    p_scr[h, rows, cw:] = jnp.zeros((FLASH_ROWS, tk - cw), BF16)
            pv = _dot(p_scr[h], v_ref[pl.ds(off, tk), h * V_HEAD:(h + 1) * V_HEAD])
            acc_scr[h] = a_scr[h] * acc_scr[h] + pv

    def body(j, carry):
        scores(j, 0)
        softmax_pv(j, 0, False)
        return carry

    assert tq == tk
    lax.fori_loop(0, i, body, 0)
    scores(i, 0)
    softmax_pv(i, 0, True)
    for h in range(FLASH_HEADS):
        o_ref[:, h * V_HEAD:(h + 1) * V_HEAD] = (acc_scr[h] / l_scr[h]).astype(o_ref.dtype)


def _mla_attention(q, k, v, batch, seq):
    tq = tk = 512
    nq = seq // tq
    T = q.shape[0]
    hh = FLASH_HEADS
    stat = pltpu.VMEM((hh, tq, LANES), F32)
    return pl.pallas_call(
        functools.partial(_flash_kernel, tq=tq, tk=tk),
        out_shape=jax.ShapeDtypeStruct((T, H_A * V_HEAD), BF16),
        grid=(batch, H_A // hh, nq),
        in_specs=[pl.BlockSpec((tq, hh * QK_SLOT), lambda b, h, i: (b * nq + i, h)),
                  pl.BlockSpec((seq, hh * QK_SLOT), lambda b, h, i: (b, h)),
                  pl.BlockSpec((seq, hh * V_HEAD), lambda b, h, i: (b, h))],
        out_specs=pl.BlockSpec((tq, hh * V_HEAD), lambda b, h, i: (b * nq + i, h)),
        scratch_shapes=[pltpu.VMEM((1, hh, tq, tk), F32), pltpu.VMEM((hh, tq, tk), BF16), stat, stat, stat,
                        pltpu.VMEM((hh, tq, V_HEAD), F32)],
        compiler_params=_params(("parallel", "parallel", "arbitrary")),
        name="mla_flash",
    )(q, k, v)


def _out_proj_kernel(o_ref, w_ref, x_ref, y_ref):
    y_ref[...] = x_ref[...] + _dot(o_ref[...], w_ref[...])


def _out_proj(o, w_o, x2d):
    T, K = o.shape
    tm = 512
    return pl.pallas_call(
        _out_proj_kernel,
        out_shape=jax.ShapeDtypeStruct((T, D_MODEL), F32),
        grid=(T // tm,),
        in_specs=[pl.BlockSpec((tm, K), lambda i: (i, 0)),
                  pl.BlockSpec((K, D_MODEL), lambda i: (0, 0)),
                  pl.BlockSpec((tm, D_MODEL), lambda i: (i, 0))],
        out_specs=pl.BlockSpec((tm, D_MODEL), lambda i: (i, 0)),
        compiler_params=_params(("parallel",)),
        name="out_proj",
    )(o, w_o.astype(BF16), x2d)


def _mlp_kernel(x_ref, g_ref, w1_ref, w2_ref, y_ref, xn_ref, acc_ref):
    f = pl.program_id(1)

    @pl.when(f == 0)
    def _():
        xn_ref[...] = _rms(x_ref[...], g_ref[...]).astype(BF16)
        acc_ref[...] = jnp.zeros_like(acc_ref)

    h = jnp.maximum(_dot(xn_ref[...], w1_ref[...]), 0.0)
    acc_ref[...] += _dot((h * h).astype(BF16), w2_ref[...])

    @pl.when(f == pl.num_programs(1) - 1)
    def _():
        y_ref[...] = x_ref[...] + acc_ref[...]


def _mlp(x2d, gain, w1, w2):
    T = x2d.shape[0]
    tm, tf = 1024, 1024
    return pl.pallas_call(
        _mlp_kernel,
        out_shape=jax.ShapeDtypeStruct((T, D_MODEL), F32),
        grid=(T // tm, D_FF // tf),
        in_specs=[pl.BlockSpec((tm, D_MODEL), lambda i, f: (i, 0)),
                  pl.BlockSpec((1, D_MODEL), lambda i, f: (0, 0)),
                  pl.BlockSpec((D_MODEL, tf), lambda i, f: (0, f)),
                  pl.BlockSpec((tf, D_MODEL), lambda i, f: (f, 0))],
        out_specs=pl.BlockSpec((tm, D_MODEL), lambda i, f: (i, 0)),
        scratch_shapes=[pltpu.VMEM((tm, D_MODEL), BF16), pltpu.VMEM((tm, D_MODEL), F32)],
        compiler_params=_params(("parallel", "arbitrary")),
        name="mlp",
    )(x2d, gain.reshape(1, D_MODEL), w1.astype(BF16), w2.astype(BF16))


def _group_proj_kernel(*refs, has_v, scale):
    if has_v:
        x_ref, g_ref, w_ref, hg_ref, cos_ref, sin_ref = refs[:6]
        outs = refs[6:12]
        xs_ref, xn_ref, y_ref = refs[12:]
    else:
        x_ref, g_ref, w_ref, hg_ref, cos_ref, sin_ref = refs[:6]
        outs = refs[6:9]
        xs_ref, xn_ref, y_ref = refs[9:]
    tm = x_ref.shape[0]
    n_col = D_MODEL // LANES
    n_chunks = tm // PROJ_CHUNK
    assert n_chunks == SLABS
    xn = _rms(x_ref[...], g_ref[...])
    for c in range(n_col):
        xs_ref[c] = xn[:, c * LANES:(c + 1) * LANES]

    def permute(gi, buf):
        d = DILATED_GROUPS[gi][1]
        rows = tm // d
        if d == 1:
            xn_ref[buf] = xn.astype(BF16)
        else:
            for r in range(d):
                for c in range(n_col):
                    xn_ref[buf, r * rows:(r + 1) * rows, c * LANES:(c + 1) * LANES] = (
                        xs_ref[c, pl.ds(r, rows, stride=d), :].astype(BF16))

    def norm_chunk(gi, buf, ci):
        d = DILATED_GROUPS[gi][1]
        rows = tm // d
        out_ref = outs[gi]
        r0 = pl.multiple_of(ci * PROJ_CHUNK, PROJ_CHUNK)

        def table_rows(t_ref):
            if d == 1:
                return t_ref[pl.ds(r0, PROJ_CHUNK), :]
            if rows >= PROJ_CHUNK:
                per = rows // PROJ_CHUNK
                return t_ref[pl.ds((ci % per) * PROJ_CHUNK * d + ci // per, PROJ_CHUNK, stride=d), :]
            per = PROJ_CHUNK // rows
            return jnp.concatenate([t_ref[pl.ds(ci * per + s, rows, stride=d), :] for s in range(per)], axis=0)

        cos = table_rows(cos_ref)
        sin = table_rows(sin_ref)
        for h in range(H_B):
            hs = slice(h * HEAD_DIM_B, (h + 1) * HEAD_DIM_B)
            sl = h * HEAD_DIM_B // SLAB_W
            lo = h * HEAD_DIM_B % SLAB_W
            yh = y_ref[buf, sl, pl.ds(r0, PROJ_CHUNK), lo:lo + HEAD_DIM_B]
            rs = lax.rsqrt(jnp.mean(yh * yh, axis=-1, keepdims=True) + NORM_EPS) * scale
            yg = yh * hg_ref[:, gi * C_B + h * HEAD_DIM_B:gi * C_B + (h + 1) * HEAD_DIM_B]
            res = ((yg * cos + _rot_half(yg) * sin) * rs).astype(BF16)
            if rows >= PROJ_CHUNK:
                per = rows // PROJ_CHUNK
                out_ref[0, ci // per, pl.ds(pl.multiple_of((ci % per) * PROJ_CHUNK, PROJ_CHUNK), PROJ_CHUNK),
                        hs] = res
            else:
                per = PROJ_CHUNK // rows
                for s in range(per):
                    out_ref[0, ci * per + s, :, hs] = res[s * rows:(s + 1) * rows]

    def v_slab(gi, buf, ci):
        d = DILATED_GROUPS[gi][1]
        rows = tm // d
        yv = _dot(xn_ref[buf], w_ref[(N_GROUPS + gi) * SLABS + ci])
        for r in range(d):
            outs[N_GROUPS + gi][0, r, ci] = yv[r * rows:(r + 1) * rows].astype(BF16)

    permute(0, 0)
    for s in range(SLABS):
        y_ref[0, s] = _dot(xn_ref[0], w_ref[s])
    for gi in range(N_GROUPS):
        buf = gi % 2
        if gi + 1 < N_GROUPS:
            permute(gi + 1, 1 - buf)

        def body(ci, carry, gi=gi, buf=buf):
            norm_chunk(gi, buf, ci)
            if gi + 1 < N_GROUPS:
                y_ref[1 - buf, ci] = _dot(xn_ref[1 - buf], w_ref[(gi + 1) * SLABS + ci])
            if has_v:
                v_slab(gi, buf, ci)
            return carry

        lax.fori_loop(0, n_chunks, body, 0)


def _group_proj(x2d, gain, w, head_gain, has_v, scale, cos_b, sin_b, batch, seq):
    tm = PROJ_TILE
    nt = seq // tm
    n_rope = N_GROUPS * C_B
    n_slabs = w.shape[1] // SLAB_W
    w_slabs = w.astype(BF16).reshape(D_MODEL, n_slabs, SLAB_W).transpose(1, 0, 2)
    in_specs = [pl.BlockSpec((tm, D_MODEL), lambda i: (i, 0)),
                pl.BlockSpec((1, D_MODEL), lambda i: (0, 0)),
                pl.BlockSpec((n_slabs, D_MODEL, SLAB_W), lambda i: (0, 0, 0), pipeline_mode=pl.Buffered(1)),
                pl.BlockSpec((1, n_rope), lambda i: (0, 0)),
                pl.BlockSpec((tm, LANES), lambda i: (i, 0)),
                pl.BlockSpec((tm, LANES), lambda i: (i, 0))]
    shapes, specs = [], []
    for _, d in DILATED_GROUPS:
        shapes.append(jax.ShapeDtypeStruct((batch, d, seq // d, C_B), BF16))
        specs.append(pl.BlockSpec((1, d, tm // d, C_B), lambda i: (i // nt, 0, i % nt, 0)))
    if has_v:
        for _, d in DILATED_GROUPS:
            shapes.append(jax.ShapeDtypeStruct((batch, d, SLABS, seq // d, SLAB_W), BF16))
            specs.append(pl.BlockSpec((1, d, SLABS, tm // d, SLAB_W), lambda i: (i // nt, 0, 0, i % nt, 0)))
    return pl.pallas_call(
        functools.partial(_group_proj_kernel, has_v=has_v, scale=scale),
        out_shape=tuple(shapes),
        grid=(batch * nt,),
        in_specs=in_specs,
        out_specs=tuple(specs),
        scratch_shapes=[pltpu.VMEM((D_MODEL // LANES, tm, LANES), F32), pltpu.VMEM((2, tm, D_MODEL), BF16),
                        pltpu.VMEM((2, SLABS, tm, SLAB_W), F32)],
        compiler_params=_params(("parallel",)),
        name="group_proj_kv" if has_v else "group_proj_q",
    )(x2d, gain.reshape(1, D_MODEL), w_slabs, head_gain.reshape(1, n_rope), cos_b, sin_b)


BAND_HEADS = 4


def _band_kernel(q_ref, k_ref, v_ref, o_ref, lse_ref, *, dilation, length):
    nk = min(2 * BAND, length)
    row = lax.broadcasted_iota(jnp.int32, (BAND, nk), 0)
    col = lax.broadcasted_iota(jnp.int32, (BAND, nk), 1)
    lane_head = lax.broadcasted_iota(jnp.int32, (BAND, LANES), 1) // (LANES // BAND_HEADS)

    def block(r, i):
        q0 = i * BAND
        start = jnp.maximum(q0 + BAND - nk, 0)
        dist = (q0 - start) + row - col
        valid = jnp.logical_and(dist >= 0, dist <= BAND)
        if not isinstance(q0, int):
            q0 = pl.multiple_of(q0, BAND)
            start = pl.multiple_of(start, BAND)
        scores = []
        for h in range(BAND_HEADS):
            hs = slice(h * HEAD_DIM_B, (h + 1) * HEAD_DIM_B)
            s = _dot_nt(q_ref[0, r, pl.ds(q0, BAND), hs], k_ref[0, r, pl.ds(start, nk), hs])
            scores.append(jnp.where(valid, s, NEG_INF))
        probs, denoms, lse_tile = [], [], jnp.zeros((BAND, LANES), F32)
        for h in range(BAND_HEADS):
            m = jnp.max(scores[h], axis=-1, keepdims=True)
            p = jnp.exp2(scores[h] - m)
            denom = jnp.sum(p, axis=-1, keepdims=True)
            probs.append(p.astype(BF16))
            denoms.append(denom)
            lse_tile = jnp.where(lane_head == h, m + jnp.log2(denom), lse_tile)
        tok = pl.ds(q0 * dilation + r, BAND, stride=dilation) if dilation > 1 else pl.ds(q0, BAND)
        for h in range(BAND_HEADS):
            hs = slice(h * HEAD_DIM_B, (h + 1) * HEAD_DIM_B)
            lo = h * HEAD_DIM_B % SLAB_W
            out = _dot(probs[h], v_ref[0, r, h * HEAD_DIM_B // SLAB_W, pl.ds(start, nk), lo:lo + HEAD_DIM_B])
            o_ref[0, h, tok, :] = out / denoms[h]
        lse_ref[0, 0, tok, :] = lse_tile

    nb = length // BAND
    for r in range(dilation):
        if nb == 1:
            block(r, 0)
        else:
            def body(i, carry, r=r):
                block(r, i)
                return carry
            lax.fori_loop(0, nb, body, 0, unroll=4)


def _band_attention(q, k, v, dilation, batch, seq):
    length = seq // dilation
    halves = H_B // BAND_HEADS
    cw = BAND_HEADS * HEAD_DIM_B
    blk = pl.BlockSpec((1, dilation, length, cw), lambda b, hh: (b, 0, 0, hh))
    v_blk = pl.BlockSpec((1, dilation, cw // SLAB_W, length, SLAB_W), lambda b, hh: (b, 0, hh, 0, 0))
    return pl.pallas_call(
        functools.partial(_band_kernel, dilation=dilation, length=length),
        out_shape=(jax.ShapeDtypeStruct((batch, H_B, seq, HEAD_DIM_B), F32),
                   jax.ShapeDtypeStruct((batch, halves, seq, LANES), F32)),
        grid=(batch, halves),
        in_specs=[blk, blk, v_blk],
        out_specs=(pl.BlockSpec((1, BAND_HEADS, seq, HEAD_DIM_B), lambda b, hh: (b, hh, 0, 0)),
                   pl.BlockSpec((1, 1, seq, LANES), lambda b, hh: (b, hh, 0, 0))),
        compiler_params=_params(("parallel", "parallel")),
        name=f"band_attn_d{dilation}",
    )(q, k, v)


def _combine_proj_kernel(o0_ref, o1_ref, o2_ref, l0_ref, l1_ref, l2_ref, w_ref, x_ref, y_ref, o_scr):
    lanes_per_head = LANES // BAND_HEADS
    for half in range(H_B // BAND_HEADS):
        l0, l1, l2 = l0_ref[0, half], l1_ref[0, half], l2_ref[0, half]
        mx = jnp.maximum(jnp.maximum(l0, l1), l2)
        e0, e1, e2 = jnp.exp2(l0 - mx), jnp.exp2(l1 - mx), jnp.exp2(l2 - mx)
        inv = 1.0 / (e0 + e1 + e2)
        w0, w1, w2 = e0 * inv, e1 * inv, e2 * inv
        for hq in range(BAND_HEADS):
            h = half * BAND_HEADS + hq
            c = hq * lanes_per_head
            o = (w0[:, c:c + 1] * o0_ref[0, h] + w1[:, c:c + 1] * o1_ref[0, h] + w2[:, c:c + 1] * o2_ref[0, h])
            o_scr[:, h * HEAD_DIM_B:(h + 1) * HEAD_DIM_B] = o.astype(BF16)
    y_ref[...] = x_ref[...] + _dot(o_scr[...], w_ref[...])


def _combine_proj(outs, lses, w_o, x2d, seq):
    T = x2d.shape[0]
    tm = 512
    nt = seq // tm
    halves = H_B // BAND_HEADS
    o_spec = pl.BlockSpec((1, H_B, tm, HEAD_DIM_B), lambda i: (i // nt, 0, i % nt, 0))
    l_spec = pl.BlockSpec((1, halves, tm, LANES), lambda i: (i // nt, 0, i % nt, 0))
    return pl.pallas_call(
        _combine_proj_kernel,
        out_shape=jax.ShapeDtypeStruct((T, D_MODEL), F32),
        grid=(T // tm,),
        in_specs=[o_spec, o_spec, o_spec, l_spec, l_spec, l_spec,
                  pl.BlockSpec((C_B, D_MODEL), lambda i: (0, 0)),
                  pl.BlockSpec((tm, D_MODEL), lambda i: (i, 0))],
        out_specs=pl.BlockSpec((tm, D_MODEL), lambda i: (i, 0)),
        scratch_shapes=[pltpu.VMEM((tm, C_B), BF16)],
        compiler_params=_params(("parallel",)),
        name="combine_proj",
    )(*outs, *lses, w_o.astype(BF16), x2d)


def kernel(x, positions, attn_norm, mlp_norm, mla_w_in, mla_qa_norm, mla_kva_norm, mla_w_qb, mla_w_kvb,
           mla_q_norm, mla_k_norm, mla_w_o, kv_norm, w_kv, k_norm_b, w_q_b, q_norm_b, w_o_b, mlp_w1, mlp_w2):
    B, S, D = x.shape
    T = B * S
    cos_a, sin_a, cos_b, sin_b = _rope_tables(positions)
    h = x.reshape(T, D)

    def head_gains(gn):
        return jnp.broadcast_to(gn[:, None, :], (N_GROUPS, H_B, HEAD_DIM_B))

    for a in range(N_A_LAYERS):
        q, k, v = _mla_proj(h, attn_norm[a], mla_w_in[a], mla_qa_norm[a], mla_kva_norm[a], mla_w_qb[a],
                            mla_w_kvb[a], mla_q_norm[a], mla_k_norm[a], cos_a, sin_a)
        o = _mla_attention(q, k, v, B, S)
        h = _out_proj(o, mla_w_o[a], h)
        h = _mlp(h, mlp_norm[a], mlp_w1[a], mlp_w2[a])

    kv = _group_proj(h, kv_norm, w_kv, head_gains(k_norm_b), True, 1.0, cos_b, sin_b, B, S)
    ks, vs = kv[:N_GROUPS], kv[N_GROUPS:]

    for b in range(N_B_LAYERS):
        layer = N_A_LAYERS + b
        qs = _group_proj(h, attn_norm[layer], w_q_b[b], head_gains(q_norm_b[b]), False,
                         HEAD_DIM_B ** -0.5 * LOG2E, cos_b, sin_b, B, S)
        outs, lses = [], []
        for g, (window, dilation) in enumerate(DILATED_GROUPS):
            assert window // dilation == BAND
            o, lse = _band_attention(qs[g], ks[g], vs[g], dilation, B, S)
            outs.append(o)
            lses.append(lse)
        h = _combine_proj(outs, lses, w_o_b[b], h, S)
        h = _mlp(h, mlp_norm[layer], mlp_w1[layer], mlp_w2[layer])

    return h.reshape(B, S, D)
```

```python
import functools

import jax
import jax.numpy as jnp
from jax import lax
from jax.experimental import pallas as pl
from jax.experimental.pallas import tpu as pltpu

D_MODEL = 1024
N_A_LAYERS = 2
N_B_LAYERS = 2
H_A = 16
QK_NOPE = 128
QK_ROPE = 64
QK_HEAD = QK_NOPE + QK_ROPE
V_HEAD = 128
Q_LORA = 256
KV_LORA = 128
DILATED_GROUPS = ((128, 1), (512, 4), (2048, 16))
N_GROUPS = 3
H_B = 8
HEAD_DIM_B = 128
C_B = H_B * HEAD_DIM_B
D_FF = 4 * D_MODEL
ROPE_THETA = 10000.0
NORM_EPS = 1e-6
NEG_INF = -1e30
LOG2E = 1.4426950408889634

LANES = 128
QK_SLOT = 2 * LANES
VMEM_LIMIT = 56 * 1024 * 1024
BAND = 128
PROJ_TILE = 512
PROJ_CHUNK = 128
SLAB_W = 2 * LANES
SLABS = C_B // SLAB_W

BF16 = jnp.bfloat16
F32 = jnp.float32


def _params(semantics):
    return pltpu.CompilerParams(dimension_semantics=semantics, vmem_limit_bytes=VMEM_LIMIT)


def _rms(x, gain):
    ms = jnp.mean(x * x, axis=-1, keepdims=True)
    return x * lax.rsqrt(ms + NORM_EPS) * gain


def _rot_half(u):
    return pltpu.roll(u, LANES // 2, axis=1)


def _dot(a, b):
    return jnp.dot(a, b, preferred_element_type=F32)


def _dot_nt(a, b):
    return lax.dot_general(a, b, (((1,), (1,)), ((), ())), preferred_element_type=F32)


def _tables_kernel(pos_ref, f_ref, cm_ref, sm_ref, cos_ref, sin_ref):
    ang = pos_ref[...] * f_ref[...]
    cos_ref[...] = jnp.cos(ang) * cm_ref[...]
    sin_ref[...] = jnp.sin(ang) * sm_ref[...]


def _rope_table(pos, freq, cos_mask, sin_sign):
    n = pos.shape[0]
    tm = 1024
    row = pl.BlockSpec((1, LANES), lambda i: (0, 0))
    tab = pl.BlockSpec((tm, LANES), lambda i: (i, 0))
    shp = jax.ShapeDtypeStruct((n, LANES), F32)
    return pl.pallas_call(
        _tables_kernel,
        out_shape=(shp, shp),
        grid=(n // tm,),
        in_specs=[pl.BlockSpec((tm, 1), lambda i: (i, 0)), row, row, row],
        out_specs=(tab, tab),
        compiler_params=_params(("parallel",)),
        name="rope_table",
    )(pos.reshape(n, 1), freq.reshape(1, LANES), cos_mask.reshape(1, LANES), sin_sign.reshape(1, LANES))


def _rope_tables(positions):
    B, S = positions.shape
    pos = positions.astype(F32)
    inv_a = ROPE_THETA ** (-jnp.arange(0, QK_ROPE, 2, dtype=F32) / QK_ROPE)
    inv_b = ROPE_THETA ** (-jnp.arange(0, HEAD_DIM_B, 2, dtype=F32) / HEAD_DIM_B)
    z32 = jnp.zeros((32,), F32)
    o32 = jnp.ones((32,), F32)
    o64 = jnp.ones((64,), F32)
    cos_a, sin_a = _rope_table(pos.reshape(B * S), jnp.concatenate([inv_a, z32, inv_a, z32]),
                               jnp.concatenate([o32, z32, o32, z32]), jnp.concatenate([-o32, z32, o32, z32]))
    cos_b, sin_b = _rope_table(pos.reshape(B * S), jnp.concatenate([inv_b, inv_b]),
                               jnp.concatenate([o64, o64]), jnp.concatenate([-o64, o64]))
    return cos_a, sin_a, cos_b, sin_b


def _mla_proj_kernel(x_ref, g_ref, win_ref, qa_ref, kva_ref, wqb_ref, wkb_ref, wvb_ref, qg_ref, kg_ref,
                     cos_ref, sin_ref, q_ref, k_ref, v_ref, cq_scr, ckv_scr, kpe_scr, kss_scr, q_scr, kn_scr,
                     *, scale):
    tm = x_ref.shape[0]
    n_pairs = H_A // 2
    xn = _rms(x_ref[...], g_ref[...]).astype(BF16)
    lat = _dot(xn, win_ref[...])
    cq_scr[...] = _rms(lat[:, :Q_LORA], qa_ref[...]).astype(BF16)
    ckv_scr[...] = _rms(lat[:, Q_LORA:Q_LORA + KV_LORA], kva_ref[...]).astype(BF16)
    k_pe = lat[:, Q_LORA + KV_LORA:]
    kss_scr[...] = jnp.broadcast_to(jnp.sum(k_pe * k_pe, axis=-1, keepdims=True), (tm, LANES))
    kpe_g = k_pe * kg_ref[:, LANES:]
    kpe_scr[...] = kpe_g * cos_ref[...] + _rot_half(kpe_g) * sin_ref[...]

    def matmuls(pair, buf):
        q_scr[buf] = _dot(cq_scr[...], wqb_ref[pair])
        kn_scr[buf] = _dot(ckv_scr[...], wkb_ref[pair])

    def finish(pair, buf):
        v_ref[pair] = _dot(ckv_scr[...], wvb_ref[pair]).astype(BF16)
        qg_n, qg_pe = qg_ref[:, :LANES], qg_ref[:, LANES:]
        kg_n = kg_ref[:, :LANES]
        for c in range(tm // PROJ_CHUNK):
            rows = slice(c * PROJ_CHUNK, (c + 1) * PROJ_CHUNK)
            cos = cos_ref[rows, :]
            sin = sin_ref[rows, :]
            kpe_ss = kss_scr[rows, :]
            kpe_rot = kpe_scr[rows, :]
            for hh in range(2):
                qn = q_scr[buf, rows, hh * QK_SLOT:hh * QK_SLOT + LANES]
                qp = q_scr[buf, rows, hh * QK_SLOT + LANES:(hh + 1) * QK_SLOT]
                ss = jnp.sum(qn * qn + qp * qp, axis=-1, keepdims=True)
                rs = lax.rsqrt(ss * (1.0 / QK_HEAD) + NORM_EPS) * scale
                qpg = qp * qg_pe
                q_ref[pair, rows, hh * QK_SLOT:hh * QK_SLOT + LANES] = (qn * rs * qg_n).astype(BF16)
                q_ref[pair, rows, hh * QK_SLOT + LANES:(hh + 1) * QK_SLOT] = (
                    (qpg * cos + _rot_half(qpg) * sin) * rs).astype(BF16)
                kn = kn_scr[buf, rows, hh * LANES:(hh + 1) * LANES]
                ssk = jnp.sum(kn * kn, axis=-1, keepdims=True) + kpe_ss
                rsk = lax.rsqrt(ssk * (1.0 / QK_HEAD) + NORM_EPS)
                k_ref[pair, rows, hh * QK_SLOT:hh * QK_SLOT + LANES] = (kn * rsk * kg_n).astype(BF16)
                k_ref[pair, rows, hh * QK_SLOT + LANES:(hh + 1) * QK_SLOT] = (kpe_rot * rsk).astype(BF16)

    matmuls(0, 0)

    def body(jj, carry):
        pair = 2 * jj
        matmuls(pair + 1, 1)
        finish(pair, 0)
        matmuls(jnp.minimum(pair + 2, n_pairs - 1), 0)
        finish(pair + 1, 1)
        return carry

    lax.fori_loop(0, n_pairs // 2, body, 0)


def _rope_tile_cols(a):
    z = jnp.zeros(a.shape[:-1] + (32,), a.dtype)
    return jnp.concatenate([a[..., :32], z, a[..., 32:], z], axis=-1)


def _mla_proj(x2d, gain, w_in, qa_norm, kva_norm, w_qb, w_kvb, q_norm, k_norm, cos_a, sin_a):
    T = x2d.shape[0]
    tm = 256
    w_in_p = jnp.concatenate(
        [w_in[:, :Q_LORA + KV_LORA], _rope_tile_cols(w_in[:, Q_LORA + KV_LORA:])], axis=-1).astype(BF16)
    wq = w_qb.reshape(Q_LORA, H_A, QK_HEAD)
    wq_p = jnp.concatenate([wq[..., :QK_NOPE], _rope_tile_cols(wq[..., QK_NOPE:])], axis=-1)
    n_pairs = H_A // 2

    def pair_slabs(w2d):
        k_dim = w2d.shape[0]
        return w2d.reshape(k_dim, n_pairs, -1).transpose(1, 0, 2).astype(BF16)

    wq_p = pair_slabs(wq_p.reshape(Q_LORA, H_A * QK_SLOT))
    wkv = w_kvb.reshape(KV_LORA, H_A, QK_NOPE + V_HEAD)
    wkb = pair_slabs(wkv[..., :QK_NOPE].reshape(KV_LORA, H_A * QK_NOPE))
    wvb = pair_slabs(wkv[..., QK_NOPE:].reshape(KV_LORA, H_A * V_HEAD))
    qg = jnp.concatenate([q_norm[:QK_NOPE], _rope_tile_cols(q_norm[QK_NOPE:])]).reshape(1, QK_SLOT)
    kg = jnp.concatenate([k_norm[:QK_NOPE], _rope_tile_cols(k_norm[QK_NOPE:])]).reshape(1, QK_SLOT)

    def const(shape):
        return pl.BlockSpec(shape, lambda i: (0,) * len(shape))

    def rows(width):
        return pl.BlockSpec((tm, width), lambda i: (i, 0))

    def pair_rows(width):
        return pl.BlockSpec((n_pairs, tm, width), lambda i: (0, i, 0))

    n_in = Q_LORA + KV_LORA + LANES
    return pl.pallas_call(
        functools.partial(_mla_proj_kernel, scale=QK_HEAD ** -0.5 * LOG2E),
        out_shape=(jax.ShapeDtypeStruct((n_pairs, T, 2 * QK_SLOT), BF16),
                   jax.ShapeDtypeStruct((n_pairs, T, 2 * QK_SLOT), BF16),
                   jax.ShapeDtypeStruct((n_pairs, T, 2 * V_HEAD), BF16)),
        grid=(T // tm,),
        in_specs=[rows(D_MODEL), const((1, D_MODEL)), const((D_MODEL, n_in)), const((1, Q_LORA)),
                  const((1, KV_LORA)), const((n_pairs, Q_LORA, 2 * QK_SLOT)), const((n_pairs, KV_LORA, 2 * QK_NOPE)),
                  const((n_pairs, KV_LORA, 2 * V_HEAD)), const((1, QK_SLOT)), const((1, QK_SLOT)),
                  rows(LANES), rows(LANES)],
        out_specs=(pair_rows(2 * QK_SLOT), pair_rows(2 * QK_SLOT), pair_rows(2 * V_HEAD)),
        scratch_shapes=[pltpu.VMEM((tm, Q_LORA), BF16), pltpu.VMEM((tm, KV_LORA), BF16),
                        pltpu.VMEM((tm, LANES), F32), pltpu.VMEM((tm, LANES), F32),
                        pltpu.VMEM((2, tm, 2 * QK_SLOT), F32), pltpu.VMEM((2, tm, 2 * QK_NOPE), F32)],
        compiler_params=_params(("parallel",)),
        name="mla_proj",
    )(x2d, gain.reshape(1, D_MODEL), w_in_p, qa_norm.reshape(1, Q_LORA), kva_norm.reshape(1, KV_LORA),
      wq_p, wkb, wvb, qg, kg, cos_a, sin_a)


FLASH_HEADS = 2


FLASH_ROWS = 64


def _flash_kernel(q_ref, k_ref, v_ref, o_ref, s_scr, p_scr, m_scr, l_scr, a_scr, acc_scr, *, tq, tk):
    i = pl.program_id(2)
    m_scr[...] = jnp.full(m_scr.shape, NEG_INF, F32)
    l_scr[...] = jnp.zeros(l_scr.shape, F32)
    acc_scr[...] = jnp.zeros(acc_scr.shape, F32)

    def scores(j, buf):
        off = pl.multiple_of(j * tk, tk)
        for h in range(FLASH_HEADS):
            s_scr[buf, h] = _dot_nt(q_ref[0, :, h * QK_SLOT:(h + 1) * QK_SLOT],
                                    k_ref[0, pl.ds(off, tk), h * QK_SLOT:(h + 1) * QK_SLOT])

    def softmax_pv(j, buf, masked):
        off = pl.multiple_of(j * tk, tk)
        for h in range(FLASH_HEADS):
            for rb in range(tq // FLASH_ROWS):
                rows = slice(rb * FLASH_ROWS, (rb + 1) * FLASH_ROWS)
                cw = min(tk, -(-((rb + 1) * FLASH_ROWS) // LANES) * LANES) if masked else tk
                s = s_scr[buf, h, rows, :cw]
                if masked:
                    row = lax.broadcasted_iota(jnp.int32, (FLASH_ROWS, cw), 0) + rb * FLASH_ROWS
                    col = lax.broadcasted_iota(jnp.int32, (FLASH_ROWS, cw), 1)
                    s = jnp.where(row >= col, s, NEG_INF)
                m_old = m_scr[h, rows, :]
                m_new = jnp.maximum(m_old, jnp.max(s, axis=-1, keepdims=True))
                p = jnp.exp2(s - jnp.tile(m_new, (1, cw // LANES)))
                alpha = jnp.exp2(m_old - m_new)
                l_scr[h, rows, :] = alpha * l_scr[h, rows, :] + jnp.sum(p, axis=-1, keepdims=True)
                m_scr[h, rows, :] = m_new
                a_scr[h, rows, :] = alpha
                p_scr[h, rows, :cw] = p.astype(BF16)
                if cw < tk:
                    p_scr[h, rows, cw:] = jnp.zeros((FLASH_ROWS, tk - cw), BF16)
            pv = _dot(p_scr[h], v_ref[0, pl.ds(off, tk), h * V_HEAD:(h + 1) * V_HEAD])
            acc_scr[h] = a_scr[h] * acc_scr[h] + pv

    def body(j, carry):
        scores(j, 0)
        softmax_pv(j, 0, False)
        return carry

    assert tq == tk
    lax.fori_loop(0, i, body, 0)
    scores(i, 0)
    softmax_pv(i, 0, True)
    for h in range(FLASH_HEADS):
        o_ref[:, h * V_HEAD:(h + 1) * V_HEAD] = (acc_scr[h] / l_scr[h]).astype(o_ref.dtype)


def _mla_attention(q, k, v, batch, seq):
    tq = tk = 512
    nq = seq // tq
    T = q.shape[1]
    hh = FLASH_HEADS
    assert q.shape[0] * hh == H_A
    stat =pltpu.VMEM((hh, tq, LANES), F32)
    return pl.pallas_call(
        functools.partial(_flash_kernel, tq=tq, tk=tk),
        out_shape=jax.ShapeDtypeStruct((T, H_A * V_HEAD), BF16),
        grid=(batch, H_A // hh, nq),
        in_specs=[pl.BlockSpec((1, tq, hh * QK_SLOT), lambda b, h, i: (h, b * nq + i, 0)),
                  pl.BlockSpec((1, seq, hh * QK_SLOT), lambda b, h, i: (h, b, 0)),
                  pl.BlockSpec((1, seq, hh * V_HEAD), lambda b, h, i: (h, b, 0))],
        out_specs=pl.BlockSpec((tq, hh * V_HEAD), lambda b, h, i: (b * nq + i, h)),
        scratch_shapes=[pltpu.VMEM((1, hh, tq, tk), F32), pltpu.VMEM((hh, tq, tk), BF16), stat, stat, stat,
                        pltpu.VMEM((hh, tq, V_HEAD), F32)],
        compiler_params=_params(("parallel", "parallel", "arbitrary")),
        name="mla_flash",
    )(q, k, v)


def _out_proj_kernel(o_ref, w_ref, x_ref, y_ref):
    y_ref[...] = x_ref[...] + _dot(o_ref[...], w_ref[...])


def _out_proj(o, w_o, x2d):
    T, K = o.shape
    tm = 512
    return pl.pallas_call(
        _out_proj_kernel,
        out_shape=jax.ShapeDtypeStruct((T, D_MODEL), F32),
        grid=(T // tm,),
        in_specs=[pl.BlockSpec((tm, K), lambda i: (i, 0)),
                  pl.BlockSpec((K, D_MODEL), lambda i: (0, 0)),
                  pl.BlockSpec((tm, D_MODEL), lambda i: (i, 0))],
        out_specs=pl.BlockSpec((tm, D_MODEL), lambda i: (i, 0)),
        compiler_params=_params(("parallel",)),
        name="out_proj",
    )(o, w_o.astype(BF16), x2d)


def _mlp_kernel(x_ref, g_ref, w1_ref, w2_ref, y_ref, xn_ref, acc_ref):
    f = pl.program_id(1)

    @pl.when(f == 0)
    def _():
        xn_ref[...] = _rms(x_ref[...], g_ref[...]).astype(BF16)
        acc_ref[...] = jnp.zeros_like(acc_ref)

    h = jnp.maximum(_dot(xn_ref[...], w1_ref[...]), 0.0)
    acc_ref[...] += _dot((h * h).astype(BF16), w2_ref[...])

    @pl.when(f == pl.num_programs(1) - 1)
    def _():
        y_ref[...] = x_ref[...] + acc_ref[...]


def _mlp(x2d, gain, w1, w2):
    T = x2d.shape[0]
    tm, tf = 1024, 1024
    return pl.pallas_call(
        _mlp_kernel,
        out_shape=jax.ShapeDtypeStruct((T, D_MODEL), F32),
        grid=(T // tm, D_FF // tf),
        in_specs=[pl.BlockSpec((tm, D_MODEL), lambda i, f: (i, 0)),
                  pl.BlockSpec((1, D_MODEL), lambda i, f: (0, 0)),
                  pl.BlockSpec((D_MODEL, tf), lambda i, f: (0, f)),
                  pl.BlockSpec((tf, D_MODEL), lambda i, f: (f, 0))],
        out_specs=pl.BlockSpec((tm, D_MODEL), lambda i, f: (i, 0)),
        scratch_shapes=[pltpu.VMEM((tm, D_MODEL), BF16), pltpu.VMEM((tm, D_MODEL), F32)],
        compiler_params=_params(("parallel", "arbitrary")),
        name="mlp",
    )(x2d, gain.reshape(1, D_MODEL), w1.astype(BF16), w2.astype(BF16))


def _group_proj_kernel(*refs, has_v, scale):
    if has_v:
        x_ref, g_ref, w_ref, hg_ref, cos_ref, sin_ref = refs[:6]
        outs = refs[6:12]
        xs_ref, xn_ref, y_ref = refs[12:]
    else:
        x_ref, g_ref, w_ref, hg_ref, cos_ref, sin_ref = refs[:6]
        outs = refs[6:9]
        xs_ref, xn_ref, y_ref = refs[9:]
    tm = x_ref.shape[0]
    n_col = D_MODEL // LANES
    n_chunks = tm // PROJ_CHUNK
    assert n_chunks == SLABS
    xn = _rms(x_ref[...], g_ref[...])
    for c in range(n_col):
        xs_ref[c] = xn[:, c * LANES:(c + 1) * LANES]

    def permute(gi, buf):
        d = DILATED_GROUPS[gi][1]
        rows = tm // d
        if d == 1:
            xn_ref[buf] = xn.astype(BF16)
        else:
            for r in range(d):
                for c in range(n_col):
                    xn_ref[buf, r * rows:(r + 1) * rows, c * LANES:(c + 1) * LANES] = (
                        xs_ref[c, pl.ds(r, rows, stride=d), :].astype(BF16))

    def norm_chunk(gi, buf, ci):
        d = DILATED_GROUPS[gi][1]
        rows = tm // d
        out_ref = outs[gi]
        r0 = pl.multiple_of(ci * PROJ_CHUNK, PROJ_CHUNK)

        def table_rows(t_ref):
            if d == 1:
                return t_ref[pl.ds(r0, PROJ_CHUNK), :]
            if rows >= PROJ_CHUNK:
                per = rows // PROJ_CHUNK
                return t_ref[pl.ds((ci % per) * PROJ_CHUNK * d + ci // per, PROJ_CHUNK, stride=d), :]
            per = PROJ_CHUNK // rows
            return jnp.concatenate([t_ref[pl.ds(ci * per + s, rows, stride=d), :] for s in range(per)], axis=0)

        cos = table_rows(cos_ref)
        sin = table_rows(sin_ref)
        for h in range(H_B):
            hs = slice(h * HEAD_DIM_B, (h + 1) * HEAD_DIM_B)
            sl = h * HEAD_DIM_B // SLAB_W
            lo = h * HEAD_DIM_B % SLAB_W
            yh = y_ref[buf, sl, pl.ds(r0, PROJ_CHUNK), lo:lo + HEAD_DIM_B]
            rs = lax.rsqrt(jnp.mean(yh * yh, axis=-1, keepdims=True) + NORM_EPS) * scale
            yg = yh * hg_ref[:, gi * C_B + h * HEAD_DIM_B:gi * C_B + (h + 1) * HEAD_DIM_B]
            res = ((yg * cos + _rot_half(yg) * sin) * rs).astype(BF16)
            if rows >= PROJ_CHUNK:
                per = rows // PROJ_CHUNK
                out_ref[0, ci // per, pl.ds(pl.multiple_of((ci % per) * PROJ_CHUNK, PROJ_CHUNK), PROJ_CHUNK),
                        hs] = res
            else:
                per = PROJ_CHUNK // rows
                for s in range(per):
                    out_ref[0, ci * per + s, :, hs] = res[s * rows:(s + 1) * rows]

    def v_slab(gi, buf, ci):
        d = DILATED_GROUPS[gi][1]
        rows = tm // d
        yv = _dot(xn_ref[buf], w_ref[(N_GROUPS + gi) * SLABS + ci])
        for r in range(d):
            outs[N_GROUPS + gi][0, r, ci] = yv[r * rows:(r + 1) * rows].astype(BF16)

    permute(0, 0)
    for s in range(SLABS):
        y_ref[0, s] = _dot(xn_ref[0], w_ref[s])
    for gi in range(N_GROUPS):
        buf = gi % 2
        if gi + 1 < N_GROUPS:
            permute(gi + 1, 1 - buf)

        def body(ci, carry, gi=gi, buf=buf):
            norm_chunk(gi, buf, ci)
            if gi + 1 < N_GROUPS:
                y_ref[1 - buf, ci] = _dot(xn_ref[1 - buf], w_ref[(gi + 1) * SLABS + ci])
            if has_v:
                v_slab(gi, buf, ci)
            return carry

        lax.fori_loop(0, n_chunks, body, 0)


def _group_proj(x2d, gain, w, head_gain, has_v, scale, cos_b, sin_b, batch, seq):
    tm = PROJ_TILE
    nt = seq // tm
    n_rope = N_GROUPS * C_B
    n_slabs = w.shape[1] // SLAB_W
    w_slabs = w.astype(BF16).reshape(D_MODEL, n_slabs, SLAB_W).transpose(1, 0, 2)
    in_specs = [pl.BlockSpec((tm, D_MODEL), lambda i: (i, 0)),
                pl.BlockSpec((1, D_MODEL), lambda i: (0, 0)),
                pl.BlockSpec((n_slabs, D_MODEL, SLAB_W), lambda i: (0, 0, 0), pipeline_mode=pl.Buffered(1)),
                pl.BlockSpec((1, n_rope), lambda i: (0, 0)),
                pl.BlockSpec((tm, LANES), lambda i: (i, 0)),
                pl.BlockSpec((tm, LANES), lambda i: (i, 0))]
    shapes, specs = [], []
    for _, d in DILATED_GROUPS:
        shapes.append(jax.ShapeDtypeStruct((batch, d, seq // d, C_B), BF16))
        specs.append(pl.BlockSpec((1, d, tm // d, C_B), lambda i: (i // nt, 0, i % nt, 0)))
    if has_v:
        for _, d in DILATED_GROUPS:
            shapes.append(jax.ShapeDtypeStruct((batch, d, SLABS, seq // d, SLAB_W), BF16))
            specs.append(pl.BlockSpec((1, d, SLABS, tm // d, SLAB_W), lambda i: (i // nt, 0, 0, i % nt, 0)))
    return pl.pallas_call(
        functools.partial(_group_proj_kernel, has_v=has_v, scale=scale),
        out_shape=tuple(shapes),
        grid=(batch * nt,),
        in_specs=in_specs,
        out_specs=tuple(specs),
        scratch_shapes=[pltpu.VMEM((D_MODEL // LANES, tm, LANES), F32), pltpu.VMEM((2, tm, D_MODEL), BF16),
                        pltpu.VMEM((2, SLABS, tm, SLAB_W), F32)],
        compiler_params=_params(("parallel",)),
        name="group_proj_kv" if has_v else "group_proj_q",
    )(x2d, gain.reshape(1, D_MODEL), w_slabs, head_gain.reshape(1, n_rope), cos_b, sin_b)


BAND_HEADS = 4


def _band_kernel(q_ref, k_ref, v_ref, o_ref, lse_ref, *, dilation, length):
    nk = min(2 * BAND, length)
    row = lax.broadcasted_iota(jnp.int32, (BAND, nk), 0)
    col = lax.broadcasted_iota(jnp.int32, (BAND, nk), 1)
    lane_head = lax.broadcasted_iota(jnp.int32, (BAND, LANES), 1) // (LANES // BAND_HEADS)

    def block(r, i):
        q0 = i * BAND
        start = jnp.maximum(q0 + BAND - nk, 0)
        dist = (q0 - start) + row - col
        valid = jnp.logical_and(dist >= 0, dist <= BAND)
        if not isinstance(q0, int):
            q0 = pl.multiple_of(q0, BAND)
            start = pl.multiple_of(start, BAND)
        scores = []
        for h in range(BAND_HEADS):
            hs = slice(h * HEAD_DIM_B, (h + 1) * HEAD_DIM_B)
            s = _dot_nt(q_ref[0, r, pl.ds(q0, BAND), hs], k_ref[0, r, pl.ds(start, nk), hs])
            scores.append(jnp.where(valid, s, NEG_INF))
        probs, denoms, lse_tile = [], [], jnp.zeros((BAND, LANES), F32)
        for h in range(BAND_HEADS):
            m = jnp.max(scores[h], axis=-1, keepdims=True)
            p = jnp.exp2(scores[h] - m)
            denom = jnp.sum(p, axis=-1, keepdims=True)
            probs.append(p.astype(BF16))
            denoms.append(denom)
            lse_tile = jnp.where(lane_head == h, m + jnp.log2(denom), lse_tile)
        tok = pl.ds(q0 * dilation + r, BAND, stride=dilation) if dilation > 1 else pl.ds(q0, BAND)
        for h in range(BAND_HEADS):
            hs = slice(h * HEAD_DIM_B, (h + 1) * HEAD_DIM_B)
            lo = h * HEAD_DIM_B % SLAB_W
            out = _dot(probs[h], v_ref[0, r, h * HEAD_DIM_B // SLAB_W, pl.ds(start, nk), lo:lo + HEAD_DIM_B])
            o_ref[0, h, tok, :] = out / denoms[h]
        lse_ref[0, 0, tok, :] = lse_tile

    nb = length // BAND
    for r in range(dilation):
        if nb == 1:
            block(r, 0)
        else:
            def body(i, carry, r=r):
                block(r, i)
                return carry
            lax.fori_loop(0, nb, body, 0, unroll=4)


def _band_attention(q, k, v, dilation, batch, seq):
    length = seq // dilation
    halves = H_B // BAND_HEADS
    cw = BAND_HEADS * HEAD_DIM_B
    blk = pl.BlockSpec((1, dilation, length, cw), lambda b, hh: (b, 0, 0, hh))
    v_blk = pl.BlockSpec((1, dilation, cw // SLAB_W, length, SLAB_W), lambda b, hh: (b, 0, hh, 0, 0))
    return pl.pallas_call(
        functools.partial(_band_kernel, dilation=dilation, length=length),
        out_shape=(jax.ShapeDtypeStruct((batch, H_B, seq, HEAD_DIM_B), F32),
                   jax.ShapeDtypeStruct((batch, halves, seq, LANES), F32)),
        grid=(batch, halves),
        in_specs=[blk, blk, v_blk],
        out_specs=(pl.BlockSpec((1, BAND_HEADS, seq, HEAD_DIM_B), lambda b, hh: (b, hh, 0, 0)),
                   pl.BlockSpec((1, 1, seq, LANES), lambda b, hh: (b, hh, 0, 0))),
        compiler_params=_params(("parallel", "parallel")),
        name=f"band_attn_d{dilation}",
    )(q, k, v)


def _combine_proj_kernel(o0_ref, o1_ref, o2_ref, l0_ref, l1_ref, l2_ref, w_ref, x_ref, y_ref, o_scr):
    lanes_per_head = LANES // BAND_HEADS
    for half in range(H_B // BAND_HEADS):
        l0, l1, l2 = l0_ref[0, half], l1_ref[0, half], l2_ref[0, half]
        mx = jnp.maximum(jnp.maximum(l0, l1), l2)
        e0, e1, e2 = jnp.exp2(l0 - mx), jnp.exp2(l1 - mx), jnp.exp2(l2 - mx)
        inv = 1.0 / (e0 + e1 + e2)
        w0, w1, w2 = e0 * inv, e1 * inv, e2 * inv
        for hq in range(BAND_HEADS):
            h = half * BAND_HEADS + hq
            c = hq * lanes_per_head
            o = (w0[:, c:c + 1] * o0_ref[0, h] + w1[:, c:c + 1] * o1_ref[0, h] + w2[:, c:c + 1] * o2_ref[0, h])
            o_scr[:, h * HEAD_DIM_B:(h + 1) * HEAD_DIM_B] = o.astype(BF16)
    y_ref[...] = x_ref[...] + _dot(o_scr[...], w_ref[...])


def _combine_proj(outs, lses, w_o, x2d, seq):
    T = x2d.shape[0]
    tm = 512
    nt = seq // tm
    halves = H_B // BAND_HEADS
    o_spec = pl.BlockSpec((1, H_B, tm, HEAD_DIM_B), lambda i: (i // nt, 0, i % nt, 0))
    l_spec = pl.BlockSpec((1, halves, tm, LANES), lambda i: (i // nt, 0, i % nt, 0))
    return pl.pallas_call(
        _combine_proj_kernel,
        out_shape=jax.ShapeDtypeStruct((T, D_MODEL), F32),
        grid=(T // tm,),
        in_specs=[o_spec, o_spec, o_spec, l_spec, l_spec, l_spec,
                  pl.BlockSpec((C_B, D_MODEL), lambda i: (0, 0)),
                  pl.BlockSpec((tm, D_MODEL), lambda i: (i, 0))],
        out_specs=pl.BlockSpec((tm, D_MODEL), lambda i: (i, 0)),
        scratch_shapes=[pltpu.VMEM((tm, C_B), BF16)],
        compiler_params=_params(("parallel",)),
        name="combine_proj",
    )(*outs, *lses, w_o.astype(BF16), x2d)


def kernel(x, positions, attn_norm, mlp_norm, mla_w_in, mla_qa_norm, mla_kva_norm, mla_w_qb, mla_w_kvb,
           mla_q_norm, mla_k_norm, mla_w_o, kv_norm, w_kv, k_norm_b, w_q_b, q_norm_b, w_o_b, mlp_w1, mlp_w2):
    B, S, D = x.shape
    T = B * S
    cos_a, sin_a, cos_b, sin_b = _rope_tables(positions)
    h = x.reshape(T, D)

    def head_gains(gn):
        return jnp.broadcast_to(gn[:, None, :], (N_GROUPS, H_B, HEAD_DIM_B))

    for a in range(N_A_LAYERS):
        q, k, v = _mla_proj(h, attn_norm[a], mla_w_in[a], mla_qa_norm[a], mla_kva_norm[a], mla_w_qb[a],
                            mla_w_kvb[a], mla_q_norm[a], mla_k_norm[a], cos_a, sin_a)
        o = _mla_attention(q, k, v, B, S)
        h = _out_proj(o, mla_w_o[a], h)
        h = _mlp(h, mlp_norm[a], mlp_w1[a], mlp_w2[a])

    kv = _group_proj(h, kv_norm, w_kv, head_gains(k_norm_b), True, 1.0, cos_b, sin_b, B, S)
    ks, vs = kv[:N_GROUPS], kv[N_GROUPS:]

    for b in range(N_B_LAYERS):
        layer = N_A_LAYERS + b
        qs = _group_proj(h, attn_norm[layer], w_q_b[b], head_gains(q_norm_b[b]), False,
                         HEAD_DIM_B ** -0.5 * LOG2E, cos_b, sin_b, B, S)
        outs, lses = [], []
        for g, (window, dilation) in enumerate(DILATED_GROUPS):
            assert window // dilation == BAND
            o, lse = _band_attention(qs[g], ks[g], vs[g], dilation, B, S)
            outs.append(o)
            lses.append(lse)
        h = _combine_proj(outs, lses, w_o_b[b], h, S)
        h = _mlp(h, mlp_norm[layer], mlp_w1[layer], mlp_w2[layer])

    return h.reshape(B, S, D)
```

```python
import functools

import jax
import jax.numpy as jnp
from jax import lax
from jax.experimental import pallas as pl
from jax.experimental.pallas import tpu as pltpu

D_MODEL = 1024
N_A_LAYERS = 2
N_B_LAYERS = 2
H_A = 16
QK_NOPE = 128
QK_ROPE = 64
QK_HEAD = QK_NOPE + QK_ROPE
V_HEAD = 128
Q_LORA = 256
KV_LORA = 128
DILATED_GROUPS = ((128, 1), (512, 4), (2048, 16))
N_GROUPS = 3
H_B = 8
HEAD_DIM_B = 128
C_B = H_B * HEAD_DIM_B
D_FF = 4 * D_MODEL
ROPE_THETA = 10000.0
NORM_EPS = 1e-6
NEG_INF = -1e30
LOG2E = 1.4426950408889634

LANES = 128
QK_SLOT = 2 * LANES
VMEM_LIMIT = 56 * 1024 * 1024
BAND = 128
PROJ_TILE = 512
PROJ_CHUNK = 128
SLAB_W = 2 * LANES
SLABS = C_B // SLAB_W

BF16 = jnp.bfloat16
F32 = jnp.float32


def _params(semantics):
    return pltpu.CompilerParams(dimension_semantics=semantics, vmem_limit_bytes=VMEM_LIMIT)


def _rms(x, gain):
    ms = jnp.mean(x * x, axis=-1, keepdims=True)
    return x * lax.rsqrt(ms + NORM_EPS) * gain


def _rot_half(u):
    return pltpu.roll(u, LANES // 2, axis=1)


def _dot(a, b):
    return jnp.dot(a, b, preferred_element_type=F32)


def _dot_nt(a, b):
    return lax.dot_general(a, b, (((1,), (1,)), ((), ())), preferred_element_type=F32)


def _tables_kernel(pos_ref, f_ref, cm_ref, sm_ref, cos_ref, sin_ref):
    ang = pos_ref[...] * f_ref[...]
    cos_ref[...] = jnp.cos(ang) * cm_ref[...]
    sin_ref[...] = jnp.sin(ang) * sm_ref[...]


def _rope_table(pos, freq, cos_mask, sin_sign):
    n = pos.shape[0]
    tm = 1024
    row = pl.BlockSpec((1, LANES), lambda i: (0, 0))
    tab = pl.BlockSpec((tm, LANES), lambda i: (i, 0))
    shp = jax.ShapeDtypeStruct((n, LANES), F32)
    return pl.pallas_call(
        _tables_kernel,
        out_shape=(shp, shp),
        grid=(n // tm,),
        in_specs=[pl.BlockSpec((tm, 1), lambda i: (i, 0)), row, row, row],
        out_specs=(tab, tab),
        compiler_params=_params(("parallel",)),
        name="rope_table",
    )(pos.reshape(n, 1), freq.reshape(1, LANES), cos_mask.reshape(1, LANES), sin_sign.reshape(1, LANES))


def _rope_tables(positions):
    B, S = positions.shape
    pos = positions.astype(F32)
    inv_a = ROPE_THETA ** (-jnp.arange(0, QK_ROPE, 2, dtype=F32) / QK_ROPE)
    inv_b = ROPE_THETA ** (-jnp.arange(0, HEAD_DIM_B, 2, dtype=F32) / HEAD_DIM_B)
    z32 = jnp.zeros((32,), F32)
    o32 = jnp.ones((32,), F32)
    o64 = jnp.ones((64,), F32)
    cos_a, sin_a = _rope_table(pos.reshape(B * S), jnp.concatenate([inv_a, z32, inv_a, z32]),
                               jnp.concatenate([o32, z32, o32, z32]), jnp.concatenate([-o32, z32, o32, z32]))
    cos_b, sin_b = _rope_table(pos.reshape(B * S), jnp.concatenate([inv_b, inv_b]),
                               jnp.concatenate([o64, o64]), jnp.concatenate([-o64, o64]))
    return cos_a, sin_a, cos_b, sin_b


def _mla_proj_kernel(x_ref, g_ref, win_ref, qa_ref, kva_ref, wqb_ref, wkb_ref, wvb_ref, qg_ref, kg_ref,
                     cos_ref, sin_ref, q_ref, k_ref, v_ref, cq_scr, ckv_scr, kpe_scr, kss_scr, q_scr, kn_scr,
                     *, scale):
    tm = x_ref.shape[0]
    n_pairs = H_A // 2
    xn = _rms(x_ref[...], g_ref[...]).astype(BF16)
    lat = _dot(xn, win_ref[...])
    cq_scr[...] = _rms(lat[:, :Q_LORA], qa_ref[...]).astype(BF16)
    ckv_scr[...] = _rms(lat[:, Q_LORA:Q_LORA + KV_LORA], kva_ref[...]).astype(BF16)
    k_pe = lat[:, Q_LORA + KV_LORA:]
    kss_scr[...] = jnp.broadcast_to(jnp.sum(k_pe * k_pe, axis=-1, keepdims=True), (tm, LANES))
    kpe_g = k_pe * kg_ref[:, LANES:]
    kpe_scr[...] = kpe_g * cos_ref[...] + _rot_half(kpe_g) * sin_ref[...]

    def matmuls(pair, buf):
        q_scr[buf] = _dot(cq_scr[...], wqb_ref[pair])
        kn_scr[buf] = _dot(ckv_scr[...], wkb_ref[pair])

    def finish(pair, buf):
        v_ref[pair] = _dot(ckv_scr[...], wvb_ref[pair]).astype(BF16)
        qg_n, qg_pe = qg_ref[:, :LANES], qg_ref[:, LANES:]
        kg_n = kg_ref[:, :LANES]
        for c in range(tm // PROJ_CHUNK):
            rows = slice(c * PROJ_CHUNK, (c + 1) * PROJ_CHUNK)
            cos = cos_ref[rows, :]
            sin = sin_ref[rows, :]
            kpe_ss = kss_scr[rows, :]
            kpe_rot = kpe_scr[rows, :]
            for hh in range(2):
                qn = q_scr[buf, rows, hh * QK_SLOT:hh * QK_SLOT + LANES]
                qp = q_scr[buf, rows, hh * QK_SLOT + LANES:(hh + 1) * QK_SLOT]
                ss = jnp.sum(qn * qn + qp * qp, axis=-1, keepdims=True)
                rs = lax.rsqrt(ss * (1.0 / QK_HEAD) + NORM_EPS) * scale
                qpg = qp * qg_pe
                q_ref[pair, rows, hh * QK_SLOT:hh * QK_SLOT + LANES] = (qn * rs * qg_n).astype(BF16)
                q_ref[pair, rows, hh * QK_SLOT + LANES:(hh + 1) * QK_SLOT] = (
                    (qpg * cos + _rot_half(qpg) * sin) * rs).astype(BF16)
                kn = kn_scr[buf, rows, hh * LANES:(hh + 1) * LANES]
                ssk = jnp.sum(kn * kn, axis=-1, keepdims=True) + kpe_ss
                rsk = lax.rsqrt(ssk * (1.0 / QK_HEAD) + NORM_EPS)
                k_ref[pair, rows, hh * QK_SLOT:hh * QK_SLOT + LANES] = (kn * rsk * kg_n).astype(BF16)
                k_ref[pair, rows, hh * QK_SLOT + LANES:(hh + 1) * QK_SLOT] = (kpe_rot * rsk).astype(BF16)

    matmuls(0, 0)

    def body(jj, carry):
        pair = 2 * jj
        matmuls(pair + 1, 1)
        finish(pair, 0)
        matmuls(jnp.minimum(pair + 2, n_pairs - 1), 0)
        finish(pair + 1, 1)
        return carry

    lax.fori_loop(0, n_pairs // 2, body, 0)


def _rope_tile_cols(a):
    z = jnp.zeros(a.shape[:-1] + (32,), a.dtype)
    return jnp.concatenate([a[..., :32], z, a[..., 32:], z], axis=-1)


def _mla_proj(x2d, gain, w_in, qa_norm, kva_norm, w_qb, w_kvb, q_norm, k_norm, cos_a, sin_a):
    T = x2d.shape[0]
    tm = 256
    w_in_p = jnp.concatenate(
        [w_in[:, :Q_LORA + KV_LORA], _rope_tile_cols(w_in[:, Q_LORA + KV_LORA:])], axis=-1).astype(BF16)
    wq = w_qb.reshape(Q_LORA, H_A, QK_HEAD)
    wq_p = jnp.concatenate([wq[..., :QK_NOPE], _rope_tile_cols(wq[..., QK_NOPE:])], axis=-1)
    n_pairs = H_A // 2

    def pair_slabs(w2d):
        k_dim = w2d.shape[0]
        return w2d.reshape(k_dim, n_pairs, -1).transpose(1, 0, 2).astype(BF16)

    wq_p = pair_slabs(wq_p.reshape(Q_LORA, H_A * QK_SLOT))
    wkv = w_kvb.reshape(KV_LORA, H_A, QK_NOPE + V_HEAD)
    wkb = pair_slabs(wkv[..., :QK_NOPE].reshape(KV_LORA, H_A * QK_NOPE))
    wvb = pair_slabs(wkv[..., QK_NOPE:].reshape(KV_LORA, H_A * V_HEAD))
    qg = jnp.concatenate([q_norm[:QK_NOPE], _rope_tile_cols(q_norm[QK_NOPE:])]).reshape(1, QK_SLOT)
    kg = jnp.concatenate([k_norm[:QK_NOPE], _rope_tile_cols(k_norm[QK_NOPE:])]).reshape(1, QK_SLOT)

    def const(shape):
        return pl.BlockSpec(shape, lambda i: (0,) * len(shape))

    def rows(width):
        return pl.BlockSpec((tm, width), lambda i: (i, 0))

    def pair_rows(width):
        return pl.BlockSpec((n_pairs, tm, width), lambda i: (0, i, 0))

    n_in = Q_LORA + KV_LORA + LANES
    return pl.pallas_call(
        functools.partial(_mla_proj_kernel, scale=QK_HEAD ** -0.5 * LOG2E),
        out_shape=(jax.ShapeDtypeStruct((n_pairs, T, 2 * QK_SLOT), BF16),
                   jax.ShapeDtypeStruct((n_pairs, T, 2 * QK_SLOT), BF16),
                   jax.ShapeDtypeStruct((n_pairs, T, 2 * V_HEAD), BF16)),
        grid=(T // tm,),
        in_specs=[rows(D_MODEL), const((1, D_MODEL)), const((D_MODEL, n_in)), const((1, Q_LORA)),
                  const((1, KV_LORA)), const((n_pairs, Q_LORA, 2 * QK_SLOT)), const((n_pairs, KV_LORA, 2 * QK_NOPE)),
                  const((n_pairs, KV_LORA, 2 * V_HEAD)), const((1, QK_SLOT)), const((1, QK_SLOT)),
                  rows(LANES), rows(LANES)],
        out_specs=(pair_rows(2 * QK_SLOT), pair_rows(2 * QK_SLOT), pair_rows(2 * V_HEAD)),
        scratch_shapes=[pltpu.VMEM((tm, Q_LORA), BF16), pltpu.VMEM((tm, KV_LORA), BF16),
                        pltpu.VMEM((tm, LANES), F32), pltpu.VMEM((tm, LANES), F32),
                        pltpu.VMEM((2, tm, 2 * QK_SLOT), F32), pltpu.VMEM((2, tm, 2 * QK_NOPE), F32)],
        compiler_params=_params(("parallel",)),
        name="mla_proj",
    )(x2d, gain.reshape(1, D_MODEL), w_in_p, qa_norm.reshape(1, Q_LORA), kva_norm.reshape(1, KV_LORA),
      wq_p, wkb, wvb, qg, kg, cos_a, sin_a)


FLASH_HEADS = 8


FLASH_ROWS = 64


def _flash_kernel(q_ref, k_ref, v_ref, o_ref, s_scr, p_scr, m_scr, l_scr, a_scr, acc_scr, *, tq, tk):
    i = pl.program_id(2)
    m_scr[...] = jnp.full(m_scr.shape, NEG_INF, F32)
    l_scr[...] = jnp.zeros(l_scr.shape, F32)
    acc_scr[...] = jnp.zeros(acc_scr.shape, F32)

    def scores(j, buf):
        off = pl.multiple_of(j * tk, tk)
        for h in range(FLASH_HEADS):
            qk_cols = slice((h % 2) * QK_SLOT, (h % 2 + 1) * QK_SLOT)
            s_scr[buf, h] = _dot_nt(q_ref[h // 2, :, qk_cols], k_ref[h // 2, pl.ds(off, tk), qk_cols])

    def softmax_pv(j, buf, masked):
        off = pl.multiple_of(j * tk, tk)
        for h in range(FLASH_HEADS):
            for rb in range(tq // FLASH_ROWS):
                rows = slice(rb * FLASH_ROWS, (rb + 1) * FLASH_ROWS)
                cw = min(tk, -(-((rb + 1) * FLASH_ROWS) // LANES) * LANES) if masked else tk
                s = s_scr[buf, h, rows, :cw]
                if masked:
                    row = lax.broadcasted_iota(jnp.int32, (FLASH_ROWS, cw), 0) + rb * FLASH_ROWS
                    col = lax.broadcasted_iota(jnp.int32, (FLASH_ROWS, cw), 1)
                    s = jnp.where(row >= col, s, NEG_INF)
                m_old = m_scr[h, rows, :]
                m_new = jnp.maximum(m_old, jnp.max(s, axis=-1, keepdims=True))
                p = jnp.exp2(s - jnp.tile(m_new, (1, cw // LANES)))
                alpha = jnp.exp2(m_old - m_new)
                l_scr[h, rows, :] = alpha * l_scr[h, rows, :] + jnp.sum(p, axis=-1, keepdims=True)
                m_scr[h, rows, :] = m_new
                a_scr[h, rows, :] = alpha
                p_scr[h, rows, :cw] = p.astype(BF16)
                if cw < tk:
                    p_scr[h, rows, cw:] = jnp.zeros((FLASH_ROWS, tk - cw), BF16)
            pv = _dot(p_scr[h], v_ref[h // 2, pl.ds(off, tk), (h % 2) * V_HEAD:(h % 2 + 1) * V_HEAD])
            acc_scr[h] = a_scr[h] * acc_scr[h] + pv

    def body(j, carry):
        scores(j, 0)
        softmax_pv(j, 0, False)
        return carry

    assert tq == tk
    lax.fori_loop(0, i, body, 0)
    scores(i, 0)
    softmax_pv(i, 0, True)
    for h in range(FLASH_HEADS):
        o_ref[:, h * V_HEAD:(h + 1) * V_HEAD] = (acc_scr[h] / l_scr[h]).astype(o_ref.dtype)


def _mla_attention(q, k, v, batch, seq):
    tq = tk = 512
    nq = seq // tq
    T = q.shape[1]
    hh = FLASH_HEADS
    assert q.shape[0] * 2 == H_A and hh % 2 == 0
    stat = pltpu.VMEM((hh, tq, LANES), F32)
    return pl.pallas_call(
        functools.partial(_flash_kernel, tq=tq, tk=tk),
        out_shape=jax.ShapeDtypeStruct((T, H_A * V_HEAD), BF16),
        grid=(batch, H_A // hh, nq),
        in_specs=[pl.BlockSpec((hh // 2, tq, 2 * QK_SLOT), lambda b, h, i: (h, b * nq + i, 0)),
                  pl.BlockSpec((hh // 2, seq, 2 * QK_SLOT), lambda b, h, i: (h, b, 0)),
                  pl.BlockSpec((hh // 2, seq, 2 * V_HEAD), lambda b, h, i: (h, b, 0))],
        out_specs=pl.BlockSpec((tq, hh * V_HEAD), lambda b, h, i: (b * nq + i, h)),
        scratch_shapes=[pltpu.VMEM((1, hh, tq, tk), F32), pltpu.VMEM((hh, tq, tk), BF16), stat, stat, stat,
                        pltpu.VMEM((hh, tq, V_HEAD), F32)],
        compiler_params=_params(("parallel", "parallel", "arbitrary")),
        name="mla_flash",
    )(q, k, v)


def _out_proj_kernel(o_ref, w_ref, x_ref, y_ref):
    y_ref[...] = x_ref[...] + _dot(o_ref[...], w_ref[...])


def _out_proj(o, w_o, x2d):
    T, K = o.shape
    tm = 512
    return pl.pallas_call(
        _out_proj_kernel,
        out_shape=jax.ShapeDtypeStruct((T, D_MODEL), F32),
        grid=(T // tm,),
        in_specs=[pl.BlockSpec((tm, K), lambda i: (i, 0)),
                  pl.BlockSpec((K, D_MODEL), lambda i: (0, 0)),
                  pl.BlockSpec((tm, D_MODEL), lambda i: (i, 0))],
        out_specs=pl.BlockSpec((tm, D_MODEL), lambda i: (i, 0)),
        compiler_params=_params(("parallel",)),
        name="out_proj",
    )(o, w_o.astype(BF16), x2d)


def _mlp_kernel(x_ref, g_ref, w1_ref, w2_ref, y_ref, xn_ref, acc_ref):
    f = pl.program_id(1)

    @pl.when(f == 0)
    def _():
        xn_ref[...] = _rms(x_ref[...], g_ref[...]).astype(BF16)
        acc_ref[...] = jnp.zeros_like(acc_ref)

    h = jnp.maximum(_dot(xn_ref[...], w1_ref[...]), 0.0)
    acc_ref[...] += _dot((h * h).astype(BF16), w2_ref[...])

    @pl.when(f == pl.num_programs(1) - 1)
    def _():
        y_ref[...] = x_ref[...] + acc_ref[...]


def _mlp(x2d, gain, w1, w2):
    T = x2d.shape[0]
    tm, tf = 1024, 1024
    return pl.pallas_call(
        _mlp_kernel,
        out_shape=jax.ShapeDtypeStruct((T, D_MODEL), F32),
        grid=(T // tm, D_FF // tf),
        in_specs=[pl.BlockSpec((tm, D_MODEL), lambda i, f: (i, 0)),
                  pl.BlockSpec((1, D_MODEL), lambda i, f: (0, 0)),
                  pl.BlockSpec((D_MODEL, tf), lambda i, f: (0, f)),
                  pl.BlockSpec((tf, D_MODEL), lambda i, f: (f, 0))],
        out_specs=pl.BlockSpec((tm, D_MODEL), lambda i, f: (i, 0)),
        scratch_shapes=[pltpu.VMEM((tm, D_MODEL), BF16), pltpu.VMEM((tm, D_MODEL), F32)],
        compiler_params=_params(("parallel", "arbitrary")),
        name="mlp",
    )(x2d, gain.reshape(1, D_MODEL), w1.astype(BF16), w2.astype(BF16))


def _group_proj_kernel(*refs, has_v, scale):
    if has_v:
        x_ref, g_ref, w_ref, hg_ref, cos_ref, sin_ref = refs[:6]
        outs = refs[6:12]
        xs_ref, xn_ref, y_ref = refs[12:]
    else:
        x_ref, g_ref, w_ref, hg_ref, cos_ref, sin_ref = refs[:6]
        outs = refs[6:9]
        xs_ref, xn_ref, y_ref = refs[9:]
    tm = x_ref.shape[0]
    n_col = D_MODEL // LANES
    n_chunks = tm // PROJ_CHUNK
    assert n_chunks == SLABS
    xn = _rms(x_ref[...], g_ref[...])
    for c in range(n_col):
        xs_ref[c] = xn[:, c * LANES:(c + 1) * LANES]

    def permute(gi, buf):
        d = DILATED_GROUPS[gi][1]
        rows = tm // d
        if d == 1:
            xn_ref[buf] = xn.astype(BF16)
        else:
            for r in range(d):
                for c in range(n_col):
                    xn_ref[buf, r * rows:(r + 1) * rows, c * LANES:(c + 1) * LANES] = (
                        xs_ref[c, pl.ds(r, rows, stride=d), :].astype(BF16))

    def norm_chunk(gi, buf, ci):
        d = DILATED_GROUPS[gi][1]
        rows = tm // d
        out_ref = outs[gi]
        r0 = pl.multiple_of(ci * PROJ_CHUNK, PROJ_CHUNK)

        def table_rows(t_ref):
            if d == 1:
                return t_ref[pl.ds(r0, PROJ_CHUNK), :]
            if rows >= PROJ_CHUNK:
                per = rows // PROJ_CHUNK
                return t_ref[pl.ds((ci % per) * PROJ_CHUNK * d + ci // per, PROJ_CHUNK, stride=d), :]
            per = PROJ_CHUNK // rows
            return jnp.concatenate([t_ref[pl.ds(ci * per + s, rows, stride=d), :] for s in range(per)], axis=0)

        cos = table_rows(cos_ref)
        sin = table_rows(sin_ref)
        for h in range(H_B):
            hs = slice(h * HEAD_DIM_B, (h + 1) * HEAD_DIM_B)
            sl = h * HEAD_DIM_B // SLAB_W
            lo = h * HEAD_DIM_B % SLAB_W
            yh = y_ref[buf, sl, pl.ds(r0, PROJ_CHUNK), lo:lo + HEAD_DIM_B]
            rs = lax.rsqrt(jnp.mean(yh * yh, axis=-1, keepdims=True) + NORM_EPS) * scale
            yg = yh * hg_ref[:, gi * C_B + h * HEAD_DIM_B:gi * C_B + (h + 1) * HEAD_DIM_B]
            res = ((yg * cos + _rot_half(yg) * sin) * rs).astype(BF16)
            if rows >= PROJ_CHUNK:
                per = rows // PROJ_CHUNK
                out_ref[0, ci // per, pl.ds(pl.multiple_of((ci % per) * PROJ_CHUNK, PROJ_CHUNK), PROJ_CHUNK),
                        hs] = res
            else:
                per = PROJ_CHUNK // rows
                for s in range(per):
                    out_ref[0, ci * per + s, :, hs] = res[s * rows:(s + 1) * rows]

    def v_slab(gi, buf, ci):
        d = DILATED_GROUPS[gi][1]
        rows = tm // d
        yv = _dot(xn_ref[buf], w_ref[(N_GROUPS + gi) * SLABS + ci])
        for r in range(d):
            outs[N_GROUPS + gi][0, r, ci] = yv[r * rows:(r + 1) * rows].astype(BF16)

    permute(0, 0)
    for s in range(SLABS):
        y_ref[0, s] = _dot(xn_ref[0], w_ref[s])
    for gi in range(N_GROUPS):
        buf = gi % 2
        if gi + 1 < N_GROUPS:
            permute(gi + 1, 1 - buf)

        def body(ci, carry, gi=gi, buf=buf):
            norm_chunk(gi, buf, ci)
            if gi + 1 < N_GROUPS:
                y_ref[1 - buf, ci] = _dot(xn_ref[1 - buf], w_ref[(gi + 1) * SLABS + ci])
            if has_v:
                v_slab(gi, buf, ci)
            return carry

        lax.fori_loop(0, n_chunks, body, 0)


def _group_proj(x2d, gain, w, head_gain, has_v, scale, cos_b, sin_b, batch, seq):
    tm = PROJ_TILE
    nt = seq // tm
    n_rope = N_GROUPS * C_B
    n_slabs = w.shape[1] // SLAB_W
    w_slabs = w.astype(BF16).reshape(D_MODEL, n_slabs, SLAB_W).transpose(1, 0, 2)
    in_specs = [pl.BlockSpec((tm, D_MODEL), lambda i: (i, 0)),
                pl.BlockSpec((1, D_MODEL), lambda i: (0, 0)),
                pl.BlockSpec((n_slabs, D_MODEL, SLAB_W), lambda i: (0, 0, 0), pipeline_mode=pl.Buffered(1)),
                pl.BlockSpec((1, n_rope), lambda i: (0, 0)),
                pl.BlockSpec((tm, LANES), lambda i: (i, 0)),
                pl.BlockSpec((tm, LANES), lambda i: (i, 0))]
    shapes, specs = [], []
    for _, d in DILATED_GROUPS:
        shapes.append(jax.ShapeDtypeStruct((batch, d, seq // d, C_B), BF16))
        specs.append(pl.BlockSpec((1, d, tm // d, C_B), lambda i: (i // nt, 0, i % nt, 0)))
    if has_v:
        for _, d in DILATED_GROUPS:
            shapes.append(jax.ShapeDtypeStruct((batch, d, SLABS, seq // d, SLAB_W), BF16))
            specs.append(pl.BlockSpec((1, d, SLABS, tm // d, SLAB_W), lambda i: (i // nt, 0, 0, i % nt, 0)))
    return pl.pallas_call(
        functools.partial(_group_proj_kernel, has_v=has_v, scale=scale),
        out_shape=tuple(shapes),
        grid=(batch * nt,),
        in_specs=in_specs,
        out_specs=tuple(specs),
        scratch_shapes=[pltpu.VMEM((D_MODEL // LANES, tm, LANES), F32), pltpu.VMEM((2, tm, D_MODEL), BF16),
                        pltpu.VMEM((2, SLABS, tm, SLAB_W), F32)],
        compiler_params=_params(("parallel",)),
        name="group_proj_kv" if has_v else "group_proj_q",
    )(x2d, gain.reshape(1, D_MODEL), w_slabs, head_gain.reshape(1, n_rope), cos_b, sin_b)


BAND_HEADS = 4


def _band_kernel(q_ref, k_ref, v_ref, o_ref, lse_ref, *, dilation, length):
    nk = min(2 * BAND, length)
    row = lax.broadcasted_iota(jnp.int32, (BAND, nk), 0)
    col = lax.broadcasted_iota(jnp.int32, (BAND, nk), 1)
    lane_head = lax.broadcasted_iota(jnp.int32, (BAND, LANES), 1) // (LANES // BAND_HEADS)

    def block(r, i):
        q0 = i * BAND
        start = jnp.maximum(q0 + BAND - nk, 0)
        dist = (q0 - start) + row - col
        valid = jnp.logical_and(dist >= 0, dist <= BAND)
        if not isinstance(q0, int):
            q0 = pl.multiple_of(q0, BAND)
            start = pl.multiple_of(start, BAND)
        scores = []
        for h in range(BAND_HEADS):
            hs = slice(h * HEAD_DIM_B, (h + 1) * HEAD_DIM_B)
            s = _dot_nt(q_ref[0, r, pl.ds(q0, BAND), hs], k_ref[0, r, pl.ds(start, nk), hs])
            scores.append(jnp.where(valid, s, NEG_INF))
        probs, denoms, lse_tile = [], [], jnp.zeros((BAND, LANES), F32)
        for h in range(BAND_HEADS):
            m = jnp.max(scores[h], axis=-1, keepdims=True)
            p = jnp.exp2(scores[h] - m)
            denom = jnp.sum(p, axis=-1, keepdims=True)
            probs.append(p.astype(BF16))
            denoms.append(denom)
            lse_tile = jnp.where(lane_head == h, m + jnp.log2(denom), lse_tile)
        tok = pl.ds(q0 * dilation + r, BAND, stride=dilation) if dilation > 1 else pl.ds(q0, BAND)
        for h in range(BAND_HEADS):
            hs = slice(h * HEAD_DIM_B, (h + 1) * HEAD_DIM_B)
            lo = h * HEAD_DIM_B % SLAB_W
            out = _dot(probs[h], v_ref[0, r, h * HEAD_DIM_B // SLAB_W, pl.ds(start, nk), lo:lo + HEAD_DIM_B])
            o_ref[0, h, tok, :] = out / denoms[h]
        lse_ref[0, 0, tok, :] = lse_tile

    nb = length // BAND
    for r in range(dilation):
        if nb == 1:
            block(r, 0)
        else:
            def body(i, carry, r=r):
                block(r, i)
                return carry
            lax.fori_loop(0, nb, body, 0, unroll=4)


def _band_attention(q, k, v, dilation, batch, seq):
    length = seq // dilation
    halves = H_B // BAND_HEADS
    cw = BAND_HEADS * HEAD_DIM_B
    blk = pl.BlockSpec((1, dilation, length, cw), lambda b, hh: (b, 0, 0, hh))
    v_blk = pl.BlockSpec((1, dilation, cw // SLAB_W, length, SLAB_W), lambda b, hh: (b, 0, hh, 0, 0))
    return pl.pallas_call(
        functools.partial(_band_kernel, dilation=dilation, length=length),
        out_shape=(jax.ShapeDtypeStruct((batch, H_B, seq, HEAD_DIM_B), F32),
                   jax.ShapeDtypeStruct((batch, halves, seq, LANES), F32)),
        grid=(batch, halves),
        in_specs=[blk, blk, v_blk],
        out_specs=(pl.BlockSpec((1, BAND_HEADS, seq, HEAD_DIM_B), lambda b, hh: (b, hh, 0, 0)),
                   pl.BlockSpec((1, 1, seq, LANES), lambda b, hh: (b, hh, 0, 0))),
        compiler_params=_params(("parallel", "parallel")),
        name=f"band_attn_d{dilation}",
    )(q, k, v)


def _combine_proj_kernel(o0_ref, o1_ref, o2_ref, l0_ref, l1_ref, l2_ref, w_ref, x_ref, y_ref, o_scr):
    lanes_per_head = LANES // BAND_HEADS
    for half in range(H_B // BAND_HEADS):
        l0, l1, l2 = l0_ref[0, half], l1_ref[0, half], l2_ref[0, half]
        mx = jnp.maximum(jnp.maximum(l0, l1), l2)
        e0, e1, e2 = jnp.exp2(l0 - mx), jnp.exp2(l1 - mx), jnp.exp2(l2 - mx)
        inv = 1.0 / (e0 + e1 + e2)
        w0, w1, w2 = e0 * inv, e1 * inv, e2 * inv
        for hq in range(BAND_HEADS):
            h = half * BAND_HEADS + hq
            c = hq * lanes_per_head
            o = (w0[:, c:c + 1] * o0_ref[0, h] + w1[:, c:c + 1] * o1_ref[0, h] + w2[:, c:c + 1] * o2_ref[0, h])
            o_scr[:, h * HEAD_DIM_B:(h + 1) * HEAD_DIM_B] = o.astype(BF16)
    y_ref[...] = x_ref[...] + _dot(o_scr[...], w_ref[...])


def _combine_proj(outs, lses, w_o, x2d, seq):
    T = x2d.shape[0]
    tm = 512
    nt = seq // tm
    halves = H_B // BAND_HEADS
    o_spec = pl.BlockSpec((1, H_B, tm, HEAD_DIM_B), lambda i: (i // nt, 0, i % nt, 0))
    l_spec = pl.BlockSpec((1, halves, tm, LANES), lambda i: (i // nt, 0, i % nt, 0))
    return pl.pallas_call(
        _combine_proj_kernel,
        out_shape=jax.ShapeDtypeStruct((T, D_MODEL), F32),
        grid=(T // tm,),
        in_specs=[o_spec, o_spec, o_spec, l_spec, l_spec, l_spec,
                  pl.BlockSpec((C_B, D_MODEL), lambda i: (0, 0)),
                  pl.BlockSpec((tm, D_MODEL), lambda i: (i, 0))],
        out_specs=pl.BlockSpec((tm, D_MODEL), lambda i: (i, 0)),
        scratch_shapes=[pltpu.VMEM((tm, C_B), BF16)],
        compiler_params=_params(("parallel",)),
        name="combine_proj",
    )(*outs, *lses, w_o.astype(BF16), x2d)


def kernel(x, positions, attn_norm, mlp_norm, mla_w_in, mla_qa_norm, mla_kva_norm, mla_w_qb, mla_w_kvb,
           mla_q_norm, mla_k_norm, mla_w_o, kv_norm, w_kv, k_norm_b, w_q_b, q_norm_b, w_o_b, mlp_w1, mlp_w2):
    B, S, D = x.shape
    T = B * S
    cos_a, sin_a, cos_b, sin_b = _rope_tables(positions)
    h = x.reshape(T, D)

    def head_gains(gn):
        return jnp.broadcast_to(gn[:, None, :], (N_GROUPS, H_B, HEAD_DIM_B))

    for a in range(N_A_LAYERS):
        q, k, v = _mla_proj(h, attn_norm[a], mla_w_in[a], mla_qa_norm[a], mla_kva_norm[a], mla_w_qb[a],
                            mla_w_kvb[a], mla_q_norm[a], mla_k_norm[a], cos_a, sin_a)
        o = _mla_attention(q, k, v, B, S)
        h = _out_proj(o, mla_w_o[a], h)
        h = _mlp(h, mlp_norm[a], mlp_w1[a], mlp_w2[a])

    kv = _group_proj(h, kv_norm, w_kv, head_gains(k_norm_b), True, 1.0, cos_b, sin_b, B, S)
    ks, vs = kv[:N_GROUPS], kv[N_GROUPS:]

    for b in range(N_B_LAYERS):
        layer = N_A_LAYERS + b
        qs = _group_proj(h, attn_norm[layer], w_q_b[b], head_gains(q_norm_b[b]), False,
                         HEAD_DIM_B ** -0.5 * LOG2E, cos_b, sin_b, B, S)
        outs, lses = [], []
        for g, (window, dilation) in enumerate(DILATED_GROUPS):
            assert window // dilation == BAND
            o, lse = _band_attention(qs[g], ks[g], vs[g], dilation, B, S)
            outs.append(o)
            lses.append(lse)
        h = _combine_proj(outs, lses, w_o_b[b], h, S)
        h = _mlp(h, mlp_norm[layer], mlp_w1[layer], mlp_w2[layer])

    return h.reshape(B, S, D)
```

```python
import functools

import jax
import jax.numpy as jnp
from jax import lax
from jax.experimental import pallas as pl
from jax.experimental.pallas import tpu as pltpu

D_MODEL = 1024
N_A_LAYERS = 2
N_B_LAYERS = 2
H_A = 16
QK_NOPE = 128
QK_ROPE = 64
QK_HEAD = QK_NOPE + QK_ROPE
V_HEAD = 128
Q_LORA = 256
KV_LORA = 128
DILATED_GROUPS = ((128, 1), (512, 4), (2048, 16))
N_GROUPS = 3
H_B = 8
HEAD_DIM_B = 128
C_B = H_B * HEAD_DIM_B
D_FF = 4 * D_MODEL
ROPE_THETA = 10000.0
NORM_EPS = 1e-6
NEG_INF = -1e30
LOG2E = 1.4426950408889634

LANES = 128
QK_SLOT = 2 * LANES
VMEM_LIMIT = 56 * 1024 * 1024
BAND = 128
PROJ_TILE = 512
PROJ_CHUNK = 128
SLAB_W = 2 * LANES
SLABS = C_B // SLAB_W

BF16 = jnp.bfloat16
F32 = jnp.float32


def _params(semantics):
    return pltpu.CompilerParams(dimension_semantics=semantics, vmem_limit_bytes=VMEM_LIMIT)


def _rms(x, gain):
    ms = jnp.mean(x * x, axis=-1, keepdims=True)
    return x * lax.rsqrt(ms + NORM_EPS) * gain


def _rot_half(u):
    return pltpu.roll(u, LANES // 2, axis=1)


def _dot(a, b):
    return jnp.dot(a, b, preferred_element_type=F32)


def _dot_nt(a, b):
    return lax.dot_general(a, b, (((1,), (1,)), ((), ())), preferred_element_type=F32)


def _tables_kernel(pos_ref, f_ref, cm_ref, sm_ref, cos_ref, sin_ref):
    ang = pos_ref[...] * f_ref[...]
    cos_ref[...] = jnp.cos(ang) * cm_ref[...]
    sin_ref[...] = jnp.sin(ang) * sm_ref[...]


def _rope_table(pos, freq, cos_mask, sin_sign):
    n = pos.shape[0]
    tm = 1024
    row = pl.BlockSpec((1, LANES), lambda i: (0, 0))
    tab = pl.BlockSpec((tm, LANES), lambda i: (i, 0))
    shp = jax.ShapeDtypeStruct((n, LANES), F32)
    return pl.pallas_call(
        _tables_kernel,
        out_shape=(shp, shp),
        grid=(n // tm,),
        in_specs=[pl.BlockSpec((tm, 1), lambda i: (i, 0)), row, row, row],
        out_specs=(tab, tab),
        compiler_params=_params(("parallel",)),
        name="rope_table",
    )(pos.reshape(n, 1), freq.reshape(1, LANES), cos_mask.reshape(1, LANES), sin_sign.reshape(1, LANES))


def _rope_tables(positions):
    B, S = positions.shape
    pos = positions.astype(F32)
    inv_a = ROPE_THETA ** (-jnp.arange(0, QK_ROPE, 2, dtype=F32) / QK_ROPE)
    inv_b = ROPE_THETA ** (-jnp.arange(0, HEAD_DIM_B, 2, dtype=F32) / HEAD_DIM_B)
    z32 = jnp.zeros((32,), F32)
    o32 = jnp.ones((32,), F32)
    o64 = jnp.ones((64,), F32)
    cos_a, sin_a = _rope_table(pos.reshape(B * S), jnp.concatenate([inv_a, z32, inv_a, z32]),
                               jnp.concatenate([o32, z32, o32, z32]), jnp.concatenate([-o32, z32, o32, z32]))
    cos_b, sin_b = _rope_table(pos.reshape(B * S), jnp.concatenate([inv_b, inv_b]),
                               jnp.concatenate([o64, o64]), jnp.concatenate([-o64, o64]))
    return cos_a, sin_a, cos_b, sin_b


def _mla_proj_kernel(x_ref, g_ref, win_ref, qa_ref, kva_ref, wqb_ref, wkb_ref, wvb_ref, qg_ref, kg_ref,
                     cos_ref, sin_ref, q_ref, k_ref, v_ref, cq_scr, ckv_scr, kpe_scr, kss_scr, q_scr, kn_scr,
                     *, scale):
    tm = x_ref.shape[0]
    n_pairs = H_A // 2
    xn = _rms(x_ref[...], g_ref[...]).astype(BF16)
    lat = _dot(xn, win_ref[...])
    cq_scr[...] = _rms(lat[:, :Q_LORA], qa_ref[...]).astype(BF16)
    ckv_scr[...] = _rms(lat[:, Q_LORA:Q_LORA + KV_LORA], kva_ref[...]).astype(BF16)
    k_pe = lat[:, Q_LORA + KV_LORA:]
    kss_scr[...] = jnp.broadcast_to(jnp.sum(k_pe * k_pe, axis=-1, keepdims=True), (tm, LANES))
    kpe_g = k_pe * kg_ref[:, LANES:]
    kpe_scr[...] = kpe_g * cos_ref[...] + _rot_half(kpe_g) * sin_ref[...]

    def matmuls(pair, buf):
        q_scr[buf] = _dot(cq_scr[...], wqb_ref[pair])
        kn_scr[buf] = _dot(ckv_scr[...], wkb_ref[pair])

    def finish(pair, buf):
        v_ref[pair] = _dot(ckv_scr[...], wvb_ref[pair]).astype(BF16)
        qg_n, qg_pe = qg_ref[:, :LANES], qg_ref[:, LANES:]
        kg_n = kg_ref[:, :LANES]
        for c in range(tm // PROJ_CHUNK):
            rows = slice(c * PROJ_CHUNK, (c + 1) * PROJ_CHUNK)
            cos = cos_ref[rows, :]
            sin = sin_ref[rows, :]
            kpe_ss = kss_scr[rows, :]
            kpe_rot = kpe_scr[rows, :]
            for hh in range(2):
                qn = q_scr[buf, rows, hh * QK_SLOT:hh * QK_SLOT + LANES]
                qp = q_scr[buf, rows, hh * QK_SLOT + LANES:(hh + 1) * QK_SLOT]
                ss = jnp.sum(qn * qn + qp * qp, axis=-1, keepdims=True)
                rs = lax.rsqrt(ss * (1.0 / QK_HEAD) + NORM_EPS) * scale
                qpg = qp * qg_pe
                q_ref[pair, rows, hh * QK_SLOT:hh * QK_SLOT + LANES] = (qn * rs * qg_n).astype(BF16)
                q_ref[pair, rows, hh * QK_SLOT + LANES:(hh + 1) * QK_SLOT] = (
                    (qpg * cos + _rot_half(qpg) * sin) * rs).astype(BF16)
                kn = kn_scr[buf, rows, hh * LANES:(hh + 1) * LANES]
                ssk = jnp.sum(kn * kn, axis=-1, keepdims=True) + kpe_ss
                rsk = lax.rsqrt(ssk * (1.0 / QK_HEAD) + NORM_EPS)
                k_ref[pair, rows, hh * QK_SLOT:hh * QK_SLOT + LANES] = (kn * rsk * kg_n).astype(BF16)
                k_ref[pair, rows, hh * QK_SLOT + LANES:(hh + 1) * QK_SLOT] = (kpe_rot * rsk).astype(BF16)

    matmuls(0, 0)

    def body(jj, carry):
        pair = 2 * jj
        matmuls(pair + 1, 1)
        finish(pair, 0)
        matmuls(jnp.minimum(pair + 2, n_pairs - 1), 0)
        finish(pair + 1, 1)
        return carry

    lax.fori_loop(0, n_pairs // 2, body, 0)


def _rope_tile_cols(a):
    z = jnp.zeros(a.shape[:-1] + (32,), a.dtype)
    return jnp.concatenate([a[..., :32], z, a[..., 32:], z], axis=-1)


def _mla_proj(x2d, gain, w_in, qa_norm, kva_norm, w_qb, w_kvb, q_norm, k_norm, cos_a, sin_a):
    T = x2d.shape[0]
    tm = 256
    w_in_p = jnp.concatenate(
        [w_in[:, :Q_LORA + KV_LORA], _rope_tile_cols(w_in[:, Q_LORA + KV_LORA:])], axis=-1).astype(BF16)
    wq = w_qb.reshape(Q_LORA, H_A, QK_HEAD)
    wq_p = jnp.concatenate([wq[..., :QK_NOPE], _rope_tile_cols(wq[..., QK_NOPE:])], axis=-1)
    n_pairs = H_A // 2

    def pair_slabs(w2d):
        k_dim = w2d.shape[0]
        return w2d.reshape(k_dim, n_pairs, -1).transpose(1, 0, 2).astype(BF16)

    wq_p = pair_slabs(wq_p.reshape(Q_LORA, H_A * QK_SLOT))
    wkv = w_kvb.reshape(KV_LORA, H_A, QK_NOPE + V_HEAD)
    wkb = pair_slabs(wkv[..., :QK_NOPE].reshape(KV_LORA, H_A * QK_NOPE))
    wvb = pair_slabs(wkv[..., QK_NOPE:].reshape(KV_LORA, H_A * V_HEAD))
    qg = jnp.concatenate([q_norm[:QK_NOPE], _rope_tile_cols(q_norm[QK_NOPE:])]).reshape(1, QK_SLOT)
    kg = jnp.concatenate([k_norm[:QK_NOPE], _rope_tile_cols(k_norm[QK_NOPE:])]).reshape(1, QK_SLOT)

    def const(shape):
        return pl.BlockSpec(shape, lambda i: (0,) * len(shape))

    def rows(width):
        return pl.BlockSpec((tm, width), lambda i: (i, 0))

    def pair_rows(width):
        return pl.BlockSpec((n_pairs, tm, width), lambda i: (0, i, 0))

    n_in = Q_LORA + KV_LORA + LANES
    return pl.pallas_call(
        functools.partial(_mla_proj_kernel, scale=QK_HEAD ** -0.5 * LOG2E),
        out_shape=(jax.ShapeDtypeStruct((n_pairs, T, 2 * QK_SLOT), BF16),
                   jax.ShapeDtypeStruct((n_pairs, T, 2 * QK_SLOT), BF16),
                   jax.ShapeDtypeStruct((n_pairs, T, 2 * V_HEAD), BF16)),
        grid=(T // tm,),
        in_specs=[rows(D_MODEL), const((1, D_MODEL)), const((D_MODEL, n_in)), const((1, Q_LORA)),
                  const((1, KV_LORA)), const((n_pairs, Q_LORA, 2 * QK_SLOT)), const((n_pairs, KV_LORA, 2 * QK_NOPE)),
                  const((n_pairs, KV_LORA, 2 * V_HEAD)), const((1, QK_SLOT)), const((1, QK_SLOT)),
                  rows(LANES), rows(LANES)],
        out_specs=(pair_rows(2 * QK_SLOT), pair_rows(2 * QK_SLOT), pair_rows(2 * V_HEAD)),
        scratch_shapes=[pltpu.VMEM((tm, Q_LORA), BF16), pltpu.VMEM((tm, KV_LORA), BF16),
                        pltpu.VMEM((tm, LANES), F32), pltpu.VMEM((tm, LANES), F32),
                        pltpu.VMEM((2, tm, 2 * QK_SLOT), F32), pltpu.VMEM((2, tm, 2 * QK_NOPE), F32)],
        compiler_params=_params(("parallel",)),
        name="mla_proj",
    )(x2d, gain.reshape(1, D_MODEL), w_in_p, qa_norm.reshape(1, Q_LORA), kva_norm.reshape(1, KV_LORA),
      wq_p, wkb, wvb, qg, kg, cos_a, sin_a)


FLASH_HEADS = 8


FLASH_ROWS = 64


def _flash_kernel(q_ref, k_ref, v_ref, o_ref, s_scr, p_scr, m_scr, l_scr, a_scr, acc_scr, *, tq, tk):
    i = pl.program_id(2)
    m_scr[...] = jnp.full(m_scr.shape, NEG_INF, F32)
    l_scr[...] = jnp.zeros(l_scr.shape, F32)
    acc_scr[...] = jnp.zeros(acc_scr.shape, F32)

    def scores(j, buf):
        off = pl.multiple_of(j * tk, tk)
        for h in range(FLASH_HEADS):
            qk_cols = slice((h % 2) * QK_SLOT, (h % 2 + 1) * QK_SLOT)
            s_scr[buf, h] = _dot_nt(q_ref[h // 2, :, qk_cols], k_ref[h // 2, pl.ds(off, tk), qk_cols])

    def softmax_pv(j, buf, masked):
        off = pl.multiple_of(j * tk, tk)
        for h in range(FLASH_HEADS):
            for rb in range(tq // FLASH_ROWS):
                rows = slice(rb * FLASH_ROWS, (rb + 1) * FLASH_ROWS)
                cw = min(tk, -(-((rb + 1) * FLASH_ROWS) // LANES) * LANES) if masked else tk
                s = s_scr[buf, h, rows, :cw]
                if masked:
                    row = lax.broadcasted_iota(jnp.int32, (FLASH_ROWS, cw), 0) + rb * FLASH_ROWS
                    col = lax.broadcasted_iota(jnp.int32, (FLASH_ROWS, cw), 1)
                    s = jnp.where(row >= col, s, NEG_INF)
                m_old = m_scr[h, rows, :]
                m_new = jnp.maximum(m_old, jnp.max(s, axis=-1, keepdims=True))
                p = jnp.exp2(s - jnp.tile(m_new, (1, cw // LANES)))
                alpha = jnp.exp2(m_old - m_new)
                l_scr[h, rows, :] = alpha * l_scr[h, rows, :] + jnp.sum(p, axis=-1, keepdims=True)
                m_scr[h, rows, :] = m_new
                a_scr[h, rows, :] = alpha
                p_scr[h, rows, :cw] = p.astype(BF16)
                if cw < tk:
                    p_scr[h, rows, cw:] = jnp.zeros((FLASH_ROWS, tk - cw), BF16)
            pv = _dot(p_scr[h], v_ref[h // 2, pl.ds(off, tk), (h % 2) * V_HEAD:(h % 2 + 1) * V_HEAD])
            acc_scr[h] = a_scr[h] * acc_scr[h] + pv

    def body(j, carry):
        scores(j, 0)
        softmax_pv(j, 0, False)
        return carry

    assert tq == tk
    lax.fori_loop(0, i, body, 0)
    scores(i, 0)
    softmax_pv(i, 0, True)
    for h in range(FLASH_HEADS):
        o_ref[:, h * V_HEAD:(h + 1) * V_HEAD] = (acc_scr[h] / l_scr[h]).astype(o_ref.dtype)


def _mla_attention(q, k, v, batch, seq):
    tq = tk = 512
    nq = seq // tq
    T = q.shape[1]
    hh = FLASH_HEADS
    assert q.shape[0] * 2 == H_A and hh % 2 == 0
    stat = pltpu.VMEM((hh, tq, LANES), F32)
    return pl.pallas_call(
        functools.partial(_flash_kernel, tq=tq, tk=tk),
        out_shape=jax.ShapeDtypeStruct((T, H_A * V_HEAD), BF16),
        grid=(batch, H_A // hh, nq),
        in_specs=[pl.BlockSpec((hh // 2, tq, 2 * QK_SLOT), lambda b, h, i: (h, b * nq + i, 0)),
                  pl.BlockSpec((hh // 2, seq, 2 * QK_SLOT), lambda b, h, i: (h, b, 0)),
                  pl.BlockSpec((hh // 2, seq, 2 * V_HEAD), lambda b, h, i: (h, b, 0))],
        out_specs=pl.BlockSpec((tq, hh * V_HEAD), lambda b, h, i: (b * nq + i, h)),
        scratch_shapes=[pltpu.VMEM((1, hh, tq, tk), F32), pltpu.VMEM((hh, tq, tk), BF16), stat, stat, stat,
                        pltpu.VMEM((hh, tq, V_HEAD), F32)],
        compiler_params=_params(("parallel", "parallel", "arbitrary")),
        name="mla_flash",
    )(q, k, v)


def _out_proj_kernel(o_ref, w_ref, x_ref, y_ref):
    y_ref[...] = x_ref[...] + _dot(o_ref[...], w_ref[...])


def _out_proj(o, w_o, x2d):
    T, K = o.shape
    tm = 512
    return pl.pallas_call(
        _out_proj_kernel,
        out_shape=jax.ShapeDtypeStruct((T, D_MODEL), F32),
        grid=(T // tm,),
        in_specs=[pl.BlockSpec((tm, K), lambda i: (i, 0)),
                  pl.BlockSpec((K, D_MODEL), lambda i: (0, 0)),
                  pl.BlockSpec((tm, D_MODEL), lambda i: (i, 0))],
        out_specs=pl.BlockSpec((tm, D_MODEL), lambda i: (i, 0)),
        compiler_params=_params(("parallel",)),
        name="out_proj",
    )(o, w_o.astype(BF16), x2d)


def _mlp_kernel(x_ref, g_ref, w1_ref, w2_ref, y_ref, xn_ref, acc_ref):
    f = pl.program_id(1)

    @pl.when(f == 0)
    def _():
        xn_ref[...] = _rms(x_ref[...], g_ref[...]).astype(BF16)
        acc_ref[...] = jnp.zeros_like(acc_ref)

    h = jnp.maximum(_dot(xn_ref[...], w1_ref[...]), 0.0)
    acc_ref[...] += _dot((h * h).astype(BF16), w2_ref[...])

    @pl.when(f == pl.num_programs(1) - 1)
    def _():
        y_ref[...] = x_ref[...] + acc_ref[...]


def _mlp(x2d, gain, w1, w2):
    T = x2d.shape[0]
    tm, tf = 1024, 1024
    return pl.pallas_call(
        _mlp_kernel,
        out_shape=jax.ShapeDtypeStruct((T, D_MODEL), F32),
        grid=(T // tm, D_FF // tf),
        in_specs=[pl.BlockSpec((tm, D_MODEL), lambda i, f: (i, 0)),
                  pl.BlockSpec((1, D_MODEL), lambda i, f: (0, 0)),
                  pl.BlockSpec((D_MODEL, tf), lambda i, f: (0, f)),
                  pl.BlockSpec((tf, D_MODEL), lambda i, f: (f, 0))],
        out_specs=pl.BlockSpec((tm, D_MODEL), lambda i, f: (i, 0)),
        scratch_shapes=[pltpu.VMEM((tm, D_MODEL), BF16), pltpu.VMEM((tm, D_MODEL), F32)],
        compiler_params=_params(("parallel", "arbitrary")),
        name="mlp",
    )(x2d, gain.reshape(1, D_MODEL), w1.astype(BF16), w2.astype(BF16))


def _group_proj_kernel(*refs, has_v, scale):
    if has_v:
        x_ref, g_ref, w_ref, hg_ref, cos_ref, sin_ref = refs[:6]
        outs = refs[6:12]
        xs_ref, xn_ref, y_ref = refs[12:]
    else:
        x_ref, g_ref, w_ref, hg_ref, cos_ref, sin_ref = refs[:6]
        outs = refs[6:9]
        xs_ref, xn_ref, y_ref = refs[9:]
    tm = x_ref.shape[0]
    n_col = D_MODEL // LANES
    n_chunks = tm // PROJ_CHUNK
    assert n_chunks == SLABS
    xn = _rms(x_ref[...], g_ref[...])
    for c in range(n_col):
        xs_ref[c] = xn[:, c * LANES:(c + 1) * LANES]

    def permute(gi, buf):
        d = DILATED_GROUPS[gi][1]
        rows = tm // d
        if d == 1:
            xn_ref[buf] = xn.astype(BF16)
        else:
            for r in range(d):
                for c in range(n_col):
                    xn_ref[buf, r * rows:(r + 1) * rows, c * LANES:(c + 1) * LANES] = (
                        xs_ref[c, pl.ds(r, rows, stride=d), :].astype(BF16))

    def norm_chunk(gi, buf, ci):
        d = DILATED_GROUPS[gi][1]
        rows = tm // d
        out_ref = outs[gi]
        r0 = pl.multiple_of(ci * PROJ_CHUNK, PROJ_CHUNK)

        def table_rows(t_ref):
            if d == 1:
                return t_ref[pl.ds(r0, PROJ_CHUNK), :]
            if rows >= PROJ_CHUNK:
                per = rows // PROJ_CHUNK
                return t_ref[pl.ds((ci % per) * PROJ_CHUNK * d + ci // per, PROJ_CHUNK, stride=d), :]
            per = PROJ_CHUNK // rows
            return jnp.concatenate([t_ref[pl.ds(ci * per + s, rows, stride=d), :] for s in range(per)], axis=0)

        cos = table_rows(cos_ref)
        sin = table_rows(sin_ref)
        for h in range(H_B):
            hs = slice(h * HEAD_DIM_B, (h + 1) * HEAD_DIM_B)
            sl = h * HEAD_DIM_B // SLAB_W
            lo = h * HEAD_DIM_B % SLAB_W
            yh = y_ref[buf, sl, pl.ds(r0, PROJ_CHUNK), lo:lo + HEAD_DIM_B]
            rs = lax.rsqrt(jnp.mean(yh * yh, axis=-1, keepdims=True) + NORM_EPS) * scale
            yg = yh * hg_ref[:, gi * C_B + h * HEAD_DIM_B:gi * C_B + (h + 1) * HEAD_DIM_B]
            res = ((yg * cos + _rot_half(yg) * sin) * rs).astype(BF16)
            if rows >= PROJ_CHUNK:
                per = rows // PROJ_CHUNK
                out_ref[0, ci // per, pl.ds(pl.multiple_of((ci % per) * PROJ_CHUNK, PROJ_CHUNK), PROJ_CHUNK),
                        hs] = res
            else:
                per = PROJ_CHUNK // rows
                for s in range(per):
                    out_ref[0, ci * per + s, :, hs] = res[s * rows:(s + 1) * rows]

    def v_slab(gi, buf, ci):
        d = DILATED_GROUPS[gi][1]
        rows = tm // d
        yv = _dot(xn_ref[buf], w_ref[(N_GROUPS + gi) * SLABS + ci])
        for r in range(d):
            outs[N_GROUPS + gi][0, r, ci] = yv[r * rows:(r + 1) * rows].astype(BF16)

    permute(0, 0)
    for s in range(SLABS):
        y_ref[0, s] = _dot(xn_ref[0], w_ref[s])
    for gi in range(N_GROUPS):
        buf = gi % 2
        if gi + 1 < N_GROUPS:
            permute(gi + 1, 1 - buf)

        def body(ci, carry, gi=gi, buf=buf):
            norm_chunk(gi, buf, ci)
            if gi + 1 < N_GROUPS:
                y_ref[1 - buf, ci] = _dot(xn_ref[1 - buf], w_ref[(gi + 1) * SLABS + ci])
            if has_v:
                v_slab(gi, buf, ci)
            return carry

        lax.fori_loop(0, n_chunks, body, 0, unroll=2 if has_v else 1)


def _group_proj(x2d, gain, w, head_gain, has_v, scale, cos_b, sin_b, batch, seq):
    tm = PROJ_TILE
    nt = seq // tm
    n_rope = N_GROUPS * C_B
    n_slabs = w.shape[1] // SLAB_W
    w_slabs = w.astype(BF16).reshape(D_MODEL, n_slabs, SLAB_W).transpose(1, 0, 2)
    in_specs = [pl.BlockSpec((tm, D_MODEL), lambda i: (i, 0)),
                pl.BlockSpec((1, D_MODEL), lambda i: (0, 0)),
                pl.BlockSpec((n_slabs, D_MODEL, SLAB_W), lambda i: (0, 0, 0), pipeline_mode=pl.Buffered(1)),
                pl.BlockSpec((1, n_rope), lambda i: (0, 0)),
                pl.BlockSpec((tm, LANES), lambda i: (i, 0)),
                pl.BlockSpec((tm, LANES), lambda i: (i, 0))]
    shapes, specs = [], []
    for _, d in DILATED_GROUPS:
        shapes.append(jax.ShapeDtypeStruct((batch, d, seq // d, C_B), BF16))
        specs.append(pl.BlockSpec((1, d, tm // d, C_B), lambda i: (i // nt, 0, i % nt, 0)))
    if has_v:
        for _, d in DILATED_GROUPS:
            shapes.append(jax.ShapeDtypeStruct((batch, d, SLABS, seq // d, SLAB_W), BF16))
            specs.append(pl.BlockSpec((1, d, SLABS, tm // d, SLAB_W), lambda i: (i // nt, 0, 0, i % nt, 0)))
    return pl.pallas_call(
        functools.partial(_group_proj_kernel, has_v=has_v, scale=scale),
        out_shape=tuple(shapes),
        grid=(batch * nt,),
        in_specs=in_specs,
        out_specs=tuple(specs),
        scratch_shapes=[pltpu.VMEM((D_MODEL // LANES, tm, LANES), F32), pltpu.VMEM((2, tm, D_MODEL), BF16),
                        pltpu.VMEM((2, SLABS, tm, SLAB_W), F32)],
        compiler_params=_params(("parallel",)),
        name="group_proj_kv" if has_v else "group_proj_q",
    )(x2d, gain.reshape(1, D_MODEL), w_slabs, head_gain.reshape(1, n_rope), cos_b, sin_b)


BAND_HEADS = 8


def _band_kernel(q_ref, k_ref, v_ref, o_ref, lse_ref, *, dilation, length):
    nk = min(2 * BAND, length)
    row = lax.broadcasted_iota(jnp.int32, (BAND, nk), 0)
    col = lax.broadcasted_iota(jnp.int32, (BAND, nk), 1)
    lane_head = lax.broadcasted_iota(jnp.int32, (BAND, LANES), 1) // (LANES // BAND_HEADS)

    def block(r, i):
        q0 = i * BAND
        start = jnp.maximum(q0 + BAND - nk, 0)
        dist = (q0 - start) + row - col
        valid = jnp.logical_and(dist >= 0, dist <= BAND)
        if not isinstance(q0, int):
            q0 = pl.multiple_of(q0, BAND)
            start = pl.multiple_of(start, BAND)
        scores = []
        for h in range(BAND_HEADS):
            hs = slice(h * HEAD_DIM_B, (h + 1) * HEAD_DIM_B)
            s = _dot_nt(q_ref[0, r, pl.ds(q0, BAND), hs], k_ref[0, r, pl.ds(start, nk), hs])
            scores.append(jnp.where(valid, s, NEG_INF))
        probs, denoms, lse_tile = [], [], jnp.zeros((BAND, LANES), F32)
        for h in range(BAND_HEADS):
            m = jnp.max(scores[h], axis=-1, keepdims=True)
            p = jnp.exp2(scores[h] - m)
            denom = jnp.sum(p, axis=-1, keepdims=True)
            probs.append(p.astype(BF16))
            denoms.append(denom)
            lse_tile = jnp.where(lane_head == h, m + jnp.log2(denom), lse_tile)
        tok = pl.ds(q0 * dilation + r, BAND, stride=dilation) if dilation > 1 else pl.ds(q0, BAND)
        for h in range(BAND_HEADS):
            hs = slice(h * HEAD_DIM_B, (h + 1) * HEAD_DIM_B)
            lo = h * HEAD_DIM_B % SLAB_W
            out = _dot(probs[h], v_ref[0, r, h * HEAD_DIM_B // SLAB_W, pl.ds(start, nk), lo:lo + HEAD_DIM_B])
            o_ref[0, h, tok, :] = out / denoms[h]
        lse_ref[0, 0, tok, :] = lse_tile

    nb = length // BAND
    for r in range(dilation):
        if nb == 1:
            block(r, 0)
        else:
            def body(i, carry, r=r):
                block(r, i)
                return carry
            lax.fori_loop(0, nb, body, 0, unroll=4)


def _band_attention(q, k, v, dilation, batch, seq):
    length = seq // dilation
    halves = H_B // BAND_HEADS
    cw = BAND_HEADS * HEAD_DIM_B
    blk = pl.BlockSpec((1, dilation, length, cw), lambda b, hh: (b, 0, 0, hh))
    v_blk = pl.BlockSpec((1, dilation, cw // SLAB_W, length, SLAB_W), lambda b, hh: (b, 0, hh, 0, 0))
    return pl.pallas_call(
        functools.partial(_band_kernel, dilation=dilation, length=length),
        out_shape=(jax.ShapeDtypeStruct((batch, H_B, seq, HEAD_DIM_B), F32),
                   jax.ShapeDtypeStruct((batch, halves, seq, LANES), F32)),
        grid=(batch, halves),
        in_specs=[blk, blk, v_blk],
        out_specs=(pl.BlockSpec((1, BAND_HEADS, seq, HEAD_DIM_B), lambda b, hh: (b, hh, 0, 0)),
                   pl.BlockSpec((1, 1, seq, LANES), lambda b, hh: (b, hh, 0, 0))),
        compiler_params=_params(("parallel", "parallel")),
        name=f"band_attn_d{dilation}",
    )(q, k, v)


def _combine_proj_kernel(o0_ref, o1_ref, o2_ref, l0_ref, l1_ref, l2_ref, w_ref, x_ref, y_ref, o_scr):
    lanes_per_head = LANES // BAND_HEADS
    for half in range(H_B // BAND_HEADS):
        l0, l1, l2 = l0_ref[0, half], l1_ref[0, half], l2_ref[0, half]
        mx = jnp.maximum(jnp.maximum(l0, l1), l2)
        e0, e1, e2 = jnp.exp2(l0 - mx), jnp.exp2(l1 - mx), jnp.exp2(l2 - mx)
        inv = 1.0 / (e0 + e1 + e2)
        w0, w1, w2 = e0 * inv, e1 * inv, e2 * inv
        for hq in range(BAND_HEADS):
            h = half * BAND_HEADS + hq
            c = hq * lanes_per_head
            o = (w0[:, c:c + 1] * o0_ref[0, h] + w1[:, c:c + 1] * o1_ref[0, h] + w2[:, c:c + 1] * o2_ref[0, h])
            o_scr[:, h * HEAD_DIM_B:(h + 1) * HEAD_DIM_B] = o.astype(BF16)
    y_ref[...] = x_ref[...] + _dot(o_scr[...], w_ref[...])


def _combine_proj(outs, lses, w_o, x2d, seq):
    T = x2d.shape[0]
    tm = 512
    nt = seq // tm
    halves = H_B // BAND_HEADS
    o_spec = pl.BlockSpec((1, H_B, tm, HEAD_DIM_B), lambda i: (i // nt, 0, i % nt, 0))
    l_spec = pl.BlockSpec((1, halves, tm, LANES), lambda i: (i // nt, 0, i % nt, 0))
    return pl.pallas_call(
        _combine_proj_kernel,
        out_shape=jax.ShapeDtypeStruct((T, D_MODEL), F32),
        grid=(T // tm,),
        in_specs=[o_spec, o_spec, o_spec, l_spec, l_spec, l_spec,
                  pl.BlockSpec((C_B, D_MODEL), lambda i: (0, 0)),
                  pl.BlockSpec((tm, D_MODEL), lambda i: (i, 0))],
        out_specs=pl.BlockSpec((tm, D_MODEL), lambda i: (i, 0)),
        scratch_shapes=[pltpu.VMEM((tm, C_B), BF16)],
        compiler_params=_params(("parallel",)),
        name="combine_proj",
    )(*outs, *lses, w_o.astype(BF16), x2d)


def kernel(x, positions, attn_norm, mlp_norm, mla_w_in, mla_qa_norm, mla_kva_norm, mla_w_qb, mla_w_kvb,
           mla_q_norm, mla_k_norm, mla_w_o, kv_norm, w_kv, k_norm_b, w_q_b, q_norm_b, w_o_b, mlp_w1, mlp_w2):
    B, S, D = x.shape
    T = B * S
    cos_a, sin_a, cos_b, sin_b = _rope_tables(positions)
    h = x.reshape(T, D)

    def head_gains(gn):
        return jnp.broadcast_to(gn[:, None, :], (N_GROUPS, H_B, HEAD_DIM_B))

    for a in range(N_A_LAYERS):
        q, k, v = _mla_proj(h, attn_norm[a], mla_w_in[a], mla_qa_norm[a], mla_kva_norm[a], mla_w_qb[a],
                            mla_w_kvb[a], mla_q_norm[a], mla_k_norm[a], cos_a, sin_a)
        o = _mla_attention(q, k, v, B, S)
        h = _out_proj(o, mla_w_o[a], h)
        h = _mlp(h, mlp_norm[a], mlp_w1[a], mlp_w2[a])

    kv = _group_proj(h, kv_norm, w_kv, head_gains(k_norm_b), True, 1.0, cos_b, sin_b, B, S)
    ks, vs = kv[:N_GROUPS], kv[N_GROUPS:]

    for b in range(N_B_LAYERS):
        layer = N_A_LAYERS + b
        qs = _group_proj(h, attn_norm[layer], w_q_b[b], head_gains(q_norm_b[b]), False,
                         HEAD_DIM_B ** -0.5 * LOG2E, cos_b, sin_b, B, S)
        outs, lses = [], []
        for g, (window, dilation) in enumerate(DILATED_GROUPS):
            assert window // dilation == BAND
            o, lse = _band_attention(qs[g], ks[g], vs[g], dilation, B, S)
            outs.append(o)
            lses.append(lse)
        h = _combine_proj(outs, lses, w_o_b[b], h, S)
        h = _mlp(h, mlp_norm[layer], mlp_w1[layer], mlp_w2[layer])

    return h.reshape(B, S, D)
```

```python
import functools

import jax
import jax.numpy as jnp
from jax import lax
from jax.experimental import pallas as pl
from jax.experimental.pallas import tpu as pltpu

D_MODEL = 1024
N_A_LAYERS = 2
N_B_LAYERS = 2
H_A = 16
QK_NOPE = 128
QK_ROPE = 64
QK_HEAD = QK_NOPE + QK_ROPE
V_HEAD = 128
Q_LORA = 256
KV_LORA = 128
DILATED_GROUPS = ((128, 1), (512, 4), (2048, 16))
N_GROUPS = 3
H_B = 8
HEAD_DIM_B = 128
C_B = H_B * HEAD_DIM_B
D_FF = 4 * D_MODEL
ROPE_THETA = 10000.0
NORM_EPS = 1e-6
NEG_INF = -1e30
LOG2E = 1.4426950408889634

LANES = 128
QK_SLOT = 2 * LANES
VMEM_LIMIT = 56 * 1024 * 1024
BAND = 128
PROJ_TILE = 512
PROJ_CHUNK = 128
SLAB_W = 2 * LANES
SLABS = C_B // SLAB_W

BF16 = jnp.bfloat16
F32 = jnp.float32


def _params(semantics):
    return pltpu.CompilerParams(dimension_semantics=semantics, vmem_limit_bytes=VMEM_LIMIT)


def _rms(x, gain):
    ms = jnp.mean(x * x, axis=-1, keepdims=True)
    return x * lax.rsqrt(ms + NORM_EPS) * gain


def _rot_half(u):
    return pltpu.roll(u, LANES // 2, axis=1)


def _dot(a, b):
    return jnp.dot(a, b, preferred_element_type=F32)


def _dot_nt(a, b):
    return lax.dot_general(a, b, (((1,), (1,)), ((), ())), preferred_element_type=F32)


def _tables_kernel(pos_ref, f_ref, cm_ref, sm_ref, cos_ref, sin_ref):
    ang = pos_ref[...] * f_ref[...]
    cos_ref[...] = jnp.cos(ang) * cm_ref[...]
    sin_ref[...] = jnp.sin(ang) * sm_ref[...]


def _rope_table(pos, freq, cos_mask, sin_sign):
    n = pos.shape[0]
    tm = 1024
    row = pl.BlockSpec((1, LANES), lambda i: (0, 0))
    tab = pl.BlockSpec((tm, LANES), lambda i: (i, 0))
    shp = jax.ShapeDtypeStruct((n, LANES), F32)
    return pl.pallas_call(
        _tables_kernel,
        out_shape=(shp, shp),
        grid=(n // tm,),
        in_specs=[pl.BlockSpec((tm, 1), lambda i: (i, 0)), row, row, row],
        out_specs=(tab, tab),
        compiler_params=_params(("parallel",)),
        name="rope_table",
    )(pos.reshape(n, 1), freq.reshape(1, LANES), cos_mask.reshape(1, LANES), sin_sign.reshape(1, LANES))


def _rope_tables(positions):
    B, S = positions.shape
    pos = positions.astype(F32)
    inv_a = ROPE_THETA ** (-jnp.arange(0, QK_ROPE, 2, dtype=F32) / QK_ROPE)
    inv_b = ROPE_THETA ** (-jnp.arange(0, HEAD_DIM_B, 2, dtype=F32) / HEAD_DIM_B)
    z32 = jnp.zeros((32,), F32)
    o32 = jnp.ones((32,), F32)
    o64 = jnp.ones((64,), F32)
    cos_a, sin_a = _rope_table(pos.reshape(B * S), jnp.concatenate([inv_a, z32, inv_a, z32]),
                               jnp.concatenate([o32, z32, o32, z32]), jnp.concatenate([-o32, z32, o32, z32]))
    cos_b, sin_b = _rope_table(pos.reshape(B * S), jnp.concatenate([inv_b, inv_b]),
                               jnp.concatenate([o64, o64]), jnp.concatenate([-o64, o64]))
    return cos_a, sin_a, cos_b, sin_b


def _mla_proj_kernel(x_ref, g_ref, win_ref, qa_ref, kva_ref, wqb_ref, wkb_ref, wvb_ref, qg_ref, kg_ref,
                     cos_ref, sin_ref, q_ref, k_ref, v_ref, cq_scr, ckv_scr, kpe_scr, kss_scr, q_scr, kn_scr,
                     *, scale):
    tm = x_ref.shape[0]
    n_pairs = H_A // 2
    xn = _rms(x_ref[...], g_ref[...]).astype(BF16)
    lat = _dot(xn, win_ref[...])
    cq_scr[...] = _rms(lat[:, :Q_LORA], qa_ref[...]).astype(BF16)
    ckv_scr[...] = _rms(lat[:, Q_LORA:Q_LORA + KV_LORA], kva_ref[...]).astype(BF16)
    k_pe = lat[:, Q_LORA + KV_LORA:]
    kss_scr[...] = jnp.broadcast_to(jnp.sum(k_pe * k_pe, axis=-1, keepdims=True), (tm, LANES))
    kpe_g = k_pe * kg_ref[:, LANES:]
    kpe_scr[...] = kpe_g * cos_ref[...] + _rot_half(kpe_g) * sin_ref[...]

    def matmuls(pair, buf):
        q_scr[buf] = _dot(cq_scr[...], wqb_ref[pair])
        kn_scr[buf] = _dot(ckv_scr[...], wkb_ref[pair])

    def finish(pair, buf):
        v_ref[pair] = _dot(ckv_scr[...], wvb_ref[pair]).astype(BF16)
        qg_n, qg_pe = qg_ref[:, :LANES], qg_ref[:, LANES:]
        kg_n = kg_ref[:, :LANES]
        for c in range(tm // PROJ_CHUNK):
            rows = slice(c * PROJ_CHUNK, (c + 1) * PROJ_CHUNK)
            cos = cos_ref[rows, :]
            sin = sin_ref[rows, :]
            kpe_ss = kss_scr[rows, :]
            kpe_rot = kpe_scr[rows, :]
            for hh in range(2):
                qn = q_scr[buf, rows, hh * QK_SLOT:hh * QK_SLOT + LANES]
                qp = q_scr[buf, rows, hh * QK_SLOT + LANES:(hh + 1) * QK_SLOT]
                ss = jnp.sum(qn * qn + qp * qp, axis=-1, keepdims=True)
                rs = lax.rsqrt(ss * (1.0 / QK_HEAD) + NORM_EPS) * scale
                qpg = qp * qg_pe
                q_ref[pair, rows, hh * QK_SLOT:hh * QK_SLOT + LANES] = (qn * rs * qg_n).astype(BF16)
                q_ref[pair, rows, hh * QK_SLOT + LANES:(hh + 1) * QK_SLOT] = (
                    (qpg * cos + _rot_half(qpg) * sin) * rs).astype(BF16)
                kn = kn_scr[buf, rows, hh * LANES:(hh + 1) * LANES]
                ssk = jnp.sum(kn * kn, axis=-1, keepdims=True) + kpe_ss
                rsk = lax.rsqrt(ssk * (1.0 / QK_HEAD) + NORM_EPS)
                k_ref[pair, rows, hh * QK_SLOT:hh * QK_SLOT + LANES] = (kn * rsk * kg_n).astype(BF16)
                k_ref[pair, rows, hh * QK_SLOT + LANES:(hh + 1) * QK_SLOT] = (kpe_rot * rsk).astype(BF16)

    matmuls(0, 0)

    def body(jj, carry):
        pair = 2 * jj
        matmuls(pair + 1, 1)
        finish(pair, 0)
        matmuls(jnp.minimum(pair + 2, n_pairs - 1), 0)
        finish(pair + 1, 1)
        return carry

    lax.fori_loop(0, n_pairs // 2, body, 0)


def _rope_tile_cols(a):
    z = jnp.zeros(a.shape[:-1] + (32,), a.dtype)
    return jnp.concatenate([a[..., :32], z, a[..., 32:], z], axis=-1)


def _mla_proj(x2d, gain, w_in, qa_norm, kva_norm, w_qb, w_kvb, q_norm, k_norm, cos_a, sin_a):
    T = x2d.shape[0]
    tm = 256
    w_in_p = jnp.concatenate(
        [w_in[:, :Q_LORA + KV_LORA], _rope_tile_cols(w_in[:, Q_LORA + KV_LORA:])], axis=-1).astype(BF16)
    wq = w_qb.reshape(Q_LORA, H_A, QK_HEAD)
    wq_p = jnp.concatenate([wq[..., :QK_NOPE], _rope_tile_cols(wq[..., QK_NOPE:])], axis=-1)
    n_pairs = H_A // 2

    def pair_slabs(w2d):
        k_dim = w2d.shape[0]
        return w2d.reshape(k_dim, n_pairs, -1).transpose(1, 0, 2).astype(BF16)

    wq_p = pair_slabs(wq_p.reshape(Q_LORA, H_A * QK_SLOT))
    wkv = w_kvb.reshape(KV_LORA, H_A, QK_NOPE + V_HEAD)
    wkb = pair_slabs(wkv[..., :QK_NOPE].reshape(KV_LORA, H_A * QK_NOPE))
    wvb = pair_slabs(wkv[..., QK_NOPE:].reshape(KV_LORA, H_A * V_HEAD))
    qg = jnp.concatenate([q_norm[:QK_NOPE], _rope_tile_cols(q_norm[QK_NOPE:])]).reshape(1, QK_SLOT)
    kg = jnp.concatenate([k_norm[:QK_NOPE], _rope_tile_cols(k_norm[QK_NOPE:])]).reshape(1, QK_SLOT)

    def const(shape):
        return pl.BlockSpec(shape, lambda i: (0,) * len(shape))

    def rows(width):
        return pl.BlockSpec((tm, width), lambda i: (i, 0))

    def pair_rows(width):
        return pl.BlockSpec((n_pairs, tm, width), lambda i: (0, i, 0))

    n_in = Q_LORA + KV_LORA + LANES
    return pl.pallas_call(
        functools.partial(_mla_proj_kernel, scale=QK_HEAD ** -0.5 * LOG2E),
        out_shape=(jax.ShapeDtypeStruct((n_pairs, T, 2 * QK_SLOT), BF16),
                   jax.ShapeDtypeStruct((n_pairs, T, 2 * QK_SLOT), BF16),
                   jax.ShapeDtypeStruct((n_pairs, T, 2 * V_HEAD), BF16)),
        grid=(T // tm,),
        in_specs=[rows(D_MODEL), const((1, D_MODEL)), const((D_MODEL, n_in)), const((1, Q_LORA)),
                  const((1, KV_LORA)), const((n_pairs, Q_LORA, 2 * QK_SLOT)), const((n_pairs, KV_LORA, 2 * QK_NOPE)),
                  const((n_pairs, KV_LORA, 2 * V_HEAD)), const((1, QK_SLOT)), const((1, QK_SLOT)),
                  rows(LANES), rows(LANES)],
        out_specs=(pair_rows(2 * QK_SLOT), pair_rows(2 * QK_SLOT), pair_rows(2 * V_HEAD)),
        scratch_shapes=[pltpu.VMEM((tm, Q_LORA), BF16), pltpu.VMEM((tm, KV_LORA), BF16),
                        pltpu.VMEM((tm, LANES), F32), pltpu.VMEM((tm, LANES), F32),
                        pltpu.VMEM((2, tm, 2 * QK_SLOT), F32), pltpu.VMEM((2, tm, 2 * QK_NOPE), F32)],
        compiler_params=_params(("parallel",)),
        name="mla_proj",
    )(x2d, gain.reshape(1, D_MODEL), w_in_p, qa_norm.reshape(1, Q_LORA), kva_norm.reshape(1, KV_LORA),
      wq_p, wkb, wvb, qg, kg, cos_a, sin_a)


FLASH_HEADS = 8


FLASH_ROWS = 64


def _flash_kernel(q_ref, k_ref, v_ref, o_ref, s_scr, p_scr, m_scr, l_scr, a_scr, acc_scr, *, tq, tk):
    i = pl.program_id(2)
    m_scr[...] = jnp.full(m_scr.shape, NEG_INF, F32)
    l_scr[...] = jnp.zeros(l_scr.shape, F32)
    acc_scr[...] = jnp.zeros(acc_scr.shape, F32)

    half = tq // 2

    def scores(j, buf, masked):
        off = pl.multiple_of(j * tk, tk)
        for h in range(FLASH_HEADS):
            qk_cols = slice((h % 2) * QK_SLOT, (h % 2 + 1) * QK_SLOT)
            if masked:
                s_scr[buf, h, :half, :half] = _dot_nt(q_ref[h // 2, :half, qk_cols],
                                                      k_ref[h // 2, pl.ds(off, half), qk_cols])
                s_scr[buf, h, half:, :] = _dot_nt(q_ref[h // 2, half:, qk_cols],
                                                  k_ref[h // 2, pl.ds(off, tk), qk_cols])
            else:
                s_scr[buf, h] = _dot_nt(q_ref[h // 2, :, qk_cols], k_ref[h // 2, pl.ds(off, tk), qk_cols])

    def softmax_pv(j, buf, masked):
        off = pl.multiple_of(j * tk, tk)
        for h in range(FLASH_HEADS):
            v_cols = slice((h % 2) * V_HEAD, (h % 2 + 1) * V_HEAD)
            for rb in range(tq // FLASH_ROWS):
                rows = slice(rb * FLASH_ROWS, (rb + 1) * FLASH_ROWS)
                cw = min(tk, -(-((rb + 1) * FLASH_ROWS) // LANES) * LANES) if masked else tk
                pw = (half if (rb + 1) * FLASH_ROWS <= half else tk) if masked else tk
                s = s_scr[buf, h, rows, :cw]
                if masked:
                    row = lax.broadcasted_iota(jnp.int32, (FLASH_ROWS, cw), 0) + rb * FLASH_ROWS
                    col = lax.broadcasted_iota(jnp.int32, (FLASH_ROWS, cw), 1)
                    s = jnp.where(row >= col, s, NEG_INF)
                m_old = m_scr[h, rows, :]
                m_new = jnp.maximum(m_old, jnp.max(s, axis=-1, keepdims=True))
                p = jnp.exp2(s - jnp.tile(m_new, (1, cw // LANES)))
                alpha = jnp.exp2(m_old - m_new)
                l_scr[h, rows, :] = alpha * l_scr[h, rows, :] + jnp.sum(p, axis=-1, keepdims=True)
                m_scr[h, rows, :] = m_new
                a_scr[h, rows, :] = alpha
                p_scr[h, rows, :cw] = p.astype(BF16)
                if cw < pw:
                    p_scr[h, rows, cw:pw] = jnp.zeros((FLASH_ROWS, pw - cw), BF16)
            if masked:
                pv = _dot(p_scr[h, :half, :half], v_ref[h // 2, pl.ds(off, half), v_cols])
                acc_scr[h, :half] = a_scr[h, :half] * acc_scr[h, :half] + pv
                pv = _dot(p_scr[h, half:, :], v_ref[h // 2, pl.ds(off, tk), v_cols])
                acc_scr[h, half:] = a_scr[h, half:] * acc_scr[h, half:] + pv
            else:
                pv = _dot(p_scr[h], v_ref[h // 2, pl.ds(off, tk), v_cols])
                acc_scr[h] = a_scr[h] * acc_scr[h] + pv

    def body(j, carry):
        scores(j, 0, False)
        softmax_pv(j, 0, False)
        return carry

    assert tq == tk
    lax.fori_loop(0, i, body, 0)
    scores(i, 0, True)
    softmax_pv(i, 0, True)
    for h in range(FLASH_HEADS):
        o_ref[:, h * V_HEAD:(h + 1) * V_HEAD] = (acc_scr[h] / l_scr[h]).astype(o_ref.dtype)


def _mla_attention(q, k, v, batch, seq):
    tq = tk = 512
    nq = seq // tq
    T = q.shape[1]
    hh = FLASH_HEADS
    assert q.shape[0] * 2 == H_A and hh % 2 == 0
    stat = pltpu.VMEM((hh, tq, LANES), F32)
    return pl.pallas_call(
        functools.partial(_flash_kernel, tq=tq, tk=tk),
        out_shape=jax.ShapeDtypeStruct((T, H_A * V_HEAD), BF16),
        grid=(batch, H_A // hh, nq),
        in_specs=[pl.BlockSpec((hh // 2, tq, 2 * QK_SLOT), lambda b, h, i: (h, b * nq + i, 0)),
                  pl.BlockSpec((hh // 2, seq, 2 * QK_SLOT), lambda b, h, i: (h, b, 0)),
                  pl.BlockSpec((hh // 2, seq, 2 * V_HEAD), lambda b, h, i: (h, b, 0))],
        out_specs=pl.BlockSpec((tq, hh * V_HEAD), lambda b, h, i: (b * nq + i, h)),
        scratch_shapes=[pltpu.VMEM((1, hh, tq, tk), F32), pltpu.VMEM((hh, tq, tk), BF16), stat, stat, stat,
                        pltpu.VMEM((hh, tq, V_HEAD), F32)],
        compiler_params=_params(("parallel", "parallel", "arbitrary")),
        name="mla_flash",
    )(q, k, v)


def _out_proj_kernel(o_ref, w_ref, x_ref, y_ref):
    y_ref[...] = x_ref[...] + _dot(o_ref[...], w_ref[...])


def _out_proj(o, w_o, x2d):
    T, K = o.shape
    tm = 512
    return pl.pallas_call(
        _out_proj_kernel,
        out_shape=jax.ShapeDtypeStruct((T, D_MODEL), F32),
        grid=(T // tm,),
        in_specs=[pl.BlockSpec((tm, K), lambda i: (i, 0)),
                  pl.BlockSpec((K, D_MODEL), lambda i: (0, 0)),
                  pl.BlockSpec((tm, D_MODEL), lambda i: (i, 0))],
        out_specs=pl.BlockSpec((tm, D_MODEL), lambda i: (i, 0)),
        compiler_params=_params(("parallel",)),
        name="out_proj",
    )(o, w_o.astype(BF16), x2d)


def _mlp_kernel(x_ref, g_ref, w1_ref, w2_ref, y_ref, xn_ref, acc_ref):
    f = pl.program_id(1)

    @pl.when(f == 0)
    def _():
        xn_ref[...] = _rms(x_ref[...], g_ref[...]).astype(BF16)
        acc_ref[...] = jnp.zeros_like(acc_ref)

    h = jnp.maximum(_dot(xn_ref[...], w1_ref[...]), 0.0)
    acc_ref[...] += _dot((h * h).astype(BF16), w2_ref[...])

    @pl.when(f == pl.num_programs(1) - 1)
    def _():
        y_ref[...] = x_ref[...] + acc_ref[...]


def _mlp(x2d, gain, w1, w2):
    T = x2d.shape[0]
    tm, tf = 1024, 1024
    return pl.pallas_call(
        _mlp_kernel,
        out_shape=jax.ShapeDtypeStruct((T, D_MODEL), F32),
        grid=(T // tm, D_FF // tf),
        in_specs=[pl.BlockSpec((tm, D_MODEL), lambda i, f: (i, 0)),
                  pl.BlockSpec((1, D_MODEL), lambda i, f: (0, 0)),
                  pl.BlockSpec((D_MODEL, tf), lambda i, f: (0, f)),
                  pl.BlockSpec((tf, D_MODEL), lambda i, f: (f, 0))],
        out_specs=pl.BlockSpec((tm, D_MODEL), lambda i, f: (i, 0)),
        scratch_shapes=[pltpu.VMEM((tm, D_MODEL), BF16), pltpu.VMEM((tm, D_MODEL), F32)],
        compiler_params=_params(("parallel", "arbitrary")),
        name="mlp",
    )(x2d, gain.reshape(1, D_MODEL), w1.astype(BF16), w2.astype(BF16))


def _group_proj_kernel(*refs, has_v, scale):
    if has_v:
        x_ref, g_ref, w_ref, hg_ref, cos_ref, sin_ref = refs[:6]
        outs = refs[6:12]
        xs_ref, xn_ref, y_ref = refs[12:]
    else:
        x_ref, g_ref, w_ref, hg_ref, cos_ref, sin_ref = refs[:6]
        outs = refs[6:9]
        xs_ref, xn_ref, y_ref = refs[9:]
    tm = x_ref.shape[0]
    n_col = D_MODEL // LANES
    n_chunks = tm // PROJ_CHUNK
    assert n_chunks == SLABS
    xn = _rms(x_ref[...], g_ref[...])
    for c in range(n_col):
        xs_ref[c] = xn[:, c * LANES:(c + 1) * LANES]

    def permute(gi, buf):
        d = DILATED_GROUPS[gi][1]
        rows = tm // d
        if d == 1:
            xn_ref[buf] = xn.astype(BF16)
        else:
            for r in range(d):
                for c in range(n_col):
                    xn_ref[buf, r * rows:(r + 1) * rows, c * LANES:(c + 1) * LANES] = (
                        xs_ref[c, pl.ds(r, rows, stride=d), :].astype(BF16))

    def norm_chunk(gi, buf, ci):
        d = DILATED_GROUPS[gi][1]
        rows = tm // d
        out_ref = outs[gi]
        r0 = pl.multiple_of(ci * PROJ_CHUNK, PROJ_CHUNK)

        def table_rows(t_ref):
            if d == 1:
                return t_ref[pl.ds(r0, PROJ_CHUNK), :]
            if rows >= PROJ_CHUNK:
                per = rows // PROJ_CHUNK
                return t_ref[pl.ds((ci % per) * PROJ_CHUNK * d + ci // per, PROJ_CHUNK, stride=d), :]
            per = PROJ_CHUNK // rows
            return jnp.concatenate([t_ref[pl.ds(ci * per + s, rows, stride=d), :] for s in range(per)], axis=0)

        cos = table_rows(cos_ref)
        sin = table_rows(sin_ref)
        for h in range(H_B):
            hs = slice(h * HEAD_DIM_B, (h + 1) * HEAD_DIM_B)
            sl = h * HEAD_DIM_B // SLAB_W
            lo = h * HEAD_DIM_B % SLAB_W
            yh = y_ref[buf, sl, pl.ds(r0, PROJ_CHUNK), lo:lo + HEAD_DIM_B]
            rs = lax.rsqrt(jnp.mean(yh * yh, axis=-1, keepdims=True) + NORM_EPS) * scale
            yg = yh * hg_ref[:, gi * C_B + h * HEAD_DIM_B:gi * C_B + (h + 1) * HEAD_DIM_B]
            res = ((yg * cos + _rot_half(yg) * sin) * rs).astype(BF16)
            if rows >= PROJ_CHUNK:
                per = rows // PROJ_CHUNK
                out_ref[0, ci // per, pl.ds(pl.multiple_of((ci % per) * PROJ_CHUNK, PROJ_CHUNK), PROJ_CHUNK),
                        hs] = res
            else:
                per = PROJ_CHUNK // rows
                for s in range(per):
                    out_ref[0, ci * per + s, :, hs] = res[s * rows:(s + 1) * rows]

    def v_slab(gi, buf, ci):
        d = DILATED_GROUPS[gi][1]
        rows = tm // d
        yv = _dot(xn_ref[buf], w_ref[(N_GROUPS + gi) * SLABS + ci])
        for r in range(d):
            outs[N_GROUPS + gi][0, r, ci] = yv[r * rows:(r + 1) * rows].astype(BF16)

    permute(0, 0)
    for s in range(SLABS):
        y_ref[0, s] = _dot(xn_ref[0], w_ref[s])
    for gi in range(N_GROUPS):
        buf = gi % 2
        if gi + 1 < N_GROUPS:
            permute(gi + 1, 1 - buf)

        def body(ci, carry, gi=gi, buf=buf):
            norm_chunk(gi, buf, ci)
            if gi + 1 < N_GROUPS:
                y_ref[1 - buf, ci] = _dot(xn_ref[1 - buf], w_ref[(gi + 1) * SLABS + ci])
            if has_v:
                v_slab(gi, buf, ci)
            return carry

        lax.fori_loop(0, n_chunks, body, 0, unroll=2 if has_v else 1)


def _group_proj(x2d, gain, w, head_gain, has_v, scale, cos_b, sin_b, batch, seq):
    tm = PROJ_TILE
    nt = seq // tm
    n_rope = N_GROUPS * C_B
    n_slabs = w.shape[1] // SLAB_W
    w_slabs = w.astype(BF16).reshape(D_MODEL, n_slabs, SLAB_W).transpose(1, 0, 2)
    in_specs = [pl.BlockSpec((tm, D_MODEL), lambda i: (i, 0)),
                pl.BlockSpec((1, D_MODEL), lambda i: (0, 0)),
                pl.BlockSpec((n_slabs, D_MODEL, SLAB_W), lambda i: (0, 0, 0), pipeline_mode=pl.Buffered(1)),
                pl.BlockSpec((1, n_rope), lambda i: (0, 0)),
                pl.BlockSpec((tm, LANES), lambda i: (i, 0)),
                pl.BlockSpec((tm, LANES), lambda i: (i, 0))]
    shapes, specs = [], []
    for _, d in DILATED_GROUPS:
        shapes.append(jax.ShapeDtypeStruct((batch, d, seq // d, C_B), BF16))
        specs.append(pl.BlockSpec((1, d, tm // d, C_B), lambda i: (i // nt, 0, i % nt, 0)))
    if has_v:
        for _, d in DILATED_GROUPS:
            shapes.append(jax.ShapeDtypeStruct((batch, d, SLABS, seq // d, SLAB_W), BF16))
            specs.append(pl.BlockSpec((1, d, SLABS, tm // d, SLAB_W), lambda i: (i // nt, 0, 0, i % nt, 0)))
    return pl.pallas_call(
        functools.partial(_group_proj_kernel, has_v=has_v, scale=scale),
        out_shape=tuple(shapes),
        grid=(batch * nt,),
        in_specs=in_specs,
        out_specs=tuple(specs),
        scratch_shapes=[pltpu.VMEM((D_MODEL // LANES, tm, LANES), F32), pltpu.VMEM((2, tm, D_MODEL), BF16),
                        pltpu.VMEM((2, SLABS, tm, SLAB_W), F32)],
        compiler_params=_params(("parallel",)),
        name="group_proj_kv" if has_v else "group_proj_q",
    )(x2d, gain.reshape(1, D_MODEL), w_slabs, head_gain.reshape(1, n_rope), cos_b, sin_b)


BAND_HEADS = 8


def _band_kernel(q_ref, k_ref, v_ref, o_ref, lse_ref, *scratch, dilation, length):
    o_scr = scratch[0] if dilation > 1 else None
    nk = min(2 * BAND, length)
    row = lax.broadcasted_iota(jnp.int32, (BAND, nk), 0)
    col = lax.broadcasted_iota(jnp.int32, (BAND, nk), 1)
    lane_head = lax.broadcasted_iota(jnp.int32, (BAND, LANES), 1) // (LANES // BAND_HEADS)

    def block(r, i):
        q0 = i * BAND
        start = jnp.maximum(q0 + BAND - nk, 0)
        dist = (q0 - start) + row - col
        valid = jnp.logical_and(dist >= 0, dist <= BAND)
        if not isinstance(q0, int):
            q0 = pl.multiple_of(q0, BAND)
            start = pl.multiple_of(start, BAND)
        scores = []
        for h in range(BAND_HEADS):
            hs = slice(h * HEAD_DIM_B, (h + 1) * HEAD_DIM_B)
            s = _dot_nt(q_ref[0, r, pl.ds(q0, BAND), hs], k_ref[0, r, pl.ds(start, nk), hs])
            scores.append(jnp.where(valid, s, NEG_INF))
        probs, denoms, lse_tile = [], [], jnp.zeros((BAND, LANES), F32)
        for h in range(BAND_HEADS):
            m = jnp.max(scores[h], axis=-1, keepdims=True)
            p = jnp.exp2(scores[h] - m)
            denom = jnp.sum(p, axis=-1, keepdims=True)
            probs.append(p.astype(BF16))
            denoms.append(denom)
            lse_tile = jnp.where(lane_head == h, m + jnp.log2(denom), lse_tile)
        tok = pl.ds(q0 * dilation + r, BAND, stride=dilation) if dilation > 1 else pl.ds(q0, BAND)
        for h in range(BAND_HEADS):
            hs = slice(h * HEAD_DIM_B, (h + 1) * HEAD_DIM_B)
            lo = h * HEAD_DIM_B % SLAB_W
            out = _dot(probs[h], v_ref[0, r, h * HEAD_DIM_B // SLAB_W, pl.ds(start, nk), lo:lo + HEAD_DIM_B])
            if dilation > 1:
                o_scr[h, tok, :] = out / denoms[h]
            else:
                o_ref[0, h, tok, :] = (out / denoms[h]).astype(o_ref.dtype)
        lse_ref[0, 0, tok, :] = lse_tile

    nb = length // BAND
    for r in range(dilation):
        if nb == 1:
            block(r, 0)
        else:
            def body(i, carry, r=r):
                block(r, i)
                return carry
            lax.fori_loop(0, nb, body, 0, unroll=4)
    if dilation > 1:
        for h in range(BAND_HEADS):
            o_ref[0, h] = o_scr[h].astype(o_ref.dtype)


def _band_attention(q, k, v, dilation, batch, seq):
    length = seq // dilation
    halves = H_B // BAND_HEADS
    cw = BAND_HEADS * HEAD_DIM_B
    blk = pl.BlockSpec((1, dilation, length, cw), lambda b, hh: (b, 0, 0, hh))
    v_blk = pl.BlockSpec((1, dilation, cw // SLAB_W, length, SLAB_W), lambda b, hh: (b, 0, hh, 0, 0))
    return pl.pallas_call(
        functools.partial(_band_kernel, dilation=dilation, length=length),
        out_shape=(jax.ShapeDtypeStruct((batch, H_B, seq, HEAD_DIM_B), BF16),
                   jax.ShapeDtypeStruct((batch, halves, seq, LANES), F32)),
        grid=(batch, halves),
        in_specs=[blk, blk, v_blk],
        out_specs=(pl.BlockSpec((1, BAND_HEADS, seq, HEAD_DIM_B), lambda b, hh: (b, hh, 0, 0)),
                   pl.BlockSpec((1, 1, seq, LANES), lambda b, hh: (b, hh, 0, 0))),
        scratch_shapes=[pltpu.VMEM((BAND_HEADS, seq, HEAD_DIM_B), F32)] if dilation > 1 else [],
        compiler_params=_params(("parallel", "parallel")),
        name=f"band_attn_d{dilation}",
    )(q, k, v)


def _combine_proj_kernel(o0_ref, o1_ref, o2_ref, l0_ref, l1_ref, l2_ref, w_ref, x_ref, y_ref, o_scr):
    lanes_per_head = LANES // BAND_HEADS
    for half in range(H_B // BAND_HEADS):
        l0, l1, l2 = l0_ref[0, half], l1_ref[0, half], l2_ref[0, half]
        mx = jnp.maximum(jnp.maximum(l0, l1), l2)
        e0, e1, e2 = jnp.exp2(l0 - mx), jnp.exp2(l1 - mx), jnp.exp2(l2 - mx)
        inv = 1.0 / (e0 + e1 + e2)
        w0, w1, w2 = e0 * inv, e1 * inv, e2 * inv
        for hq in range(BAND_HEADS):
            h = half * BAND_HEADS + hq
            c = hq * lanes_per_head
            o = (w0[:, c:c + 1] * o0_ref[0, h].astype(F32) + w1[:, c:c + 1] * o1_ref[0, h].astype(F32)
                 + w2[:, c:c + 1] * o2_ref[0, h].astype(F32))
            o_scr[:, h * HEAD_DIM_B:(h + 1) * HEAD_DIM_B] = o.astype(BF16)
    y_ref[...] = x_ref[...] + _dot(o_scr[...], w_ref[...])


def _combine_proj(outs, lses, w_o, x2d, seq):
    T = x2d.shape[0]
    tm = 512
    nt = seq // tm
    halves = H_B // BAND_HEADS
    o_spec = pl.BlockSpec((1, H_B, tm, HEAD_DIM_B), lambda i: (i // nt, 0, i % nt, 0))
    l_spec = pl.BlockSpec((1, halves, tm, LANES), lambda i: (i // nt, 0, i % nt, 0))
    return pl.pallas_call(
        _combine_proj_kernel,
        out_shape=jax.ShapeDtypeStruct((T, D_MODEL), F32),
        grid=(T // tm,),
        in_specs=[o_spec, o_spec, o_spec, l_spec, l_spec, l_spec,
                  pl.BlockSpec((C_B, D_MODEL), lambda i: (0, 0)),
                  pl.BlockSpec((tm, D_MODEL), lambda i: (i, 0))],
        out_specs=pl.BlockSpec((tm, D_MODEL), lambda i: (i, 0)),
        scratch_shapes=[pltpu.VMEM((tm, C_B), BF16)],
        compiler_params=_params(("parallel",)),
        name="combine_proj",
    )(*outs, *lses, w_o.astype(BF16), x2d)


def kernel(x, positions, attn_norm, mlp_norm, mla_w_in, mla_qa_norm, mla_kva_norm, mla_w_qb, mla_w_kvb,
           mla_q_norm, mla_k_norm, mla_w_o, kv_norm, w_kv, k_norm_b, w_q_b, q_norm_b, w_o_b, mlp_w1, mlp_w2):
    B, S, D = x.shape
    T = B * S
    cos_a, sin_a, cos_b, sin_b = _rope_tables(positions)
    h = x.reshape(T, D)

    def head_gains(gn):
        return jnp.broadcast_to(gn[:, None, :], (N_GROUPS, H_B, HEAD_DIM_B))

    for a in range(N_A_LAYERS):
        q, k, v = _mla_proj(h, attn_norm[a], mla_w_in[a], mla_qa_norm[a], mla_kva_norm[a], mla_w_qb[a],
                            mla_w_kvb[a], mla_q_norm[a], mla_k_norm[a], cos_a, sin_a)
        o = _mla_attention(q, k, v, B, S)
        h = _out_proj(o, mla_w_o[a], h)
        h = _mlp(h, mlp_norm[a], mlp_w1[a], mlp_w2[a])

    kv = _group_proj(h, kv_norm, w_kv, head_gains(k_norm_b), True, 1.0, cos_b, sin_b, B, S)
    ks, vs = kv[:N_GROUPS], kv[N_GROUPS:]

    for b in range(N_B_LAYERS):
        layer = N_A_LAYERS + b
        qs = _group_proj(h, attn_norm[layer], w_q_b[b], head_gains(q_norm_b[b]), False,
                         HEAD_DIM_B ** -0.5 * LOG2E, cos_b, sin_b, B, S)
        outs, lses = [], []
        for g, (window, dilation) in enumerate(DILATED_GROUPS):
            assert window // dilation == BAND
            o, lse = _band_attention(qs[g], ks[g], vs[g], dilation, B, S)
            outs.append(o)
            lses.append(lse)
        h = _combine_proj(outs, lses, w_o_b[b], h, S)
        h = _mlp(h, mlp_norm[layer], mlp_w1[layer], mlp_w2[layer])

    return h.reshape(B, S, D)
```

```python
import functools

import jax
import jax.numpy as jnp
from jax import lax
from jax.experimental import pallas as pl
from jax.experimental.pallas import tpu as pltpu

D_MODEL = 1024
N_A_LAYERS = 2
N_B_LAYERS = 2
H_A = 16
QK_NOPE = 128
QK_ROPE = 64
QK_HEAD = QK_NOPE + QK_ROPE
ROPE_HALF_A = QK_ROPE // 2
V_HEAD = 128
Q_LORA = 256
KV_LORA = 128
DILATED_GROUPS = ((128, 1), (512, 4), (2048, 16))
N_GROUPS = 3
H_B = 8
HEAD_DIM_B = 128
C_B = H_B * HEAD_DIM_B
D_FF = 4 * D_MODEL
ROPE_THETA = 10000.0
NORM_EPS = 1e-6
NEG_INF = -1e30
LOG2E = 1.4426950408889634

LANES = 128
QK_SLOT = 2 * LANES
VMEM_LIMIT = 56 * 1024 * 1024
BAND = 128
PROJ_TILE = 512
PROJ_CHUNK = 128
SLAB_W = 2 * LANES
SLABS = C_B // SLAB_W

BF16 = jnp.bfloat16
F32 = jnp.float32


def _params(semantics):
    return pltpu.CompilerParams(dimension_semantics=semantics, vmem_limit_bytes=VMEM_LIMIT)


def _rms(x, gain):
    ms = jnp.mean(x * x, axis=-1, keepdims=True)
    return x * lax.rsqrt(ms + NORM_EPS) * gain


def _rot_half(u):
    return pltpu.roll(u, LANES // 2, axis=1)


def _dot(a, b):
    return jnp.dot(a, b, preferred_element_type=F32)


def _dot_nt(a, b):
    return lax.dot_general(a, b, (((1,), (1,)), ((), ())), preferred_element_type=F32)


def _tables_kernel(pos_ref, f_ref, cm_ref, sm_ref, cos_ref, sin_ref):
    ang = pos_ref[...] * f_ref[...]
    cos_ref[...] = jnp.cos(ang) * cm_ref[...]
    sin_ref[...] = jnp.sin(ang) * sm_ref[...]


def _rope_table(pos, freq, cos_mask, sin_sign):
    n = pos.shape[0]
    tm = 1024
    row = pl.BlockSpec((1, LANES), lambda i: (0, 0))
    tab = pl.BlockSpec((tm, LANES), lambda i: (i, 0))
    shp = jax.ShapeDtypeStruct((n, LANES), F32)
    return pl.pallas_call(
        _tables_kernel,
        out_shape=(shp, shp),
        grid=(n // tm,),
        in_specs=[pl.BlockSpec((tm, 1), lambda i: (i, 0)), row, row, row],
        out_specs=(tab, tab),
        compiler_params=_params(("parallel",)),
        name="rope_table",
    )(pos.reshape(n, 1), freq.reshape(1, LANES), cos_mask.reshape(1, LANES), sin_sign.reshape(1, LANES))


def _rope_tables(positions):
    B, S = positions.shape
    pos = positions.astype(F32)
    inv_a = ROPE_THETA ** (-jnp.arange(0, QK_ROPE, 2, dtype=F32) / QK_ROPE)
    inv_b = ROPE_THETA ** (-jnp.arange(0, HEAD_DIM_B, 2, dtype=F32) / HEAD_DIM_B)
    za = jnp.zeros((ROPE_HALF_A,), F32)
    oa = jnp.ones((ROPE_HALF_A,), F32)
    ob = jnp.ones((HEAD_DIM_B // 2,), F32)
    cos_a, sin_a = _rope_table(pos.reshape(B * S), jnp.concatenate([inv_a, za, inv_a, za]),
                               jnp.concatenate([oa, za, oa, za]), jnp.concatenate([-oa, za, oa, za]))
    cos_b, sin_b = _rope_table(pos.reshape(B * S), jnp.concatenate([inv_b, inv_b]),
                               jnp.concatenate([ob, ob]), jnp.concatenate([-ob, ob]))
    return cos_a, sin_a, cos_b, sin_b


def _mla_proj_kernel(x_ref, g_ref, win_ref, qa_ref, kva_ref, wqb_ref, wkb_ref, wvb_ref, qg_ref, kg_ref,
                     cos_ref, sin_ref, q_ref, k_ref, v_ref, cq_scr, ckv_scr, kpe_scr, kss_scr, q_scr, kn_scr,
                     *, scale):
    tm = x_ref.shape[0]
    n_pairs = H_A // 2
    xn = _rms(x_ref[...], g_ref[...]).astype(BF16)
    lat = _dot(xn, win_ref[...])
    cq_scr[...] = _rms(lat[:, :Q_LORA], qa_ref[...]).astype(BF16)
    ckv_scr[...] = _rms(lat[:, Q_LORA:Q_LORA + KV_LORA], kva_ref[...]).astype(BF16)
    k_pe = lat[:, Q_LORA + KV_LORA:]
    kss_scr[...] = jnp.broadcast_to(jnp.sum(k_pe * k_pe, axis=-1, keepdims=True), (tm, LANES))
    kpe_g = k_pe * kg_ref[:, LANES:]
    kpe_scr[...] = kpe_g * cos_ref[...] + _rot_half(kpe_g) * sin_ref[...]

    def matmuls(pair, buf):
        q_scr[buf] = _dot(cq_scr[...], wqb_ref[pair])
        kn_scr[buf] = _dot(ckv_scr[...], wkb_ref[pair])

    def finish(pair, buf):
        v_ref[pair] = _dot(ckv_scr[...], wvb_ref[pair]).astype(BF16)
        qg_n, qg_pe = qg_ref[:, :LANES], qg_ref[:, LANES:]
        kg_n = kg_ref[:, :LANES]
        for c in range(tm // PROJ_CHUNK):
            rows = slice(c * PROJ_CHUNK, (c + 1) * PROJ_CHUNK)
            cos = cos_ref[rows, :]
            sin = sin_ref[rows, :]
            kpe_ss = kss_scr[rows, :]
            kpe_rot = kpe_scr[rows, :]
            for hh in range(2):
                qn = q_scr[buf, rows, hh * QK_SLOT:hh * QK_SLOT + LANES]
                qp = q_scr[buf, rows, hh * QK_SLOT + LANES:(hh + 1) * QK_SLOT]
                ss = jnp.sum(qn * qn + qp * qp, axis=-1, keepdims=True)
                rs = lax.rsqrt(ss * (1.0 / QK_HEAD) + NORM_EPS) * scale
                qpg = qp * qg_pe
                q_ref[pair, rows, hh * QK_SLOT:hh * QK_SLOT + LANES] = (qn * rs * qg_n).astype(BF16)
                q_ref[pair, rows, hh * QK_SLOT + LANES:(hh + 1) * QK_SLOT] = (
                    (qpg * cos + _rot_half(qpg) * sin) * rs).astype(BF16)
                kn = kn_scr[buf, rows, hh * LANES:(hh + 1) * LANES]
                ssk = jnp.sum(kn * kn, axis=-1, keepdims=True) + kpe_ss
                rsk = lax.rsqrt(ssk * (1.0 / QK_HEAD) + NORM_EPS)
                k_ref[pair, rows, hh * QK_SLOT:hh * QK_SLOT + LANES] = (kn * rsk * kg_n).astype(BF16)
                k_ref[pair, rows, hh * QK_SLOT + LANES:(hh + 1) * QK_SLOT] = (kpe_rot * rsk).astype(BF16)

    matmuls(0, 0)

    def body(jj, carry):
        pair = 2 * jj
        matmuls(pair + 1, 1)
        finish(pair, 0)
        matmuls(jnp.minimum(pair + 2, n_pairs - 1), 0)
        finish(pair + 1, 1)
        return carry

    lax.fori_loop(0, n_pairs // 2, body, 0)


def _rope_tile_cols(a):
    z = jnp.zeros(a.shape[:-1] + (ROPE_HALF_A,), a.dtype)
    return jnp.concatenate([a[..., :ROPE_HALF_A], z, a[..., ROPE_HALF_A:], z], axis=-1)


def _mla_proj(x2d, gain, w_in, qa_norm, kva_norm, w_qb, w_kvb, q_norm, k_norm, cos_a, sin_a):
    T = x2d.shape[0]
    tm = 256
    w_in_p = jnp.concatenate(
        [w_in[:, :Q_LORA + KV_LORA], _rope_tile_cols(w_in[:, Q_LORA + KV_LORA:])], axis=-1).astype(BF16)
    wq = w_qb.reshape(Q_LORA, H_A, QK_HEAD)
    wq_p = jnp.concatenate([wq[..., :QK_NOPE], _rope_tile_cols(wq[..., QK_NOPE:])], axis=-1)
    n_pairs = H_A // 2

    def pair_slabs(w2d):
        k_dim = w2d.shape[0]
        return w2d.reshape(k_dim, n_pairs, -1).transpose(1, 0, 2).astype(BF16)

    wq_p = pair_slabs(wq_p.reshape(Q_LORA, H_A * QK_SLOT))
    wkv = w_kvb.reshape(KV_LORA, H_A, QK_NOPE + V_HEAD)
    wkb = pair_slabs(wkv[..., :QK_NOPE].reshape(KV_LORA, H_A * QK_NOPE))
    wvb = pair_slabs(wkv[..., QK_NOPE:].reshape(KV_LORA, H_A * V_HEAD))
    qg = jnp.concatenate([q_norm[:QK_NOPE], _rope_tile_cols(q_norm[QK_NOPE:])]).reshape(1, QK_SLOT)
    kg = jnp.concatenate([k_norm[:QK_NOPE], _rope_tile_cols(k_norm[QK_NOPE:])]).reshape(1, QK_SLOT)

    def const(shape):
        return pl.BlockSpec(shape, lambda i: (0,) * len(shape))

    def rows(width):
        return pl.BlockSpec((tm, width), lambda i: (i, 0))

    def pair_rows(width):
        return pl.BlockSpec((n_pairs, tm, width), lambda i: (0, i, 0))

    n_in = Q_LORA + KV_LORA + LANES
    return pl.pallas_call(
        functools.partial(_mla_proj_kernel, scale=QK_HEAD ** -0.5 * LOG2E),
        out_shape=(jax.ShapeDtypeStruct((n_pairs, T, 2 * QK_SLOT), BF16),
                   jax.ShapeDtypeStruct((n_pairs, T, 2 * QK_SLOT), BF16),
                   jax.ShapeDtypeStruct((n_pairs, T, 2 * V_HEAD), BF16)),
        grid=(T // tm,),
        in_specs=[rows(D_MODEL), const((1, D_MODEL)), const((D_MODEL, n_in)), const((1, Q_LORA)),
                  const((1, KV_LORA)), const((n_pairs, Q_LORA, 2 * QK_SLOT)), const((n_pairs, KV_LORA, 2 * QK_NOPE)),
                  const((n_pairs, KV_LORA, 2 * V_HEAD)), const((1, QK_SLOT)), const((1, QK_SLOT)),
                  rows(LANES), rows(LANES)],
        out_specs=(pair_rows(2 * QK_SLOT), pair_rows(2 * QK_SLOT), pair_rows(2 * V_HEAD)),
        scratch_shapes=[pltpu.VMEM((tm, Q_LORA), BF16), pltpu.VMEM((tm, KV_LORA), BF16),
                        pltpu.VMEM((tm, LANES), F32), pltpu.VMEM((tm, LANES), F32),
                        pltpu.VMEM((2, tm, 2 * QK_SLOT), F32), pltpu.VMEM((2, tm, 2 * QK_NOPE), F32)],
        compiler_params=_params(("parallel",)),
        name="mla_proj",
    )(x2d, gain.reshape(1, D_MODEL), w_in_p, qa_norm.reshape(1, Q_LORA), kva_norm.reshape(1, KV_LORA),
      wq_p, wkb, wvb, qg, kg, cos_a, sin_a)


FLASH_HEADS = 8


FLASH_ROWS = 64


def _flash_kernel(q_ref, k_ref, v_ref, o_ref, s_scr, p_scr, m_scr, l_scr, a_scr, acc_scr, *, tq, tk):
    i = pl.program_id(2)
    m_scr[...] = jnp.full(m_scr.shape, NEG_INF, F32)
    l_scr[...] = jnp.zeros(l_scr.shape, F32)
    acc_scr[...] = jnp.zeros(acc_scr.shape, F32)

    half = tq // 2

    def scores(j, buf, masked):
        off = pl.multiple_of(j * tk, tk)
        for h in range(FLASH_HEADS):
            qk_cols = slice((h % 2) * QK_SLOT, (h % 2 + 1) * QK_SLOT)
            if masked:
                s_scr[buf, h, :half, :half] = _dot_nt(q_ref[h // 2, :half, qk_cols],
                                                      k_ref[h // 2, pl.ds(off, half), qk_cols])
                s_scr[buf, h, half:, :] = _dot_nt(q_ref[h // 2, half:, qk_cols],
                                                  k_ref[h // 2, pl.ds(off, tk), qk_cols])
            else:
                s_scr[buf, h] = _dot_nt(q_ref[h // 2, :, qk_cols], k_ref[h // 2, pl.ds(off, tk), qk_cols])

    def softmax_pv(j, buf, masked):
        off = pl.multiple_of(j * tk, tk)
        for h in range(FLASH_HEADS):
            v_cols = slice((h % 2) * V_HEAD, (h % 2 + 1) * V_HEAD)
            for rb in range(tq // FLASH_ROWS):
                rows = slice(rb * FLASH_ROWS, (rb + 1) * FLASH_ROWS)
                cw = min(tk, -(-((rb + 1) * FLASH_ROWS) // LANES) * LANES) if masked else tk
                pw = (half if (rb + 1) * FLASH_ROWS <= half else tk) if masked else tk
                s = s_scr[buf, h, rows, :cw]
                if masked:
                    row = lax.broadcasted_iota(jnp.int32, (FLASH_ROWS, cw), 0) + rb * FLASH_ROWS
                    col = lax.broadcasted_iota(jnp.int32, (FLASH_ROWS, cw), 1)
                    s = jnp.where(row >= col, s, NEG_INF)
                m_old = m_scr[h, rows, :]
                m_new = jnp.maximum(m_old, jnp.max(s, axis=-1, keepdims=True))
                p = jnp.exp2(s - jnp.tile(m_new, (1, cw // LANES)))
                alpha = jnp.exp2(m_old - m_new)
                l_scr[h, rows, :] = alpha * l_scr[h, rows, :] + jnp.sum(p, axis=-1, keepdims=True)
                m_scr[h, rows, :] = m_new
                a_scr[h, rows, :] = alpha
                p_scr[h, rows, :cw] = p.astype(BF16)
                if cw < pw:
                    p_scr[h, rows, cw:pw] = jnp.zeros((FLASH_ROWS, pw - cw), BF16)
            if masked:
                pv = _dot(p_scr[h, :half, :half], v_ref[h // 2, pl.ds(off, half), v_cols])
                acc_scr[h, :half] = a_scr[h, :half] * acc_scr[h, :half] + pv
                pv = _dot(p_scr[h, half:, :], v_ref[h // 2, pl.ds(off, tk), v_cols])
                acc_scr[h, half:] = a_scr[h, half:] * acc_scr[h, half:] + pv
            else:
                pv = _dot(p_scr[h], v_ref[h // 2, pl.ds(off, tk), v_cols])
                acc_scr[h] = a_scr[h] * acc_scr[h] + pv

    def body(j, carry):
        scores(j, 0, False)
        softmax_pv(j, 0, False)
        return carry

    assert tq == tk
    lax.fori_loop(0, i, body, 0)
    scores(i, 0, True)
    softmax_pv(i, 0, True)
    for h in range(FLASH_HEADS):
        o_ref[:, h * V_HEAD:(h + 1) * V_HEAD] = (acc_scr[h] / l_scr[h]).astype(o_ref.dtype)


def _mla_attention(q, k, v, batch, seq):
    tq = tk = 512
    nq = seq // tq
    T = q.shape[1]
    hh = FLASH_HEADS
    assert q.shape[0] * 2 == H_A and hh % 2 == 0
    stat = pltpu.VMEM((hh, tq, LANES), F32)
    return pl.pallas_call(
        functools.partial(_flash_kernel, tq=tq, tk=tk),
        out_shape=jax.ShapeDtypeStruct((T, H_A * V_HEAD), BF16),
        grid=(batch, H_A // hh, nq),
        in_specs=[pl.BlockSpec((hh // 2, tq, 2 * QK_SLOT), lambda b, h, i: (h, b * nq + i, 0)),
                  pl.BlockSpec((hh // 2, seq, 2 * QK_SLOT), lambda b, h, i: (h, b, 0)),
                  pl.BlockSpec((hh // 2, seq, 2 * V_HEAD), lambda b, h, i: (h, b, 0))],
        out_specs=pl.BlockSpec((tq, hh * V_HEAD), lambda b, h, i: (b * nq + i, h)),
        scratch_shapes=[pltpu.VMEM((1, hh, tq, tk), F32), pltpu.VMEM((hh, tq, tk), BF16), stat, stat, stat,
                        pltpu.VMEM((hh, tq, V_HEAD), F32)],
        compiler_params=_params(("parallel", "parallel", "arbitrary")),
        name="mla_flash",
    )(q, k, v)


def _out_proj_kernel(o_ref, w_ref, x_ref, y_ref):
    y_ref[...] = x_ref[...] + _dot(o_ref[...], w_ref[...])


def _out_proj(o, w_o, x2d):
    T, K = o.shape
    tm = 512
    return pl.pallas_call(
        _out_proj_kernel,
        out_shape=jax.ShapeDtypeStruct((T, D_MODEL), F32),
        grid=(T // tm,),
        in_specs=[pl.BlockSpec((tm, K), lambda i: (i, 0)),
                  pl.BlockSpec((K, D_MODEL), lambda i: (0, 0)),
                  pl.BlockSpec((tm, D_MODEL), lambda i: (i, 0))],
        out_specs=pl.BlockSpec((tm, D_MODEL), lambda i: (i, 0)),
        compiler_params=_params(("parallel",)),
        name="out_proj",
    )(o, w_o.astype(BF16), x2d)


def _mlp_kernel(x_ref, g_ref, w1_ref, w2_ref, y_ref, xn_ref, acc_ref):
    f = pl.program_id(1)

    @pl.when(f == 0)
    def _():
        xn_ref[...] = _rms(x_ref[...], g_ref[...]).astype(BF16)
        acc_ref[...] = jnp.zeros_like(acc_ref)

    h = jnp.maximum(_dot(xn_ref[...], w1_ref[...]), 0.0)
    acc_ref[...] += _dot((h * h).astype(BF16), w2_ref[...])

    @pl.when(f == pl.num_programs(1) - 1)
    def _():
        y_ref[...] = x_ref[...] + acc_ref[...]


def _mlp(x2d, gain, w1, w2):
    T = x2d.shape[0]
    tm, tf = 1024, 2048
    return pl.pallas_call(
        _mlp_kernel,
        out_shape=jax.ShapeDtypeStruct((T, D_MODEL), F32),
        grid=(T // tm, D_FF // tf),
        in_specs=[pl.BlockSpec((tm, D_MODEL), lambda i, f: (i, 0)),
                  pl.BlockSpec((1, D_MODEL), lambda i, f: (0, 0)),
                  pl.BlockSpec((D_MODEL, tf), lambda i, f: (0, f)),
                  pl.BlockSpec((tf, D_MODEL), lambda i, f: (f, 0))],
        out_specs=pl.BlockSpec((tm, D_MODEL), lambda i, f: (i, 0)),
        scratch_shapes=[pltpu.VMEM((tm, D_MODEL), BF16), pltpu.VMEM((tm, D_MODEL), F32)],
        compiler_params=_params(("parallel", "arbitrary")),
        name="mlp",
    )(x2d, gain.reshape(1, D_MODEL), w1.astype(BF16), w2.astype(BF16))


def _group_proj_kernel(*refs, has_v, scale):
    if has_v:
        x_ref, g_ref, w_ref, hg_ref, cos_ref, sin_ref = refs[:6]
        outs = refs[6:12]
        xs_ref, xn_ref, y_ref = refs[12:]
    else:
        x_ref, g_ref, w_ref, hg_ref, cos_ref, sin_ref = refs[:6]
        outs = refs[6:9]
        xs_ref, xn_ref, y_ref = refs[9:]
    tm = x_ref.shape[0]
    n_col = D_MODEL // LANES
    n_chunks = tm // PROJ_CHUNK
    assert n_chunks == SLABS
    xn = _rms(x_ref[...], g_ref[...])
    for c in range(n_col):
        xs_ref[c] = xn[:, c * LANES:(c + 1) * LANES]

    def permute(gi, buf):
        d = DILATED_GROUPS[gi][1]
        rows = tm // d
        if d == 1:
            xn_ref[buf] = xn.astype(BF16)
        else:
            for r in range(d):
                for c in range(n_col):
                    xn_ref[buf, r * rows:(r + 1) * rows, c * LANES:(c + 1) * LANES] = (
                        xs_ref[c, pl.ds(r, rows, stride=d), :].astype(BF16))

    def norm_chunk(gi, buf, ci):
        d = DILATED_GROUPS[gi][1]
        rows = tm // d
        out_ref = outs[gi]
        r0 = pl.multiple_of(ci * PROJ_CHUNK, PROJ_CHUNK)

        def table_rows(t_ref):
            if d == 1:
                return t_ref[pl.ds(r0, PROJ_CHUNK), :]
            if rows >= PROJ_CHUNK:
                per = rows // PROJ_CHUNK
                return t_ref[pl.ds((ci % per) * PROJ_CHUNK * d + ci // per, PROJ_CHUNK, stride=d), :]
            per = PROJ_CHUNK // rows
            return jnp.concatenate([t_ref[pl.ds(ci * per + s, rows, stride=d), :] for s in range(per)], axis=0)

        cos = table_rows(cos_ref)
        sin = table_rows(sin_ref)
        for h in range(H_B):
            hs = slice(h * HEAD_DIM_B, (h + 1) * HEAD_DIM_B)
            sl = h * HEAD_DIM_B // SLAB_W
            lo = h * HEAD_DIM_B % SLAB_W
            yh = y_ref[buf, sl, pl.ds(r0, PROJ_CHUNK), lo:lo + HEAD_DIM_B]
            rs = lax.rsqrt(jnp.mean(yh * yh, axis=-1, keepdims=True) + NORM_EPS) * scale
            yg = yh * hg_ref[:, gi * C_B + h * HEAD_DIM_B:gi * C_B + (h + 1) * HEAD_DIM_B]
            res = ((yg * cos + _rot_half(yg) * sin) * rs).astype(BF16)
            if rows >= PROJ_CHUNK:
                per = rows // PROJ_CHUNK
                out_ref[0, ci // per, pl.ds(pl.multiple_of((ci % per) * PROJ_CHUNK, PROJ_CHUNK), PROJ_CHUNK),
                        hs] = res
            else:
                per = PROJ_CHUNK // rows
                for s in range(per):
                    out_ref[0, ci * per + s, :, hs] = res[s * rows:(s + 1) * rows]

    def v_slab(gi, buf, ci):
        d = DILATED_GROUPS[gi][1]
        rows = tm // d
        yv = _dot(xn_ref[buf], w_ref[(N_GROUPS + gi) * SLABS + ci])
        for r in range(d):
            outs[N_GROUPS + gi][0, r, ci] = yv[r * rows:(r + 1) * rows].astype(BF16)

    permute(0, 0)
    for s in range(SLABS):
        y_ref[0, s] = _dot(xn_ref[0], w_ref[s])
    for gi in range(N_GROUPS):
        buf = gi % 2
        if gi + 1 < N_GROUPS:
            permute(gi + 1, 1 - buf)

        def body(ci, carry, gi=gi, buf=buf):
            norm_chunk(gi, buf, ci)
            if gi + 1 < N_GROUPS:
                y_ref[1 - buf, ci] = _dot(xn_ref[1 - buf], w_ref[(gi + 1) * SLABS + ci])
            if has_v:
                v_slab(gi, buf, ci)
            return carry

        lax.fori_loop(0, n_chunks, body, 0, unroll=2 if has_v else 1)


def _group_proj(x2d, gain, w, head_gain, has_v, scale, cos_b, sin_b, batch, seq):
    tm = PROJ_TILE
    nt = seq // tm
    n_rope = N_GROUPS * C_B
    n_slabs = w.shape[1] // SLAB_W
    w_slabs = w.astype(BF16).reshape(D_MODEL, n_slabs, SLAB_W).transpose(1, 0, 2)
    in_specs = [pl.BlockSpec((tm, D_MODEL), lambda i: (i, 0)),
                pl.BlockSpec((1, D_MODEL), lambda i: (0, 0)),
                pl.BlockSpec((n_slabs, D_MODEL, SLAB_W), lambda i: (0, 0, 0), pipeline_mode=pl.Buffered(1)),
                pl.BlockSpec((1, n_rope), lambda i: (0, 0)),
                pl.BlockSpec((tm, LANES), lambda i: (i, 0)),
                pl.BlockSpec((tm, LANES), lambda i: (i, 0))]
    shapes, specs = [], []
    for _, d in DILATED_GROUPS:
        shapes.append(jax.ShapeDtypeStruct((batch, d, seq // d, C_B), BF16))
        specs.append(pl.BlockSpec((1, d, tm // d, C_B), lambda i: (i // nt, 0, i % nt, 0)))
    if has_v:
        for _, d in DILATED_GROUPS:
            shapes.append(jax.ShapeDtypeStruct((batch, d, SLABS, seq // d, SLAB_W), BF16))
            specs.append(pl.BlockSpec((1, d, SLABS, tm // d, SLAB_W), lambda i: (i // nt, 0, 0, i % nt, 0)))
    return pl.pallas_call(
        functools.partial(_group_proj_kernel, has_v=has_v, scale=scale),
        out_shape=tuple(shapes),
        grid=(batch * nt,),
        in_specs=in_specs,
        out_specs=tuple(specs),
        scratch_shapes=[pltpu.VMEM((D_MODEL // LANES, tm, LANES), F32), pltpu.VMEM((2, tm, D_MODEL), BF16),
                        pltpu.VMEM((2, SLABS, tm, SLAB_W), F32)],
        compiler_params=_params(("parallel",)),
        name="group_proj_kv" if has_v else "group_proj_q",
    )(x2d, gain.reshape(1, D_MODEL), w_slabs, head_gain.reshape(1, n_rope), cos_b, sin_b)


BAND_HEADS = 8


def _band_kernel(q_ref, k_ref, v_ref, o_ref, lse_ref, *scratch, dilation, length):
    o_scr = scratch[0] if dilation > 1 else None
    nk = min(2 * BAND, length)
    row = lax.broadcasted_iota(jnp.int32, (BAND, nk), 0)
    col = lax.broadcasted_iota(jnp.int32, (BAND, nk), 1)
    lane_head = lax.broadcasted_iota(jnp.int32, (BAND, LANES), 1) // (LANES // BAND_HEADS)

    def block(r, i):
        q0 = i * BAND
        start = jnp.maximum(q0 + BAND - nk, 0)
        dist = (q0 - start) + row - col
        valid = jnp.logical_and(dist >= 0, dist <= BAND)
        if not isinstance(q0, int):
            q0 = pl.multiple_of(q0, BAND)
            start = pl.multiple_of(start, BAND)
        scores = []
        for h in range(BAND_HEADS):
            hs = slice(h * HEAD_DIM_B, (h + 1) * HEAD_DIM_B)
            s = _dot_nt(q_ref[0, r, pl.ds(q0, BAND), hs], k_ref[0, r, pl.ds(start, nk), hs])
            scores.append(jnp.where(valid, s, NEG_INF))
        probs, denoms, lse_tile = [], [], jnp.zeros((BAND, LANES), F32)
        for h in range(BAND_HEADS):
            m = jnp.max(scores[h], axis=-1, keepdims=True)
            p = jnp.exp2(scores[h] - m)
            denom = jnp.sum(p, axis=-1, keepdims=True)
            probs.append(p.astype(BF16))
            denoms.append(denom)
            lse_tile = jnp.where(lane_head == h, m + jnp.log2(denom), lse_tile)
        tok = pl.ds(q0 * dilation + r, BAND, stride=dilation) if dilation > 1 else pl.ds(q0, BAND)
        for h in range(BAND_HEADS):
            hs = slice(h * HEAD_DIM_B, (h + 1) * HEAD_DIM_B)
            lo = h * HEAD_DIM_B % SLAB_W
            out = _dot(probs[h], v_ref[0, r, h * HEAD_DIM_B // SLAB_W, pl.ds(start, nk), lo:lo + HEAD_DIM_B])
            if dilation > 1:
                o_scr[h, tok, :] = out / denoms[h]
            else:
                o_ref[0, h, tok, :] = (out / denoms[h]).astype(o_ref.dtype)
        lse_ref[0, 0, tok, :] = lse_tile

    nb = length // BAND
    for r in range(dilation):
        if nb == 1:
            block(r, 0)
        else:
            def body(i, carry, r=r):
                block(r, i)
                return carry
            lax.fori_loop(0, nb, body, 0, unroll=4)
    if dilation > 1:
        for h in range(BAND_HEADS):
            o_ref[0, h] = o_scr[h].astype(o_ref.dtype)


def _band_attention(q, k, v, dilation, batch, seq):
    length = seq // dilation
    halves = H_B // BAND_HEADS
    cw = BAND_HEADS * HEAD_DIM_B
    blk = pl.BlockSpec((1, dilation, length, cw), lambda b, hh: (b, 0, 0, hh))
    v_blk = pl.BlockSpec((1, dilation, cw // SLAB_W, length, SLAB_W), lambda b, hh: (b, 0, hh, 0, 0))
    return pl.pallas_call(
        functools.partial(_band_kernel, dilation=dilation, length=length),
        out_shape=(jax.ShapeDtypeStruct((batch, H_B, seq, HEAD_DIM_B), BF16),
                   jax.ShapeDtypeStruct((batch, halves, seq, LANES), F32)),
        grid=(batch, halves),
        in_specs=[blk, blk, v_blk],
        out_specs=(pl.BlockSpec((1, BAND_HEADS, seq, HEAD_DIM_B), lambda b, hh: (b, hh, 0, 0)),
                   pl.BlockSpec((1, 1, seq, LANES), lambda b, hh: (b, hh, 0, 0))),
        scratch_shapes=[pltpu.VMEM((BAND_HEADS, seq, HEAD_DIM_B), F32)] if dilation > 1 else [],
        compiler_params=_params(("parallel", "parallel")),
        name=f"band_attn_d{dilation}",
    )(q, k, v)


def _combine_proj_kernel(o0_ref, o1_ref, o2_ref, l0_ref, l1_ref, l2_ref, w_ref, x_ref, y_ref, o_scr):
    lanes_per_head = LANES // BAND_HEADS
    for half in range(H_B // BAND_HEADS):
        l0, l1, l2 = l0_ref[0, half], l1_ref[0, half], l2_ref[0, half]
        mx = jnp.maximum(jnp.maximum(l0, l1), l2)
        e0, e1, e2 = jnp.exp2(l0 - mx), jnp.exp2(l1 - mx), jnp.exp2(l2 - mx)
        inv = 1.0 / (e0 + e1 + e2)
        w0, w1, w2 = e0 * inv, e1 * inv, e2 * inv
        for hq in range(BAND_HEADS):
            h = half * BAND_HEADS + hq
            c = hq * lanes_per_head
            o = (w0[:, c:c + 1] * o0_ref[0, h].astype(F32) + w1[:, c:c + 1] * o1_ref[0, h].astype(F32)
                 + w2[:, c:c + 1] * o2_ref[0, h].astype(F32))
            o_scr[:, h * HEAD_DIM_B:(h + 1) * HEAD_DIM_B] = o.astype(BF16)
    y_ref[...] = x_ref[...] + _dot(o_scr[...], w_ref[...])


def _combine_proj(outs, lses, w_o, x2d, seq):
    T = x2d.shape[0]
    tm = 512
    nt = seq // tm
    halves = H_B // BAND_HEADS
    o_spec = pl.BlockSpec((1, H_B, tm, HEAD_DIM_B), lambda i: (i // nt, 0, i % nt, 0))
    l_spec = pl.BlockSpec((1, halves, tm, LANES), lambda i: (i // nt, 0, i % nt, 0))
    return pl.pallas_call(
        _combine_proj_kernel,
        out_shape=jax.ShapeDtypeStruct((T, D_MODEL), F32),
        grid=(T // tm,),
        in_specs=[o_spec, o_spec, o_spec, l_spec, l_spec, l_spec,
                  pl.BlockSpec((C_B, D_MODEL), lambda i: (0, 0)),
                  pl.BlockSpec((tm, D_MODEL), lambda i: (i, 0))],
        out_specs=pl.BlockSpec((tm, D_MODEL), lambda i: (i, 0)),
        scratch_shapes=[pltpu.VMEM((tm, C_B), BF16)],
        compiler_params=_params(("parallel",)),
        name="combine_proj",
    )(*outs, *lses, w_o.astype(BF16), x2d)


def kernel(x, positions, attn_norm, mlp_norm, mla_w_in, mla_qa_norm, mla_kva_norm, mla_w_qb, mla_w_kvb,
           mla_q_norm, mla_k_norm, mla_w_o, kv_norm, w_kv, k_norm_b, w_q_b, q_norm_b, w_o_b, mlp_w1, mlp_w2):
    B, S, D = x.shape
    T = B * S
    cos_a, sin_a, cos_b, sin_b = _rope_tables(positions)
    h = x.reshape(T, D)

    def head_gains(gn):
        return jnp.broadcast_to(gn[:, None, :], (N_GROUPS, H_B, HEAD_DIM_B))

    for a in range(N_A_LAYERS):
        q, k, v = _mla_proj(h, attn_norm[a], mla_w_in[a], mla_qa_norm[a], mla_kva_norm[a], mla_w_qb[a],
                            mla_w_kvb[a], mla_q_norm[a], mla_k_norm[a], cos_a, sin_a)
        o = _mla_attention(q, k, v, B, S)
        h = _out_proj(o, mla_w_o[a], h)
        h = _mlp(h, mlp_norm[a], mlp_w1[a], mlp_w2[a])

    kv = _group_proj(h, kv_norm, w_kv, head_gains(k_norm_b), True, 1.0, cos_b, sin_b, B, S)
    ks, vs = kv[:N_GROUPS], kv[N_GROUPS:]

    for b in range(N_B_LAYERS):
        layer = N_A_LAYERS + b
        qs = _group_proj(h, attn_norm[layer], w_q_b[b], head_gains(q_norm_b[b]), False,
                         HEAD_DIM_B ** -0.5 * LOG2E, cos_b, sin_b, B, S)
        outs, lses = [], []
        for g, (window, dilation) in enumerate(DILATED_GROUPS):
            assert window // dilation == BAND
            o, lse = _band_attention(qs[g], ks[g], vs[g], dilation, B, S)
            outs.append(o)
            lses.append(lse)
        h = _combine_proj(outs, lses, w_o_b[b], h, S)
        h = _mlp(h, mlp_norm[layer], mlp_w1[layer], mlp_w2[layer])

    return h.reshape(B, S, D)
```

```python
import functools

import jax
import jax.numpy as jnp
from jax import lax
from jax.experimental import pallas as pl
from jax.experimental.pallas import tpu as pltpu

D_MODEL = 1024
N_A_LAYERS = 2
N_B_LAYERS = 2
H_A = 16
QK_NOPE = 128
QK_ROPE = 64
QK_HEAD = QK_NOPE + QK_ROPE
ROPE_HALF_A = QK_ROPE // 2
V_HEAD = 128
Q_LORA = 256
KV_LORA = 128
DILATED_GROUPS = ((128, 1), (512, 4), (2048, 16))
N_GROUPS = 3
H_B = 8
HEAD_DIM_B = 128
C_B = H_B * HEAD_DIM_B
D_FF = 4 * D_MODEL
ROPE_THETA = 10000.0
NORM_EPS = 1e-6
NEG_INF = -1e30
LOG2E = 1.4426950408889634

LANES = 128
QK_SLOT = 2 * LANES
VMEM_LIMIT = 56 * 1024 * 1024
BAND = 128
PROJ_TILE = 512
PROJ_CHUNK = 128
SLAB_W = 2 * LANES
SLABS = C_B // SLAB_W

BF16 = jnp.bfloat16
F32 = jnp.float32


def _params(semantics):
    return pltpu.CompilerParams(dimension_semantics=semantics, vmem_limit_bytes=VMEM_LIMIT)


def _rms(x, gain):
    ms = jnp.mean(x * x, axis=-1, keepdims=True)
    return x * lax.rsqrt(ms + NORM_EPS) * gain


def _rot_half(u):
    return pltpu.roll(u, LANES // 2, axis=1)


def _dot(a, b):
    return jnp.dot(a, b, preferred_element_type=F32)


def _dot_nt(a, b):
    return lax.dot_general(a, b, (((1,), (1,)), ((), ())), preferred_element_type=F32)


def _tables_kernel(pos_ref, f_ref, cm_ref, sm_ref, cos_ref, sin_ref):
    ang = pos_ref[...] * f_ref[...]
    cos_ref[...] = jnp.cos(ang) * cm_ref[...]
    sin_ref[...] = jnp.sin(ang) * sm_ref[...]


def _rope_table(pos, freq, cos_mask, sin_sign):
    n = pos.shape[0]
    tm = 1024
    row = pl.BlockSpec((1, LANES), lambda i: (0, 0))
    tab = pl.BlockSpec((tm, LANES), lambda i: (i, 0))
    shp = jax.ShapeDtypeStruct((n, LANES), F32)
    return pl.pallas_call(
        _tables_kernel,
        out_shape=(shp, shp),
        grid=(n // tm,),
        in_specs=[pl.BlockSpec((tm, 1), lambda i: (i, 0)), row, row, row],
        out_specs=(tab, tab),
        compiler_params=_params(("parallel",)),
        name="rope_table",
    )(pos.reshape(n, 1), freq.reshape(1, LANES), cos_mask.reshape(1, LANES), sin_sign.reshape(1, LANES))


def _rope_tables(positions):
    B, S = positions.shape
    pos = positions.astype(F32)
    inv_a = ROPE_THETA ** (-jnp.arange(0, QK_ROPE, 2, dtype=F32) / QK_ROPE)
    inv_b = ROPE_THETA ** (-jnp.arange(0, HEAD_DIM_B, 2, dtype=F32) / HEAD_DIM_B)
    za = jnp.zeros((ROPE_HALF_A,), F32)
    oa = jnp.ones((ROPE_HALF_A,), F32)
    ob = jnp.ones((HEAD_DIM_B // 2,), F32)
    cos_a, sin_a = _rope_table(pos.reshape(B * S), jnp.concatenate([inv_a, za, inv_a, za]),
                               jnp.concatenate([oa, za, oa, za]), jnp.concatenate([-oa, za, oa, za]))
    cos_b, sin_b = _rope_table(pos.reshape(B * S), jnp.concatenate([inv_b, inv_b]),
                               jnp.concatenate([ob, ob]), jnp.concatenate([-ob, ob]))
    return cos_a, sin_a, cos_b, sin_b


def _mla_proj_kernel(x_ref, g_ref, win_ref, qa_ref, kva_ref, wqb_ref, wkb_ref, wvb_ref, qg_ref, kg_ref,
                     cos_ref, sin_ref, q_ref, k_ref, v_ref, cq_scr, ckv_scr, kpe_scr, kss_scr, q_scr, kn_scr,
                     *, scale):
    tm = x_ref.shape[0]
    n_pairs = H_A // 2
    xn = _rms(x_ref[...], g_ref[...]).astype(BF16)
    lat = _dot(xn, win_ref[...])
    cq_scr[...] = _rms(lat[:, :Q_LORA], qa_ref[...]).astype(BF16)
    ckv_scr[...] = _rms(lat[:, Q_LORA:Q_LORA + KV_LORA], kva_ref[...]).astype(BF16)
    k_pe = lat[:, Q_LORA + KV_LORA:]
    kss_scr[...] = jnp.broadcast_to(jnp.sum(k_pe * k_pe, axis=-1, keepdims=True), (tm, LANES))
    kpe_g = k_pe * kg_ref[:, LANES:]
    kpe_scr[...] = kpe_g * cos_ref[...] + _rot_half(kpe_g) * sin_ref[...]

    def matmuls(pair, buf):
        q_scr[buf] = _dot(cq_scr[...], wqb_ref[pair])
        kn_scr[buf] = _dot(ckv_scr[...], wkb_ref[pair])

    def finish(pair, buf):
        v_ref[pair] = _dot(ckv_scr[...], wvb_ref[pair]).astype(BF16)
        qg_n, qg_pe = qg_ref[:, :LANES], qg_ref[:, LANES:]
        kg_n = kg_ref[:, :LANES]
        for c in range(tm // PROJ_CHUNK):
            rows = slice(c * PROJ_CHUNK, (c + 1) * PROJ_CHUNK)
            cos = cos_ref[rows, :]
            sin = sin_ref[rows, :]
            kpe_ss = kss_scr[rows, :]
            kpe_rot = kpe_scr[rows, :]
            for hh in range(2):
                qn = q_scr[buf, rows, hh * QK_SLOT:hh * QK_SLOT + LANES]
                qp = q_scr[buf, rows, hh * QK_SLOT + LANES:(hh + 1) * QK_SLOT]
                ss = jnp.sum(qn * qn + qp * qp, axis=-1, keepdims=True)
                rs = lax.rsqrt(ss * (1.0 / QK_HEAD) + NORM_EPS) * scale
                qpg = qp * qg_pe
                q_ref[pair, rows, hh * QK_SLOT:hh * QK_SLOT + LANES] = (qn * rs * qg_n).astype(BF16)
                q_ref[pair, rows, hh * QK_SLOT + LANES:(hh + 1) * QK_SLOT] = (
                    (qpg * cos + _rot_half(qpg) * sin) * rs).astype(BF16)
                kn = kn_scr[buf, rows, hh * LANES:(hh + 1) * LANES]
                ssk = jnp.sum(kn * kn, axis=-1, keepdims=True) + kpe_ss
                rsk = lax.rsqrt(ssk * (1.0 / QK_HEAD) + NORM_EPS)
                k_ref[pair, rows, hh * QK_SLOT:hh * QK_SLOT + LANES] = (kn * rsk * kg_n).astype(BF16)
                k_ref[pair, rows, hh * QK_SLOT + LANES:(hh + 1) * QK_SLOT] = (kpe_rot * rsk).astype(BF16)

    matmuls(0, 0)

    def body(jj, carry):
        pair = 2 * jj
        matmuls(pair + 1, 1)
        finish(pair, 0)
        matmuls(jnp.minimum(pair + 2, n_pairs - 1), 0)
        finish(pair + 1, 1)
        return carry

    lax.fori_loop(0, n_pairs // 2, body, 0)


def _rope_tile_cols(a):
    z = jnp.zeros(a.shape[:-1] + (ROPE_HALF_A,), a.dtype)
    return jnp.concatenate([a[..., :ROPE_HALF_A], z, a[..., ROPE_HALF_A:], z], axis=-1)


def _mla_proj(x2d, gain, w_in, qa_norm, kva_norm, w_qb, w_kvb, q_norm, k_norm, cos_a, sin_a):
    T = x2d.shape[0]
    tm = 512
    w_in_p = jnp.concatenate(
        [w_in[:, :Q_LORA + KV_LORA], _rope_tile_cols(w_in[:, Q_LORA + KV_LORA:])], axis=-1).astype(BF16)
    wq = w_qb.reshape(Q_LORA, H_A, QK_HEAD)
    wq_p = jnp.concatenate([wq[..., :QK_NOPE], _rope_tile_cols(wq[..., QK_NOPE:])], axis=-1)
    n_pairs = H_A // 2

    def pair_slabs(w2d):
        k_dim = w2d.shape[0]
        return w2d.reshape(k_dim, n_pairs, -1).transpose(1, 0, 2).astype(BF16)

    wq_p = pair_slabs(wq_p.reshape(Q_LORA, H_A * QK_SLOT))
    wkv = w_kvb.reshape(KV_LORA, H_A, QK_NOPE + V_HEAD)
    wkb = pair_slabs(wkv[..., :QK_NOPE].reshape(KV_LORA, H_A * QK_NOPE))
    wvb = pair_slabs(wkv[..., QK_NOPE:].reshape(KV_LORA, H_A * V_HEAD))
    qg = jnp.concatenate([q_norm[:QK_NOPE], _rope_tile_cols(q_norm[QK_NOPE:])]).reshape(1, QK_SLOT)
    kg = jnp.concatenate([k_norm[:QK_NOPE], _rope_tile_cols(k_norm[QK_NOPE:])]).reshape(1, QK_SLOT)

    def const(shape):
        return pl.BlockSpec(shape, lambda i: (0,) * len(shape))

    def rows(width):
        return pl.BlockSpec((tm, width), lambda i: (i, 0))

    def pair_rows(width):
        return pl.BlockSpec((n_pairs, tm, width), lambda i: (0, i, 0))

    n_in = Q_LORA + KV_LORA + LANES
    return pl.pallas_call(
        functools.partial(_mla_proj_kernel, scale=QK_HEAD ** -0.5 * LOG2E),
        out_shape=(jax.ShapeDtypeStruct((n_pairs, T, 2 * QK_SLOT), BF16),
                   jax.ShapeDtypeStruct((n_pairs, T, 2 * QK_SLOT), BF16),
                   jax.ShapeDtypeStruct((n_pairs, T, 2 * V_HEAD), BF16)),
        grid=(T // tm,),
        in_specs=[rows(D_MODEL), const((1, D_MODEL)), const((D_MODEL, n_in)), const((1, Q_LORA)),
                  const((1, KV_LORA)), const((n_pairs, Q_LORA, 2 * QK_SLOT)), const((n_pairs, KV_LORA, 2 * QK_NOPE)),
                  const((n_pairs, KV_LORA, 2 * V_HEAD)), const((1, QK_SLOT)), const((1, QK_SLOT)),
                  rows(LANES), rows(LANES)],
        out_specs=(pair_rows(2 * QK_SLOT), pair_rows(2 * QK_SLOT), pair_rows(2 * V_HEAD)),
        scratch_shapes=[pltpu.VMEM((tm, Q_LORA), BF16), pltpu.VMEM((tm, KV_LORA), BF16),
                        pltpu.VMEM((tm, LANES), F32), pltpu.VMEM((tm, LANES), F32),
                        pltpu.VMEM((2, tm, 2 * QK_SLOT), F32), pltpu.VMEM((2, tm, 2 * QK_NOPE), F32)],
        compiler_params=_params(("parallel",)),
        name="mla_proj",
    )(x2d, gain.reshape(1, D_MODEL), w_in_p, qa_norm.reshape(1, Q_LORA), kva_norm.reshape(1, KV_LORA),
      wq_p, wkb, wvb, qg, kg, cos_a, sin_a)


FLASH_HEADS = 8


FLASH_ROWS = 64


def _flash_kernel(q_ref, k_ref, v_ref, o_ref, s_scr, p_scr, m_scr, l_scr, a_scr, acc_scr, *, tq, tk):
    i = pl.program_id(2)
    m_scr[...] = jnp.full(m_scr.shape, NEG_INF, F32)
    l_scr[...] = jnp.zeros(l_scr.shape, F32)
    acc_scr[...] = jnp.zeros(acc_scr.shape, F32)

    half = tq // 2

    def scores(j, buf, masked):
        off = pl.multiple_of(j * tk, tk)
        for h in range(FLASH_HEADS):
            qk_cols = slice((h % 2) * QK_SLOT, (h % 2 + 1) * QK_SLOT)
            if masked:
                s_scr[buf, h, :half, :half] = _dot_nt(q_ref[h // 2, :half, qk_cols],
                                                      k_ref[h // 2, pl.ds(off, half), qk_cols])
                s_scr[buf, h, half:, :] = _dot_nt(q_ref[h // 2, half:, qk_cols],
                                                  k_ref[h // 2, pl.ds(off, tk), qk_cols])
            else:
                s_scr[buf, h] = _dot_nt(q_ref[h // 2, :, qk_cols], k_ref[h // 2, pl.ds(off, tk), qk_cols])

    def softmax_pv(j, buf, masked):
        off = pl.multiple_of(j * tk, tk)
        for h in range(FLASH_HEADS):
            v_cols = slice((h % 2) * V_HEAD, (h % 2 + 1) * V_HEAD)
            for rb in range(tq // FLASH_ROWS):
                rows = slice(rb * FLASH_ROWS, (rb + 1) * FLASH_ROWS)
                cw = min(tk, -(-((rb + 1) * FLASH_ROWS) // LANES) * LANES) if masked else tk
                pw = (half if (rb + 1) * FLASH_ROWS <= half else tk) if masked else tk
                s = s_scr[buf, h, rows, :cw]
                if masked:
                    row = lax.broadcasted_iota(jnp.int32, (FLASH_ROWS, cw), 0) + rb * FLASH_ROWS
                    col = lax.broadcasted_iota(jnp.int32, (FLASH_ROWS, cw), 1)
                    s = jnp.where(row >= col, s, NEG_INF)
                m_old = m_scr[h, rows, :]
                m_new = jnp.maximum(m_old, jnp.max(s, axis=-1, keepdims=True))
                p = jnp.exp2(s - jnp.tile(m_new, (1, cw // LANES)))
                alpha = jnp.exp2(m_old - m_new)
                l_scr[h, rows, :] = alpha * l_scr[h, rows, :] + jnp.sum(p, axis=-1, keepdims=True)
                m_scr[h, rows, :] = m_new
                a_scr[h, rows, :] = alpha
                p_scr[h, rows, :cw] = p.astype(BF16)
                if cw < pw:
                    p_scr[h, rows, cw:pw] = jnp.zeros((FLASH_ROWS, pw - cw), BF16)
            if masked:
                pv = _dot(p_scr[h, :half, :half], v_ref[h // 2, pl.ds(off, half), v_cols])
                acc_scr[h, :half] = a_scr[h, :half] * acc_scr[h, :half] + pv
                pv = _dot(p_scr[h, half:, :], v_ref[h // 2, pl.ds(off, tk), v_cols])
                acc_scr[h, half:] = a_scr[h, half:] * acc_scr[h, half:] + pv
            else:
                pv = _dot(p_scr[h], v_ref[h // 2, pl.ds(off, tk), v_cols])
                acc_scr[h] = a_scr[h] * acc_scr[h] + pv

    def body(j, carry):
        scores(j, 0, False)
        softmax_pv(j, 0, False)
        return carry

    assert tq == tk
    lax.fori_loop(0, i, body, 0)
    scores(i, 0, True)
    softmax_pv(i, 0, True)
    for h in range(FLASH_HEADS):
        o_ref[:, h * V_HEAD:(h + 1) * V_HEAD] = (acc_scr[h] / l_scr[h]).astype(o_ref.dtype)


def _mla_attention(q, k, v, batch, seq):
    tq = tk = 512
    nq = seq // tq
    T = q.shape[1]
    hh = FLASH_HEADS
    assert q.shape[0] * 2 == H_A and hh % 2 == 0
    stat = pltpu.VMEM((hh, tq, LANES), F32)
    return pl.pallas_call(
        functools.partial(_flash_kernel, tq=tq, tk=tk),
        out_shape=jax.ShapeDtypeStruct((T, H_A * V_HEAD), BF16),
        grid=(batch, H_A // hh, nq),
        in_specs=[pl.BlockSpec((hh // 2, tq, 2 * QK_SLOT), lambda b, h, i: (h, b * nq + i, 0)),
                  pl.BlockSpec((hh // 2, seq, 2 * QK_SLOT), lambda b, h, i: (h, b, 0)),
                  pl.BlockSpec((hh // 2, seq, 2 * V_HEAD), lambda b, h, i: (h, b, 0))],
        out_specs=pl.BlockSpec((tq, hh * V_HEAD), lambda b, h, i: (b * nq + i, h)),
        scratch_shapes=[pltpu.VMEM((1, hh, tq, tk), F32), pltpu.VMEM((hh, tq, tk), BF16), stat, stat, stat,
                        pltpu.VMEM((hh, tq, V_HEAD), F32)],
        compiler_params=_params(("parallel", "parallel", "arbitrary")),
        name="mla_flash",
    )(q, k, v)


def _out_proj_kernel(o_ref, w_ref, x_ref, y_ref):
    y_ref[...] = x_ref[...] + _dot(o_ref[...], w_ref[...])


def _out_proj(o, w_o, x2d):
    T, K = o.shape
    tm = 1024
    return pl.pallas_call(
        _out_proj_kernel,
        out_shape=jax.ShapeDtypeStruct((T, D_MODEL), F32),
        grid=(T // tm,),
        in_specs=[pl.BlockSpec((tm, K), lambda i: (i, 0)),
                  pl.BlockSpec((K, D_MODEL), lambda i: (0, 0)),
                  pl.BlockSpec((tm, D_MODEL), lambda i: (i, 0))],
        out_specs=pl.BlockSpec((tm, D_MODEL), lambda i: (i, 0)),
        compiler_params=_params(("parallel",)),
        name="out_proj",
    )(o, w_o.astype(BF16), x2d)


def _mlp_kernel(x_ref, g_ref, w1_ref, w2_ref, y_ref, xn_ref, acc_ref):
    f = pl.program_id(1)

    @pl.when(f == 0)
    def _():
        xn_ref[...] = _rms(x_ref[...], g_ref[...]).astype(BF16)
        acc_ref[...] = jnp.zeros_like(acc_ref)

    h = jnp.maximum(_dot(xn_ref[...], w1_ref[...]), 0.0)
    acc_ref[...] += _dot((h * h).astype(BF16), w2_ref[...])

    @pl.when(f == pl.num_programs(1) - 1)
    def _():
        y_ref[...] = x_ref[...] + acc_ref[...]


def _mlp(x2d, gain, w1, w2):
    T = x2d.shape[0]
    tm, tf = 1024, 2048
    return pl.pallas_call(
        _mlp_kernel,
        out_shape=jax.ShapeDtypeStruct((T, D_MODEL), F32),
        grid=(T // tm, D_FF // tf),
        in_specs=[pl.BlockSpec((tm, D_MODEL), lambda i, f: (i, 0)),
                  pl.BlockSpec((1, D_MODEL), lambda i, f: (0, 0)),
                  pl.BlockSpec((D_MODEL, tf), lambda i, f: (0, f)),
                  pl.BlockSpec((tf, D_MODEL), lambda i, f: (f, 0))],
        out_specs=pl.BlockSpec((tm, D_MODEL), lambda i, f: (i, 0)),
        scratch_shapes=[pltpu.VMEM((tm, D_MODEL), BF16), pltpu.VMEM((tm, D_MODEL), F32)],
        compiler_params=_params(("parallel", "arbitrary")),
        name="mlp",
    )(x2d, gain.reshape(1, D_MODEL), w1.astype(BF16), w2.astype(BF16))


def _group_proj_kernel(*refs, has_v, scale):
    if has_v:
        x_ref, g_ref, w_ref, hg_ref, cos_ref, sin_ref = refs[:6]
        outs = refs[6:12]
        xs_ref, xn_ref, y_ref = refs[12:]
    else:
        x_ref, g_ref, w_ref, hg_ref, cos_ref, sin_ref = refs[:6]
        outs = refs[6:9]
        xs_ref, xn_ref, y_ref = refs[9:]
    tm = x_ref.shape[0]
    n_col = D_MODEL // LANES
    n_chunks = tm // PROJ_CHUNK
    assert n_chunks == SLABS
    xn = _rms(x_ref[...], g_ref[...])
    for c in range(n_col):
        xs_ref[c] = xn[:, c * LANES:(c + 1) * LANES]

    def permute(gi, buf):
        d = DILATED_GROUPS[gi][1]
        rows = tm // d
        if d == 1:
            xn_ref[buf] = xn.astype(BF16)
        else:
            for r in range(d):
                for c in range(n_col):
                    xn_ref[buf, r * rows:(r + 1) * rows, c * LANES:(c + 1) * LANES] = (
                        xs_ref[c, pl.ds(r, rows, stride=d), :].astype(BF16))

    def norm_chunk(gi, buf, ci):
        d = DILATED_GROUPS[gi][1]
        rows = tm // d
        out_ref = outs[gi]
        r0 = pl.multiple_of(ci * PROJ_CHUNK, PROJ_CHUNK)

        def table_rows(t_ref):
            if d == 1:
                return t_ref[pl.ds(r0, PROJ_CHUNK), :]
            if rows >= PROJ_CHUNK:
                per = rows // PROJ_CHUNK
                return t_ref[pl.ds((ci % per) * PROJ_CHUNK * d + ci // per, PROJ_CHUNK, stride=d), :]
            per = PROJ_CHUNK // rows
            return jnp.concatenate([t_ref[pl.ds(ci * per + s, rows, stride=d), :] for s in range(per)], axis=0)

        cos = table_rows(cos_ref)
        sin = table_rows(sin_ref)
        for h in range(H_B):
            hs = slice(h * HEAD_DIM_B, (h + 1) * HEAD_DIM_B)
            sl = h * HEAD_DIM_B // SLAB_W
            lo = h * HEAD_DIM_B % SLAB_W
            yh = y_ref[buf, sl, pl.ds(r0, PROJ_CHUNK), lo:lo + HEAD_DIM_B]
            rs = lax.rsqrt(jnp.mean(yh * yh, axis=-1, keepdims=True) + NORM_EPS) * scale
            yg = yh * hg_ref[:, gi * C_B + h * HEAD_DIM_B:gi * C_B + (h + 1) * HEAD_DIM_B]
            res = ((yg * cos + _rot_half(yg) * sin) * rs).astype(BF16)
            if rows >= PROJ_CHUNK:
                per = rows // PROJ_CHUNK
                out_ref[0, ci // per, pl.ds(pl.multiple_of((ci % per) * PROJ_CHUNK, PROJ_CHUNK), PROJ_CHUNK),
                        hs] = res
            else:
                per = PROJ_CHUNK // rows
                for s in range(per):
                    out_ref[0, ci * per + s, :, hs] = res[s * rows:(s + 1) * rows]

    def v_slab(gi, buf, ci):
        d = DILATED_GROUPS[gi][1]
        rows = tm // d
        yv = _dot(xn_ref[buf], w_ref[(N_GROUPS + gi) * SLABS + ci])
        for r in range(d):
            outs[N_GROUPS + gi][0, r, ci] = yv[r * rows:(r + 1) * rows].astype(BF16)

    permute(0, 0)
    for s in range(SLABS):
        y_ref[0, s] = _dot(xn_ref[0], w_ref[s])
    for gi in range(N_GROUPS):
        buf = gi % 2
        if gi + 1 < N_GROUPS:
            permute(gi + 1, 1 - buf)

        def body(ci, carry, gi=gi, buf=buf):
            norm_chunk(gi, buf, ci)
            if gi + 1 < N_GROUPS:
                y_ref[1 - buf, ci] = _dot(xn_ref[1 - buf], w_ref[(gi + 1) * SLABS + ci])
            if has_v:
                v_slab(gi, buf, ci)
            return carry

        lax.fori_loop(0, n_chunks, body, 0, unroll=2 if has_v else 1)


def _group_proj(x2d, gain, w, head_gain, has_v, scale, cos_b, sin_b, batch, seq):
    tm = PROJ_TILE
    nt = seq // tm
    n_rope = N_GROUPS * C_B
    n_slabs = w.shape[1] // SLAB_W
    w_slabs = w.astype(BF16).reshape(D_MODEL, n_slabs, SLAB_W).transpose(1, 0, 2)
    in_specs = [pl.BlockSpec((tm, D_MODEL), lambda i: (i, 0)),
                pl.BlockSpec((1, D_MODEL), lambda i: (0, 0)),
                pl.BlockSpec((n_slabs, D_MODEL, SLAB_W), lambda i: (0, 0, 0), pipeline_mode=pl.Buffered(1)),
                pl.BlockSpec((1, n_rope), lambda i: (0, 0)),
                pl.BlockSpec((tm, LANES), lambda i: (i, 0)),
                pl.BlockSpec((tm, LANES), lambda i: (i, 0))]
    shapes, specs = [], []
    for _, d in DILATED_GROUPS:
        shapes.append(jax.ShapeDtypeStruct((batch, d, seq // d, C_B), BF16))
        specs.append(pl.BlockSpec((1, d, tm // d, C_B), lambda i: (i // nt, 0, i % nt, 0)))
    if has_v:
        for _, d in DILATED_GROUPS:
            shapes.append(jax.ShapeDtypeStruct((batch, d, SLABS, seq // d, SLAB_W), BF16))
            specs.append(pl.BlockSpec((1, d, SLABS, tm // d, SLAB_W), lambda i: (i // nt, 0, 0, i % nt, 0)))
    return pl.pallas_call(
        functools.partial(_group_proj_kernel, has_v=has_v, scale=scale),
        out_shape=tuple(shapes),
        grid=(batch * nt,),
        in_specs=in_specs,
        out_specs=tuple(specs),
        scratch_shapes=[pltpu.VMEM((D_MODEL // LANES, tm, LANES), F32), pltpu.VMEM((2, tm, D_MODEL), BF16),
                        pltpu.VMEM((2, SLABS, tm, SLAB_W), F32)],
        compiler_params=_params(("parallel",)),
        name="group_proj_kv" if has_v else "group_proj_q",
    )(x2d, gain.reshape(1, D_MODEL), w_slabs, head_gain.reshape(1, n_rope), cos_b, sin_b)


BAND_HEADS = 8


def _band_kernel(q_ref, k_ref, v_ref, o_ref, lse_ref, *scratch, dilation, length):
    o_scr = scratch[0] if dilation > 1 else None
    nk = min(2 * BAND, length)
    row = lax.broadcasted_iota(jnp.int32, (BAND, nk), 0)
    col = lax.broadcasted_iota(jnp.int32, (BAND, nk), 1)
    lane_head = lax.broadcasted_iota(jnp.int32, (BAND, LANES), 1) // (LANES // BAND_HEADS)

    def block(r, i):
        q0 = i * BAND
        start = jnp.maximum(q0 + BAND - nk, 0)
        dist = (q0 - start) + row - col
        valid = jnp.logical_and(dist >= 0, dist <= BAND)
        if not isinstance(q0, int):
            q0 = pl.multiple_of(q0, BAND)
            start = pl.multiple_of(start, BAND)
        scores = []
        for h in range(BAND_HEADS):
            hs = slice(h * HEAD_DIM_B, (h + 1) * HEAD_DIM_B)
            s = _dot_nt(q_ref[0, r, pl.ds(q0, BAND), hs], k_ref[0, r, pl.ds(start, nk), hs])
            scores.append(jnp.where(valid, s, NEG_INF))
        probs, denoms, lse_tile = [], [], jnp.zeros((BAND, LANES), F32)
        for h in range(BAND_HEADS):
            m = jnp.max(scores[h], axis=-1, keepdims=True)
            p = jnp.exp2(scores[h] - m)
            denom = jnp.sum(p, axis=-1, keepdims=True)
            probs.append(p.astype(BF16))
            denoms.append(denom)
            lse_tile = jnp.where(lane_head == h, m + jnp.log2(denom), lse_tile)
        tok = pl.ds(q0 * dilation + r, BAND, stride=dilation) if dilation > 1 else pl.ds(q0, BAND)
        for h in range(BAND_HEADS):
            hs = slice(h * HEAD_DIM_B, (h + 1) * HEAD_DIM_B)
            lo = h * HEAD_DIM_B % SLAB_W
            out = _dot(probs[h], v_ref[0, r, h * HEAD_DIM_B // SLAB_W, pl.ds(start, nk), lo:lo + HEAD_DIM_B])
            if dilation > 1:
                o_scr[h, tok, :] = out / denoms[h]
            else:
                o_ref[0, h, tok, :] = (out / denoms[h]).astype(o_ref.dtype)
        lse_ref[0, 0, tok, :] = lse_tile

    nb = length // BAND
    for r in range(dilation):
        if nb == 1:
            block(r, 0)
        else:
            def body(i, carry, r=r):
                block(r, i)
                return carry
            lax.fori_loop(0, nb, body, 0, unroll=4)
    if dilation > 1:
        for h in range(BAND_HEADS):
            o_ref[0, h] = o_scr[h].astype(o_ref.dtype)


def _band_attention(q, k, v, dilation, batch, seq):
    length = seq // dilation
    halves = H_B // BAND_HEADS
    cw = BAND_HEADS * HEAD_DIM_B
    blk = pl.BlockSpec((1, dilation, length, cw), lambda b, hh: (b, 0, 0, hh))
    v_blk = pl.BlockSpec((1, dilation, cw // SLAB_W, length, SLAB_W), lambda b, hh: (b, 0, hh, 0, 0))
    return pl.pallas_call(
        functools.partial(_band_kernel, dilation=dilation, length=length),
        out_shape=(jax.ShapeDtypeStruct((batch, H_B, seq, HEAD_DIM_B), BF16),
                   jax.ShapeDtypeStruct((batch, halves, seq, LANES), F32)),
        grid=(batch, halves),
        in_specs=[blk, blk, v_blk],
        out_specs=(pl.BlockSpec((1, BAND_HEADS, seq, HEAD_DIM_B), lambda b, hh: (b, hh, 0, 0)),
                   pl.BlockSpec((1, 1, seq, LANES), lambda b, hh: (b, hh, 0, 0))),
        scratch_shapes=[pltpu.VMEM((BAND_HEADS, seq, HEAD_DIM_B), F32)] if dilation > 1 else [],
        compiler_params=_params(("parallel", "parallel")),
        name=f"band_attn_d{dilation}",
    )(q, k, v)


def _combine_proj_kernel(o0_ref, o1_ref, o2_ref, l0_ref, l1_ref, l2_ref, w_ref, x_ref, y_ref, o_scr):
    lanes_per_head = LANES // BAND_HEADS
    for half in range(H_B // BAND_HEADS):
        l0, l1, l2 = l0_ref[0, half], l1_ref[0, half], l2_ref[0, half]
        mx = jnp.maximum(jnp.maximum(l0, l1), l2)
        e0, e1, e2 = jnp.exp2(l0 - mx), jnp.exp2(l1 - mx), jnp.exp2(l2 - mx)
        inv = 1.0 / (e0 + e1 + e2)
        w0, w1, w2 = e0 * inv, e1 * inv, e2 * inv
        for hq in range(BAND_HEADS):
            h = half * BAND_HEADS + hq
            c = hq * lanes_per_head
            o = (w0[:, c:c + 1] * o0_ref[0, h].astype(F32) + w1[:, c:c + 1] * o1_ref[0, h].astype(F32)
                 + w2[:, c:c + 1] * o2_ref[0, h].astype(F32))
            o_scr[:, h * HEAD_DIM_B:(h + 1) * HEAD_DIM_B] = o.astype(BF16)
    y_ref[...] = x_ref[...] + _dot(o_scr[...], w_ref[...])


def _combine_proj(outs, lses, w_o, x2d, seq):
    T = x2d.shape[0]
    tm = 1024
    nt = seq // tm
    halves = H_B // BAND_HEADS
    o_spec = pl.BlockSpec((1, H_B, tm, HEAD_DIM_B), lambda i: (i // nt, 0, i % nt, 0))
    l_spec = pl.BlockSpec((1, halves, tm, LANES), lambda i: (i // nt, 0, i % nt, 0))
    return pl.pallas_call(
        _combine_proj_kernel,
        out_shape=jax.ShapeDtypeStruct((T, D_MODEL), F32),
        grid=(T // tm,),
        in_specs=[o_spec, o_spec, o_spec, l_spec, l_spec, l_spec,
                  pl.BlockSpec((C_B, D_MODEL), lambda i: (0, 0)),
                  pl.BlockSpec((tm, D_MODEL), lambda i: (i, 0))],
        out_specs=pl.BlockSpec((tm, D_MODEL), lambda i: (i, 0)),
        scratch_shapes=[pltpu.VMEM((tm, C_B), BF16)],
        compiler_params=_params(("parallel",)),
        name="combine_proj",
    )(*outs, *lses, w_o.astype(BF16), x2d)


def kernel(x, positions, attn_norm, mlp_norm, mla_w_in, mla_qa_norm, mla_kva_norm, mla_w_qb, mla_w_kvb,
           mla_q_norm, mla_k_norm, mla_w_o, kv_norm, w_kv, k_norm_b, w_q_b, q_norm_b, w_o_b, mlp_w1, mlp_w2):
    B, S, D = x.shape
    T = B * S
    cos_a, sin_a, cos_b, sin_b = _rope_tables(positions)
    h = x.reshape(T, D)

    def head_gains(gn):
        return jnp.broadcast_to(gn[:, None, :], (N_GROUPS, H_B, HEAD_DIM_B))

    for a in range(N_A_LAYERS):
        q, k, v = _mla_proj(h, attn_norm[a], mla_w_in[a], mla_qa_norm[a], mla_kva_norm[a], mla_w_qb[a],
                            mla_w_kvb[a], mla_q_norm[a], mla_k_norm[a], cos_a, sin_a)
        o = _mla_attention(q, k, v, B, S)
        h = _out_proj(o, mla_w_o[a], h)
        h = _mlp(h, mlp_norm[a], mlp_w1[a], mlp_w2[a])

    kv = _group_proj(h, kv_norm, w_kv, head_gains(k_norm_b), True, 1.0, cos_b, sin_b, B, S)
    ks, vs = kv[:N_GROUPS], kv[N_GROUPS:]

    for b in range(N_B_LAYERS):
        layer = N_A_LAYERS + b
        qs = _group_proj(h, attn_norm[layer], w_q_b[b], head_gains(q_norm_b[b]), False,
                         HEAD_DIM_B ** -0.5 * LOG2E, cos_b, sin_b, B, S)
        outs, lses = [], []
        for g, (window, dilation) in enumerate(DILATED_GROUPS):
            assert window // dilation == BAND
            o, lse = _band_attention(qs[g], ks[g], vs[g], dilation, B, S)
            outs.append(o)
            lses.append(lse)
        h = _combine_proj(outs, lses, w_o_b[b], h, S)
        h = _mlp(h, mlp_norm[layer], mlp_w1[layer], mlp_w2[layer])

    return h.reshape(B, S, D)
```

```python
import functools

import jax
import jax.numpy as jnp
from jax import lax
from jax.experimental import pallas as pl
from jax.experimental.pallas import tpu as pltpu

D_MODEL = 1024
N_A_LAYERS = 2
N_B_LAYERS = 2
H_A = 16
QK_NOPE = 128
QK_ROPE = 64
QK_HEAD = QK_NOPE + QK_ROPE
ROPE_HALF_A = QK_ROPE // 2
V_HEAD = 128
Q_LORA = 256
KV_LORA = 128
DILATED_GROUPS = ((128, 1), (512, 4), (2048, 16))
N_GROUPS = 3
H_B = 8
HEAD_DIM_B = 128
C_B = H_B * HEAD_DIM_B
D_FF = 4 * D_MODEL
ROPE_THETA = 10000.0
NORM_EPS = 1e-6
NEG_INF = -1e30
LOG2E = 1.4426950408889634

LANES = 128
QK_SLOT = 2 * LANES
VMEM_LIMIT = 56 * 1024 * 1024
BAND = 128
PROJ_TILE = 512
PROJ_CHUNK = 128
GROUP_CHUNK = 256
SLAB_W = 2 * LANES
SLABS = C_B // SLAB_W

BF16 = jnp.bfloat16
F32 = jnp.float32


def _params(semantics):
    return pltpu.CompilerParams(dimension_semantics=semantics, vmem_limit_bytes=VMEM_LIMIT)


def _rms(x, gain):
    ms = jnp.mean(x * x, axis=-1, keepdims=True)
    return x * lax.rsqrt(ms + NORM_EPS) * gain


def _rot_half(u):
    return pltpu.roll(u, LANES // 2, axis=1)


def _dot(a, b):
    return jnp.dot(a, b, preferred_element_type=F32)


def _dot_nt(a, b):
    return lax.dot_general(a, b, (((1,), (1,)), ((), ())), preferred_element_type=F32)


def _tables_kernel(pos_ref, f_ref, cm_ref, sm_ref, cos_ref, sin_ref):
    ang = pos_ref[...] * f_ref[...]
    cos_ref[...] = jnp.cos(ang) * cm_ref[...]
    sin_ref[...] = jnp.sin(ang) * sm_ref[...]


def _rope_table(pos, freq, cos_mask, sin_sign):
    n = pos.shape[0]
    tm = 1024
    row = pl.BlockSpec((1, LANES), lambda i: (0, 0))
    tab = pl.BlockSpec((tm, LANES), lambda i: (i, 0))
    shp = jax.ShapeDtypeStruct((n, LANES), F32)
    return pl.pallas_call(
        _tables_kernel,
        out_shape=(shp, shp),
        grid=(n // tm,),
        in_specs=[pl.BlockSpec((tm, 1), lambda i: (i, 0)), row, row, row],
        out_specs=(tab, tab),
        compiler_params=_params(("parallel",)),
        name="rope_table",
    )(pos.reshape(n, 1), freq.reshape(1, LANES), cos_mask.reshape(1, LANES), sin_sign.reshape(1, LANES))


def _rope_tables(positions):
    B, S = positions.shape
    pos = positions.astype(F32)
    inv_a = ROPE_THETA ** (-jnp.arange(0, QK_ROPE, 2, dtype=F32) / QK_ROPE)
    inv_b = ROPE_THETA ** (-jnp.arange(0, HEAD_DIM_B, 2, dtype=F32) / HEAD_DIM_B)
    za = jnp.zeros((ROPE_HALF_A,), F32)
    oa = jnp.ones((ROPE_HALF_A,), F32)
    ob = jnp.ones((HEAD_DIM_B // 2,), F32)
    cos_a, sin_a = _rope_table(pos.reshape(B * S), jnp.concatenate([inv_a, za, inv_a, za]),
                               jnp.concatenate([oa, za, oa, za]), jnp.concatenate([-oa, za, oa, za]))
    cos_b, sin_b = _rope_table(pos.reshape(B * S), jnp.concatenate([inv_b, inv_b]),
                               jnp.concatenate([ob, ob]), jnp.concatenate([-ob, ob]))
    return cos_a, sin_a, cos_b, sin_b


def _mla_proj_kernel(x_ref, g_ref, win_ref, qa_ref, kva_ref, wqb_ref, wkb_ref, wvb_ref, qg_ref, kg_ref,
                     cos_ref, sin_ref, q_ref, k_ref, v_ref, cq_scr, ckv_scr, kpe_scr, kss_scr, q_scr, kn_scr,
                     *, scale):
    tm = x_ref.shape[0]
    n_pairs = H_A // 2
    xn = _rms(x_ref[...], g_ref[...]).astype(BF16)
    lat = _dot(xn, win_ref[...])
    cq_scr[...] = _rms(lat[:, :Q_LORA], qa_ref[...]).astype(BF16)
    ckv_scr[...] = _rms(lat[:, Q_LORA:Q_LORA + KV_LORA], kva_ref[...]).astype(BF16)
    k_pe = lat[:, Q_LORA + KV_LORA:]
    kss_scr[...] = jnp.broadcast_to(jnp.sum(k_pe * k_pe, axis=-1, keepdims=True), (tm, LANES))
    kpe_g = k_pe * kg_ref[:, LANES:]
    kpe_scr[...] = kpe_g * cos_ref[...] + _rot_half(kpe_g) * sin_ref[...]

    def matmuls(pair, buf):
        q_scr[buf] = _dot(cq_scr[...], wqb_ref[pair])
        kn_scr[buf] = _dot(ckv_scr[...], wkb_ref[pair])

    def finish(pair, buf):
        v_ref[pair] = _dot(ckv_scr[...], wvb_ref[pair]).astype(BF16)
        qg_n, qg_pe = qg_ref[:, :LANES], qg_ref[:, LANES:]
        kg_n = kg_ref[:, :LANES]
        for c in range(tm // PROJ_CHUNK):
            rows = slice(c * PROJ_CHUNK, (c + 1) * PROJ_CHUNK)
            cos = cos_ref[rows, :]
            sin = sin_ref[rows, :]
            kpe_ss = kss_scr[rows, :]
            kpe_rot = kpe_scr[rows, :]
            for hh in range(2):
                qn = q_scr[buf, rows, hh * QK_SLOT:hh * QK_SLOT + LANES]
                qp = q_scr[buf, rows, hh * QK_SLOT + LANES:(hh + 1) * QK_SLOT]
                ss = jnp.sum(qn * qn + qp * qp, axis=-1, keepdims=True)
                rs = lax.rsqrt(ss * (1.0 / QK_HEAD) + NORM_EPS) * scale
                qpg = qp * qg_pe
                q_ref[pair, rows, hh * QK_SLOT:hh * QK_SLOT + LANES] = (qn * rs * qg_n).astype(BF16)
                q_ref[pair, rows, hh * QK_SLOT + LANES:(hh + 1) * QK_SLOT] = (
                    (qpg * cos + _rot_half(qpg) * sin) * rs).astype(BF16)
                kn = kn_scr[buf, rows, hh * LANES:(hh + 1) * LANES]
                ssk = jnp.sum(kn * kn, axis=-1, keepdims=True) + kpe_ss
                rsk = lax.rsqrt(ssk * (1.0 / QK_HEAD) + NORM_EPS)
                k_ref[pair, rows, hh * QK_SLOT:hh * QK_SLOT + LANES] = (kn * rsk * kg_n).astype(BF16)
                k_ref[pair, rows, hh * QK_SLOT + LANES:(hh + 1) * QK_SLOT] = (kpe_rot * rsk).astype(BF16)

    matmuls(0, 0)

    def body(jj, carry):
        pair = 2 * jj
        matmuls(pair + 1, 1)
        finish(pair, 0)
        matmuls(jnp.minimum(pair + 2, n_pairs - 1), 0)
        finish(pair + 1, 1)
        return carry

    lax.fori_loop(0, n_pairs // 2, body, 0)


def _rope_tile_cols(a):
    z = jnp.zeros(a.shape[:-1] + (ROPE_HALF_A,), a.dtype)
    return jnp.concatenate([a[..., :ROPE_HALF_A], z, a[..., ROPE_HALF_A:], z], axis=-1)


def _mla_proj(x2d, gain, w_in, qa_norm, kva_norm, w_qb, w_kvb, q_norm, k_norm, cos_a, sin_a):
    T = x2d.shape[0]
    tm = 512
    w_in_p = jnp.concatenate(
        [w_in[:, :Q_LORA + KV_LORA], _rope_tile_cols(w_in[:, Q_LORA + KV_LORA:])], axis=-1).astype(BF16)
    wq = w_qb.reshape(Q_LORA, H_A, QK_HEAD)
    wq_p = jnp.concatenate([wq[..., :QK_NOPE], _rope_tile_cols(wq[..., QK_NOPE:])], axis=-1)
    n_pairs = H_A // 2

    def pair_slabs(w2d):
        k_dim = w2d.shape[0]
        return w2d.reshape(k_dim, n_pairs, -1).transpose(1, 0, 2).astype(BF16)

    wq_p = pair_slabs(wq_p.reshape(Q_LORA, H_A * QK_SLOT))
    wkv = w_kvb.reshape(KV_LORA, H_A, QK_NOPE + V_HEAD)
    wkb = pair_slabs(wkv[..., :QK_NOPE].reshape(KV_LORA, H_A * QK_NOPE))
    wvb = pair_slabs(wkv[..., QK_NOPE:].reshape(KV_LORA, H_A * V_HEAD))
    qg = jnp.concatenate([q_norm[:QK_NOPE], _rope_tile_cols(q_norm[QK_NOPE:])]).reshape(1, QK_SLOT)
    kg = jnp.concatenate([k_norm[:QK_NOPE], _rope_tile_cols(k_norm[QK_NOPE:])]).reshape(1, QK_SLOT)

    def const(shape):
        return pl.BlockSpec(shape, lambda i: (0,) * len(shape))

    def rows(width):
        return pl.BlockSpec((tm, width), lambda i: (i, 0))

    def pair_rows(width):
        return pl.BlockSpec((n_pairs, tm, width), lambda i: (0, i, 0))

    n_in = Q_LORA + KV_LORA + LANES
    return pl.pallas_call(
        functools.partial(_mla_proj_kernel, scale=QK_HEAD ** -0.5 * LOG2E),
        out_shape=(jax.ShapeDtypeStruct((n_pairs, T, 2 * QK_SLOT), BF16),
                   jax.ShapeDtypeStruct((n_pairs, T, 2 * QK_SLOT), BF16),
                   jax.ShapeDtypeStruct((n_pairs, T, 2 * V_HEAD), BF16)),
        grid=(T // tm,),
        in_specs=[rows(D_MODEL), const((1, D_MODEL)), const((D_MODEL, n_in)), const((1, Q_LORA)),
                  const((1, KV_LORA)), const((n_pairs, Q_LORA, 2 * QK_SLOT)), const((n_pairs, KV_LORA, 2 * QK_NOPE)),
                  const((n_pairs, KV_LORA, 2 * V_HEAD)), const((1, QK_SLOT)), const((1, QK_SLOT)),
                  rows(LANES), rows(LANES)],
        out_specs=(pair_rows(2 * QK_SLOT), pair_rows(2 * QK_SLOT), pair_rows(2 * V_HEAD)),
        scratch_shapes=[pltpu.VMEM((tm, Q_LORA), BF16), pltpu.VMEM((tm, KV_LORA), BF16),
                        pltpu.VMEM((tm, LANES), F32), pltpu.VMEM((tm, LANES), F32),
                        pltpu.VMEM((2, tm, 2 * QK_SLOT), F32), pltpu.VMEM((2, tm, 2 * QK_NOPE), F32)],
        compiler_params=_params(("parallel",)),
        name="mla_proj",
    )(x2d, gain.reshape(1, D_MODEL), w_in_p, qa_norm.reshape(1, Q_LORA), kva_norm.reshape(1, KV_LORA),
      wq_p, wkb, wvb, qg, kg, cos_a, sin_a)


FLASH_HEADS = 8


FLASH_ROWS = 64


def _flash_kernel(q_ref, k_ref, v_ref, o_ref, s_scr, p_scr, m_scr, l_scr, a_scr, acc_scr, *, tq, tk):
    i = pl.program_id(2)
    m_scr[...] = jnp.full(m_scr.shape, NEG_INF, F32)
    l_scr[...] = jnp.zeros(l_scr.shape, F32)
    acc_scr[...] = jnp.zeros(acc_scr.shape, F32)

    half = tq // 2

    def scores(j, buf, masked):
        off = pl.multiple_of(j * tk, tk)
        for h in range(FLASH_HEADS):
            qk_cols = slice((h % 2) * QK_SLOT, (h % 2 + 1) * QK_SLOT)
            if masked:
                s_scr[buf, h, :half, :half] = _dot_nt(q_ref[h // 2, :half, qk_cols],
                                                      k_ref[h // 2, pl.ds(off, half), qk_cols])
                s_scr[buf, h, half:, :] = _dot_nt(q_ref[h // 2, half:, qk_cols],
                                                  k_ref[h // 2, pl.ds(off, tk), qk_cols])
            else:
                s_scr[buf, h] = _dot_nt(q_ref[h // 2, :, qk_cols], k_ref[h // 2, pl.ds(off, tk), qk_cols])

    def softmax_pv(j, buf, masked):
        off = pl.multiple_of(j * tk, tk)
        for h in range(FLASH_HEADS):
            v_cols = slice((h % 2) * V_HEAD, (h % 2 + 1) * V_HEAD)
            for rb in range(tq // FLASH_ROWS):
                rows = slice(rb * FLASH_ROWS, (rb + 1) * FLASH_ROWS)
                cw = min(tk, -(-((rb + 1) * FLASH_ROWS) // LANES) * LANES) if masked else tk
                pw = (half if (rb + 1) * FLASH_ROWS <= half else tk) if masked else tk
                s = s_scr[buf, h, rows, :cw]
                if masked:
                    row = lax.broadcasted_iota(jnp.int32, (FLASH_ROWS, cw), 0) + rb * FLASH_ROWS
                    col = lax.broadcasted_iota(jnp.int32, (FLASH_ROWS, cw), 1)
                    s = jnp.where(row >= col, s, NEG_INF)
                m_old = m_scr[h, rows, :]
                m_new = jnp.maximum(m_old, jnp.max(s, axis=-1, keepdims=True))
                p = jnp.exp2(s - jnp.tile(m_new, (1, cw // LANES)))
                alpha = jnp.exp2(m_old - m_new)
                l_scr[h, rows, :] = alpha * l_scr[h, rows, :] + jnp.sum(p, axis=-1, keepdims=True)
                m_scr[h, rows, :] = m_new
                a_scr[h, rows, :] = alpha
                p_scr[h, rows, :cw] = p.astype(BF16)
                if cw < pw:
                    p_scr[h, rows, cw:pw] = jnp.zeros((FLASH_ROWS, pw - cw), BF16)
            if masked:
                pv = _dot(p_scr[h, :half, :half], v_ref[h // 2, pl.ds(off, half), v_cols])
                acc_scr[h, :half] = a_scr[h, :half] * acc_scr[h, :half] + pv
                pv = _dot(p_scr[h, half:, :], v_ref[h // 2, pl.ds(off, tk), v_cols])
                acc_scr[h, half:] = a_scr[h, half:] * acc_scr[h, half:] + pv
            else:
                pv = _dot(p_scr[h], v_ref[h // 2, pl.ds(off, tk), v_cols])
                acc_scr[h] = a_scr[h] * acc_scr[h] + pv

    def body(j, carry):
        scores(j, 0, False)
        softmax_pv(j, 0, False)
        return carry

    assert tq == tk
    lax.fori_loop(0, i, body, 0)
    scores(i, 0, True)
    softmax_pv(i, 0, True)
    for h in range(FLASH_HEADS):
        o_ref[:, h * V_HEAD:(h + 1) * V_HEAD] = (acc_scr[h] / l_scr[h]).astype(o_ref.dtype)


def _mla_attention(q, k, v, batch, seq):
    tq = tk = 512
    nq = seq // tq
    T = q.shape[1]
    hh = FLASH_HEADS
    assert q.shape[0] * 2 == H_A and hh % 2 == 0
    stat = pltpu.VMEM((hh, tq, LANES), F32)
    return pl.pallas_call(
        functools.partial(_flash_kernel, tq=tq, tk=tk),
        out_shape=jax.ShapeDtypeStruct((T, H_A * V_HEAD), BF16),
        grid=(batch, H_A // hh, nq),
        in_specs=[pl.BlockSpec((hh // 2, tq, 2 * QK_SLOT), lambda b, h, i: (h, b * nq + i, 0)),
                  pl.BlockSpec((hh // 2, seq, 2 * QK_SLOT), lambda b, h, i: (h, b, 0)),
                  pl.BlockSpec((hh // 2, seq, 2 * V_HEAD), lambda b, h, i: (h, b, 0))],
        out_specs=pl.BlockSpec((tq, hh * V_HEAD), lambda b, h, i: (b * nq + i, h)),
        scratch_shapes=[pltpu.VMEM((1, hh, tq, tk), F32), pltpu.VMEM((hh, tq, tk), BF16), stat, stat, stat,
                        pltpu.VMEM((hh, tq, V_HEAD), F32)],
        compiler_params=_params(("parallel", "parallel", "arbitrary")),
        name="mla_flash",
    )(q, k, v)


def _out_proj_kernel(o_ref, w_ref, x_ref, y_ref):
    y_ref[...] = x_ref[...] + _dot(o_ref[...], w_ref[...])


def _out_proj(o, w_o, x2d):
    T, K = o.shape
    tm = 1024
    return pl.pallas_call(
        _out_proj_kernel,
        out_shape=jax.ShapeDtypeStruct((T, D_MODEL), F32),
        grid=(T // tm,),
        in_specs=[pl.BlockSpec((tm, K), lambda i: (i, 0)),
                  pl.BlockSpec((K, D_MODEL), lambda i: (0, 0)),
                  pl.BlockSpec((tm, D_MODEL), lambda i: (i, 0))],
        out_specs=pl.BlockSpec((tm, D_MODEL), lambda i: (i, 0)),
        compiler_params=_params(("parallel",)),
        name="out_proj",
    )(o, w_o.astype(BF16), x2d)


def _mlp_kernel(x_ref, g_ref, w1_ref, w2_ref, y_ref, xn_ref, acc_ref):
    f = pl.program_id(1)

    @pl.when(f == 0)
    def _():
        xn_ref[...] = _rms(x_ref[...], g_ref[...]).astype(BF16)
        acc_ref[...] = jnp.zeros_like(acc_ref)

    h = jnp.maximum(_dot(xn_ref[...], w1_ref[...]), 0.0)
    acc_ref[...] += _dot((h * h).astype(BF16), w2_ref[...])

    @pl.when(f == pl.num_programs(1) - 1)
    def _():
        y_ref[...] = x_ref[...] + acc_ref[...]


def _mlp(x2d, gain, w1, w2):
    T = x2d.shape[0]
    tm, tf = 1024, 2048
    return pl.pallas_call(
        _mlp_kernel,
        out_shape=jax.ShapeDtypeStruct((T, D_MODEL), F32),
        grid=(T // tm, D_FF // tf),
        in_specs=[pl.BlockSpec((tm, D_MODEL), lambda i, f: (i, 0)),
                  pl.BlockSpec((1, D_MODEL), lambda i, f: (0, 0)),
                  pl.BlockSpec((D_MODEL, tf), lambda i, f: (0, f)),
                  pl.BlockSpec((tf, D_MODEL), lambda i, f: (f, 0))],
        out_specs=pl.BlockSpec((tm, D_MODEL), lambda i, f: (i, 0)),
        scratch_shapes=[pltpu.VMEM((tm, D_MODEL), BF16), pltpu.VMEM((tm, D_MODEL), F32)],
        compiler_params=_params(("parallel", "arbitrary")),
        name="mlp",
    )(x2d, gain.reshape(1, D_MODEL), w1.astype(BF16), w2.astype(BF16))


def _group_proj_kernel(*refs, has_v, scale):
    if has_v:
        x_ref, g_ref, w_ref, hg_ref, cos_ref, sin_ref = refs[:6]
        outs = refs[6:12]
        xs_ref, xn_ref, y_ref = refs[12:]
    else:
        x_ref, g_ref, w_ref, hg_ref, cos_ref, sin_ref = refs[:6]
        outs = refs[6:9]
        xs_ref, xn_ref, y_ref = refs[9:]
    tm = x_ref.shape[0]
    n_col = D_MODEL // LANES
    chunk = GROUP_CHUNK
    n_chunks = tm // chunk
    slabs_per_chunk = SLABS // n_chunks
    assert slabs_per_chunk * n_chunks == SLABS
    xn = _rms(x_ref[...], g_ref[...])
    for c in range(n_col):
        xs_ref[c] = xn[:, c * LANES:(c + 1) * LANES]

    def permute(gi, buf):
        d = DILATED_GROUPS[gi][1]
        rows = tm // d
        if d == 1:
            xn_ref[buf] = xn.astype(BF16)
        else:
            for r in range(d):
                for c in range(n_col):
                    xn_ref[buf, r * rows:(r + 1) * rows, c * LANES:(c + 1) * LANES] = (
                        xs_ref[c, pl.ds(r, rows, stride=d), :].astype(BF16))

    def norm_chunk(gi, buf, ci):
        d = DILATED_GROUPS[gi][1]
        rows = tm // d
        out_ref = outs[gi]
        r0 = pl.multiple_of(ci * chunk, chunk)

        def table_rows(t_ref):
            if d == 1:
                return t_ref[pl.ds(r0, chunk), :]
            if rows >= chunk:
                per = rows // chunk
                return t_ref[pl.ds((ci % per) * chunk * d + ci // per, chunk, stride=d), :]
            per = chunk // rows
            return jnp.concatenate([t_ref[pl.ds(ci * per + s, rows, stride=d), :] for s in range(per)], axis=0)

        cos = table_rows(cos_ref)
        sin = table_rows(sin_ref)
        for h in range(H_B):
            hs = slice(h * HEAD_DIM_B, (h + 1) * HEAD_DIM_B)
            sl = h * HEAD_DIM_B // SLAB_W
            lo = h * HEAD_DIM_B % SLAB_W
            yh = y_ref[buf, sl, pl.ds(r0, chunk), lo:lo + HEAD_DIM_B]
            rs = lax.rsqrt(jnp.mean(yh * yh, axis=-1, keepdims=True) + NORM_EPS) * scale
            yg = yh * hg_ref[:, gi * C_B + h * HEAD_DIM_B:gi * C_B + (h + 1) * HEAD_DIM_B]
            res = ((yg * cos + _rot_half(yg) * sin) * rs).astype(BF16)
            if rows >= chunk:
                per = rows // chunk
                out_ref[0, ci // per, pl.ds(pl.multiple_of((ci % per) * chunk, chunk), chunk), hs] = res
            else:
                per = chunk // rows
                for s in range(per):
                    out_ref[0, ci * per + s, :, hs] = res[s * rows:(s + 1) * rows]

    def v_slab(gi, buf, ci):
        d = DILATED_GROUPS[gi][1]
        rows = tm // d
        yv = _dot(xn_ref[buf], w_ref[(N_GROUPS + gi) * SLABS + ci])
        for r in range(d):
            outs[N_GROUPS + gi][0, r, ci] = yv[r * rows:(r + 1) * rows].astype(BF16)

    permute(0, 0)
    for s in range(SLABS):
        y_ref[0, s] = _dot(xn_ref[0], w_ref[s])
    for gi in range(N_GROUPS):
        buf = gi % 2
        if gi + 1 < N_GROUPS:
            permute(gi + 1, 1 - buf)

        def body(ci, carry, gi=gi, buf=buf):
            norm_chunk(gi, buf, ci)
            for sl in range(slabs_per_chunk):
                slab = ci * slabs_per_chunk + sl
                if gi + 1 < N_GROUPS:
                    y_ref[1 - buf, slab] = _dot(xn_ref[1 - buf], w_ref[(gi + 1) * SLABS + slab])
                if has_v:
                    v_slab(gi, buf, slab)
            return carry

        lax.fori_loop(0, n_chunks, body, 0)


def _group_proj(x2d, gain, w, head_gain, has_v, scale, cos_b, sin_b, batch, seq):
    tm = PROJ_TILE
    nt = seq // tm
    n_rope = N_GROUPS * C_B
    n_slabs = w.shape[1] // SLAB_W
    w_slabs = w.astype(BF16).reshape(D_MODEL, n_slabs, SLAB_W).transpose(1, 0, 2)
    in_specs = [pl.BlockSpec((tm, D_MODEL), lambda i: (i, 0)),
                pl.BlockSpec((1, D_MODEL), lambda i: (0, 0)),
                pl.BlockSpec((n_slabs, D_MODEL, SLAB_W), lambda i: (0, 0, 0), pipeline_mode=pl.Buffered(1)),
                pl.BlockSpec((1, n_rope), lambda i: (0, 0)),
                pl.BlockSpec((tm, LANES), lambda i: (i, 0)),
                pl.BlockSpec((tm, LANES), lambda i: (i, 0))]
    shapes, specs = [], []
    for _, d in DILATED_GROUPS:
        shapes.append(jax.ShapeDtypeStruct((batch, d, seq // d, C_B), BF16))
        specs.append(pl.BlockSpec((1, d, tm // d, C_B), lambda i: (i // nt, 0, i % nt, 0)))
    if has_v:
        for _, d in DILATED_GROUPS:
            shapes.append(jax.ShapeDtypeStruct((batch, d, SLABS, seq // d, SLAB_W), BF16))
            specs.append(pl.BlockSpec((1, d, SLABS, tm // d, SLAB_W), lambda i: (i // nt, 0, 0, i % nt, 0)))
    return pl.pallas_call(
        functools.partial(_group_proj_kernel, has_v=has_v, scale=scale),
        out_shape=tuple(shapes),
        grid=(batch * nt,),
        in_specs=in_specs,
        out_specs=tuple(specs),
        scratch_shapes=[pltpu.VMEM((D_MODEL // LANES, tm, LANES), F32), pltpu.VMEM((2, tm, D_MODEL), BF16),
                        pltpu.VMEM((2, SLABS, tm, SLAB_W), F32)],
        compiler_params=_params(("parallel",)),
        name="group_proj_kv" if has_v else "group_proj_q",
    )(x2d, gain.reshape(1, D_MODEL), w_slabs, head_gain.reshape(1, n_rope), cos_b, sin_b)


BAND_HEADS = 8


def _band_kernel(q_ref, k_ref, v_ref, o_ref, lse_ref, *scratch, dilation, length):
    o_scr = scratch[0] if dilation > 1 else None
    nk = min(2 * BAND, length)
    row = lax.broadcasted_iota(jnp.int32, (BAND, nk), 0)
    col = lax.broadcasted_iota(jnp.int32, (BAND, nk), 1)
    lane_head = lax.broadcasted_iota(jnp.int32, (BAND, LANES), 1) // (LANES // BAND_HEADS)

    def block(r, i):
        q0 = i * BAND
        start = jnp.maximum(q0 + BAND - nk, 0)
        dist = (q0 - start) + row - col
        valid = jnp.logical_and(dist >= 0, dist <= BAND)
        if not isinstance(q0, int):
            q0 = pl.multiple_of(q0, BAND)
            start = pl.multiple_of(start, BAND)
        scores = []
        for h in range(BAND_HEADS):
            hs = slice(h * HEAD_DIM_B, (h + 1) * HEAD_DIM_B)
            s = _dot_nt(q_ref[0, r, pl.ds(q0, BAND), hs], k_ref[0, r, pl.ds(start, nk), hs])
            scores.append(jnp.where(valid, s, NEG_INF))
        probs, denoms, lse_tile = [], [], jnp.zeros((BAND, LANES), F32)
        for h in range(BAND_HEADS):
            m = jnp.max(scores[h], axis=-1, keepdims=True)
            p = jnp.exp2(scores[h] - m)
            denom = jnp.sum(p, axis=-1, keepdims=True)
            probs.append(p.astype(BF16))
            denoms.append(denom)
            lse_tile = jnp.where(lane_head == h, m + jnp.log2(denom), lse_tile)
        tok = pl.ds(q0 * dilation + r, BAND, stride=dilation) if dilation > 1 else pl.ds(q0, BAND)
        for h in range(BAND_HEADS):
            hs = slice(h * HEAD_DIM_B, (h + 1) * HEAD_DIM_B)
            lo = h * HEAD_DIM_B % SLAB_W
            out = _dot(probs[h], v_ref[0, r, h * HEAD_DIM_B // SLAB_W, pl.ds(start, nk), lo:lo + HEAD_DIM_B])
            if dilation > 1:
                o_scr[h, tok, :] = out / denoms[h]
            else:
                o_ref[0, h, tok, :] = (out / denoms[h]).astype(o_ref.dtype)
        lse_ref[0, 0, tok, :] = lse_tile

    nb = length // BAND
    for r in range(dilation):
        if nb == 1:
            block(r, 0)
        else:
            def body(i, carry, r=r):
                block(r, i)
                return carry
            lax.fori_loop(0, nb, body, 0, unroll=4)
    if dilation > 1:
        for h in range(BAND_HEADS):
            o_ref[0, h] = o_scr[h].astype(o_ref.dtype)


def _band_attention(q, k, v, dilation, batch, seq):
    length = seq // dilation
    halves = H_B // BAND_HEADS
    cw = BAND_HEADS * HEAD_DIM_B
    blk = pl.BlockSpec((1, dilation, length, cw), lambda b, hh: (b, 0, 0, hh))
    v_blk = pl.BlockSpec((1, dilation, cw // SLAB_W, length, SLAB_W), lambda b, hh: (b, 0, hh, 0, 0))
    return pl.pallas_call(
        functools.partial(_band_kernel, dilation=dilation, length=length),
        out_shape=(jax.ShapeDtypeStruct((batch, H_B, seq, HEAD_DIM_B), BF16),
                   jax.ShapeDtypeStruct((batch, halves, seq, LANES), F32)),
        grid=(batch, halves),
        in_specs=[blk, blk, v_blk],
        out_specs=(pl.BlockSpec((1, BAND_HEADS, seq, HEAD_DIM_B), lambda b, hh: (b, hh, 0, 0)),
                   pl.BlockSpec((1, 1, seq, LANES), lambda b, hh: (b, hh, 0, 0))),
        scratch_shapes=[pltpu.VMEM((BAND_HEADS, seq, HEAD_DIM_B), F32)] if dilation > 1 else [],
        compiler_params=_params(("parallel", "parallel")),
        name=f"band_attn_d{dilation}",
    )(q, k, v)


def _combine_proj_kernel(o0_ref, o1_ref, o2_ref, l0_ref, l1_ref, l2_ref, w_ref, x_ref, y_ref, o_scr):
    lanes_per_head = LANES // BAND_HEADS
    for half in range(H_B // BAND_HEADS):
        l0, l1, l2 = l0_ref[0, half], l1_ref[0, half], l2_ref[0, half]
        mx = jnp.maximum(jnp.maximum(l0, l1), l2)
        e0, e1, e2 = jnp.exp2(l0 - mx), jnp.exp2(l1 - mx), jnp.exp2(l2 - mx)
        inv = 1.0 / (e0 + e1 + e2)
        w0, w1, w2 = e0 * inv, e1 * inv, e2 * inv
        for hq in range(BAND_HEADS):
            h = half * BAND_HEADS + hq
            c = hq * lanes_per_head
            o = (w0[:, c:c + 1] * o0_ref[0, h].astype(F32) + w1[:, c:c + 1] * o1_ref[0, h].astype(F32)
                 + w2[:, c:c + 1] * o2_ref[0, h].astype(F32))
            o_scr[:, h * HEAD_DIM_B:(h + 1) * HEAD_DIM_B] = o.astype(BF16)
    y_ref[...] = x_ref[...] + _dot(o_scr[...], w_ref[...])


def _combine_proj(outs, lses, w_o, x2d, seq):
    T = x2d.shape[0]
    tm = 1024
    nt = seq // tm
    halves = H_B // BAND_HEADS
    o_spec = pl.BlockSpec((1, H_B, tm, HEAD_DIM_B), lambda i: (i // nt, 0, i % nt, 0))
    l_spec = pl.BlockSpec((1, halves, tm, LANES), lambda i: (i // nt, 0, i % nt, 0))
    return pl.pallas_call(
        _combine_proj_kernel,
        out_shape=jax.ShapeDtypeStruct((T, D_MODEL), F32),
        grid=(T // tm,),
        in_specs=[o_spec, o_spec, o_spec, l_spec, l_spec, l_spec,
                  pl.BlockSpec((C_B, D_MODEL), lambda i: (0, 0)),
                  pl.BlockSpec((tm, D_MODEL), lambda i: (i, 0))],
        out_specs=pl.BlockSpec((tm, D_MODEL), lambda i: (i, 0)),
        scratch_shapes=[pltpu.VMEM((tm, C_B), BF16)],
        compiler_params=_params(("parallel",)),
        name="combine_proj",
    )(*outs, *lses, w_o.astype(BF16), x2d)


def kernel(x, positions, attn_norm, mlp_norm, mla_w_in, mla_qa_norm, mla_kva_norm, mla_w_qb, mla_w_kvb,
           mla_q_norm, mla_k_norm, mla_w_o, kv_norm, w_kv, k_norm_b, w_q_b, q_norm_b, w_o_b, mlp_w1, mlp_w2):
    B, S, D = x.shape
    T = B * S
    cos_a, sin_a, cos_b, sin_b = _rope_tables(positions)
    h = x.reshape(T, D)

    def head_gains(gn):
        return jnp.broadcast_to(gn[:, None, :], (N_GROUPS, H_B, HEAD_DIM_B))

    for a in range(N_A_LAYERS):
        q, k, v = _mla_proj(h, attn_norm[a], mla_w_in[a], mla_qa_norm[a], mla_kva_norm[a], mla_w_qb[a],
                            mla_w_kvb[a], mla_q_norm[a], mla_k_norm[a], cos_a, sin_a)
        o = _mla_attention(q, k, v, B, S)
        h = _out_proj(o, mla_w_o[a], h)
        h = _mlp(h, mlp_norm[a], mlp_w1[a], mlp_w2[a])

    kv = _group_proj(h, kv_norm, w_kv, head_gains(k_norm_b), True, 1.0, cos_b, sin_b, B, S)
    ks, vs = kv[:N_GROUPS], kv[N_GROUPS:]

    for b in range(N_B_LAYERS):
        layer = N_A_LAYERS + b
        qs = _group_proj(h, attn_norm[layer], w_q_b[b], head_gains(q_norm_b[b]), False,
                         HEAD_DIM_B ** -0.5 * LOG2E, cos_b, sin_b, B, S)
        outs, lses = [], []
        for g, (window, dilation) in enumerate(DILATED_GROUPS):
            assert window // dilation == BAND
            o, lse = _band_attention(qs[g], ks[g], vs[g], dilation, B, S)
            outs.append(o)
            lses.append(lse)
        h = _combine_proj(outs, lses, w_o_b[b], h, S)
        h = _mlp(h, mlp_norm[layer], mlp_w1[layer], mlp_w2[layer])

    return h.reshape(B, S, D)
```
